```python
import math
import jax, jax.numpy as jnp
from jax import lax
import numpy as np

D_MODEL = 2048
BATCH = 4
SEQ = 4096
DEPTH = 2

MIX_WIDTH = D_MODEL
ATTN_WIDTH = MIX_WIDTH // 2
SSM_WIDTH = MIX_WIDTH - ATTN_WIDTH
ATTN_HEAD_DIM = 128
ATTN_HEADS = ATTN_WIDTH // ATTN_HEAD_DIM
MOBA_BLOCK = 256
MOBA_TOPK = 3
Q_CHUNK = 32
SSM_HEAD_DIM = 64
SSM_HEADS = SSM_WIDTH // SSM_HEAD_DIM
SSM_GROUPS = 2
SSM_HEADS_PER_GROUP = SSM_HEADS // SSM_GROUPS
SSM_STATE = 128
SSM_CONV = 4
SSD_CHUNK = 128
XBC_WIDTH = SSM_WIDTH + 2 * SSM_GROUPS * SSM_STATE
IN_COLS = 3 * ATTN_WIDTH + SSM_WIDTH + XBC_WIDTH + SSM_HEADS
SPLIT_POINTS = (ATTN_WIDTH, 2 * ATTN_WIDTH, 3 * ATTN_WIDTH,
                3 * ATTN_WIDTH + SSM_WIDTH, 3 * ATTN_WIDTH + SSM_WIDTH + XBC_WIDTH)
D_FF = 5632
FFN_CONV = 3
EPS = 1e-6

kernel_name = "hymba_moba_ssd_convglu"


def rms_norm(x, w):
    xf = x.astype(jnp.float32)
    y = xf * lax.rsqrt(jnp.mean(xf * xf, axis=-1, keepdims=True) + EPS)
    return (y * w.astype(jnp.float32)).astype(x.dtype)


def causal_dwconv(u, w, bias):
    k_width = w.shape[0]
    s = u.shape[1]
    up = jnp.pad(u, ((0, 0), (k_width - 1, 0), (0, 0)))
    out = bias + up[:, 0:s] * w[0]
    for j in range(1, k_width):
        out = out + up[:, j:j + s] * w[j]
    return out


def moba_attention(q, k, v):
    b, h, s, dh = q.shape
    nb = -(-s // MOBA_BLOCK)
    pad = nb * MOBA_BLOCK - s
    kb = jnp.pad(k, ((0, 0), (0, 0), (0, pad), (0, 0))).reshape(b, h, nb, MOBA_BLOCK, dh)
    vb = jnp.pad(v, ((0, 0), (0, 0), (0, pad), (0, 0))).reshape(b, h, nb, MOBA_BLOCK, dh)
    k_mean = jnp.mean(kb, axis=3)
    n_sel = min(MOBA_TOPK, nb - 1)
    scale = dh ** -0.5
    blk_ids = jnp.arange(nb, dtype=jnp.int32)
    key_off = jnp.arange(MOBA_BLOCK, dtype=jnp.int32)
    gather_blocks = jax.vmap(jax.vmap(lambda blocks, idx: blocks[idx]))
    nq = s // Q_CHUNK
    q_chunks = q.reshape(b, h, nq, Q_CHUNK, dh).transpose(2, 0, 1, 3, 4)

    def chunk_attend(args):
        qc, c = args
        q_pos = c * Q_CHUNK + jnp.arange(Q_CHUNK, dtype=jnp.int32)
        own = (c * Q_CHUNK) // MOBA_BLOCK
        k_own = lax.dynamic_index_in_dim(kb, own, axis=2, keepdims=False)
        v_own = lax.dynamic_index_in_dim(vb, own, axis=2, keepdims=False)
        k_pos = own * MOBA_BLOCK + key_off
        s_own = jnp.einsum('bhqd,bhkd->bhqk', qc, k_own).astype(jnp.float32) * scale
        s_own = jnp.where(k_pos[None, :] <= q_pos[:, None], s_own, -jnp.inf)
        if n_sel == 0:
            p_own = jax.nn.softmax(s_own, axis=-1).astype(v.dtype)
            return jnp.einsum('bhqk,bhkd->bhqd', p_own, v_own)
        gate = jnp.einsum('bhqd,bhnd->bhqn', qc, k_mean).astype(jnp.float32)
        gate = jnp.where(blk_ids < own, gate, -jnp.inf)
        _, sel = lax.top_k(gate, n_sel)
        valid = sel < own
        k_sel = gather_blocks(kb, sel)
        v_sel = gather_blocks(vb, sel)
        s_sel = jnp.einsum('bhqd,bhqjkd->bhqjk', qc, k_sel).astype(jnp.float32) * scale
        s_sel = jnp.where(valid[..., None], s_sel, -jnp.inf).reshape(b, h, Q_CHUNK, n_sel * MOBA_BLOCK)
        p = jax.nn.softmax(jnp.concatenate([s_sel, s_own], axis=-1), axis=-1).astype(v.dtype)
        p_sel = p[..., :n_sel * MOBA_BLOCK].reshape(b, h, Q_CHUNK, n_sel, MOBA_BLOCK)
        p_own = p[..., n_sel * MOBA_BLOCK:]
        return (jnp.einsum('bhqjk,bhqjkd->bhqd', p_sel, v_sel)
                + jnp.einsum('bhqk,bhkd->bhqd', p_own, v_own))

    out = lax.map(chunk_attend, (q_chunks, jnp.arange(nq, dtype=jnp.int32)))
    return out.transpose(1, 2, 0, 3, 4).reshape(b, h, s, dh)


def ssd_chunked_scan(xs, dt, a, bm, cm):
    b, s, g, r, p = xs.shape
    n = bm.shape[-1]
    nc = s // SSD_CHUNK
    L = SSD_CHUNK
    log_a = (dt * a).reshape(b, nc, L, g, r).transpose(0, 3, 4, 1, 2)
    xdt = (xs * dt[..., None]).reshape(b, nc, L, g, r, p)
    bc = bm.reshape(b, nc, L, g, n)
    cc = cm.reshape(b, nc, L, g, n)
    a_cum = jnp.cumsum(log_a, axis=-1)
    tril = jnp.tril(jnp.ones((L, L), dtype=bool))
    seg = a_cum[..., :, None] - a_cum[..., None, :]
    decay = jnp.where(tril, jnp.exp(jnp.where(tril, seg, 0.0)), 0.0)
    cb = jnp.einsum('bclgn,bcsgn->bgcls', cc, bc)
    y_diag = jnp.einsum('bgrcls,bcsgrp->bclgrp', cb[:, :, None] * decay, xdt)
    decay_to_end = jnp.exp(a_cum[..., -1:] - a_cum)
    chunk_states = jnp.einsum('bclgn,bgrcl,bclgrp->cbgrpn', bc, decay_to_end, xdt)
    chunk_decay = jnp.exp(a_cum[..., -1]).transpose(3, 0, 1, 2)

    def carry_state(h, inputs):
        st, dec = inputs
        return h * dec[..., None, None] + st, h

    h0 = jnp.zeros((b, g, r, p, n), xs.dtype)
    _, h_in = lax.scan(carry_state, h0, (chunk_states, chunk_decay))
    y_off = jnp.einsum('bclgn,cbgrpn,bgrcl->bclgrp', cc, h_in, jnp.exp(a_cum))
    return (y_diag + y_off).reshape(b, s, g, r, p)


def hybrid_layer(x, norm1_w, w_in, q_norm_w, k_norm_w, ssm_conv_w, ssm_conv_b, dt_bias, a_log,
                 d_skip, ssm_norm_w, w_out, norm2_w, w_up, ffn_conv_w, ffn_conv_b, w_down):
    b, s, _ = x.shape
    h = rms_norm(x, norm1_w)
    proj = h @ w_in
    q, k, v, z, xbc, dt_raw = jnp.split(proj, SPLIT_POINTS, axis=-1)

    q = rms_norm(q.reshape(b, s, ATTN_HEADS, ATTN_HEAD_DIM), q_norm_w).transpose(0, 2, 1, 3)
    k = rms_norm(k.reshape(b, s, ATTN_HEADS, ATTN_HEAD_DIM), k_norm_w).transpose(0, 2, 1, 3)
    v = v.reshape(b, s, ATTN_HEADS, ATTN_HEAD_DIM).transpose(0, 2, 1, 3)
    attn = moba_attention(q, k, v).transpose(0, 2, 1, 3).reshape(b, s, ATTN_WIDTH)

    xbc = jax.nn.silu(causal_dwconv(xbc, ssm_conv_w, ssm_conv_b))
    xs, bm, cm = jnp.split(xbc, (SSM_WIDTH, SSM_WIDTH + SSM_GROUPS * SSM_STATE), axis=-1)
    xs = xs.astype(jnp.float32).reshape(b, s, SSM_GROUPS, SSM_HEADS_PER_GROUP, SSM_HEAD_DIM)
    bm = bm.astype(jnp.float32).reshape(b, s, SSM_GROUPS, SSM_STATE)
    cm = cm.astype(jnp.float32).reshape(b, s, SSM_GROUPS, SSM_STATE)
    dt = jax.nn.softplus(dt_raw.astype(jnp.float32) + dt_bias.astype(jnp.float32))
    dt = dt.reshape(b, s, SSM_GROUPS, SSM_HEADS_PER_GROUP)
    a = -jnp.exp(a_log.astype(jnp.float32)).reshape(SSM_GROUPS, SSM_HEADS_PER_GROUP)
    y = ssd_chunked_scan(xs, dt, a, bm, cm)
    y = y + d_skip.astype(jnp.float32).reshape(SSM_GROUPS, SSM_HEADS_PER_GROUP)[:, :, None] * xs
    gate = jax.nn.silu(z.astype(jnp.float32)).reshape(b, s, SSM_GROUPS, SSM_HEADS_PER_GROUP * SSM_HEAD_DIM)
    y = y.reshape(b, s, SSM_GROUPS, SSM_HEADS_PER_GROUP * SSM_HEAD_DIM) * gate
    y = rms_norm(y, ssm_norm_w.reshape(SSM_GROUPS, SSM_HEADS_PER_GROUP * SSM_HEAD_DIM))
    ssm = y.reshape(b, s, SSM_WIDTH).astype(x.dtype)

    x = x + jnp.concatenate([attn, ssm], axis=-1) @ w_out

    h = rms_norm(x, norm2_w)
    u = causal_dwconv(h @ w_up, ffn_conv_w, ffn_conv_b)
    u_gate, u_val = jnp.split(u, 2, axis=-1)
    return x + (jax.nn.silu(u_gate) * u_val) @ w_down


def setup_inputs(seed: int = 0) -> dict:
    key = jax.random.key(seed)
    ks = jax.random.split(key, 17)

    def nrm(k, shape, scale):
        return jax.random.normal(k, shape, jnp.float32) * scale

    x = nrm(ks[0], (BATCH, SEQ, D_MODEL), 1.0)
    norm1_w = 1.0 + nrm(ks[1], (DEPTH, D_MODEL), 0.02)
    w_in = nrm(ks[2], (DEPTH, D_MODEL, IN_COLS), D_MODEL ** -0.5)
    q_norm_w = 1.0 + nrm(ks[3], (DEPTH, ATTN_HEAD_DIM), 0.02)
    k_norm_w = 1.0 + nrm(ks[4], (DEPTH, ATTN_HEAD_DIM), 0.02)
    ssm_conv_w = nrm(ks[5], (DEPTH, SSM_CONV, XBC_WIDTH), SSM_CONV ** -0.5)
    ssm_conv_b = nrm(ks[6], (DEPTH, XBC_WIDTH), 0.02)
    dt0 = jnp.exp(jax.random.uniform(ks[7], (DEPTH, SSM_HEADS), jnp.float32,
                                     minval=math.log(1e-3), maxval=math.log(1e-1)))
    dt_bias = dt0 + jnp.log(-jnp.expm1(-dt0))
    a_log = jnp.log(jax.random.uniform(ks[8], (DEPTH, SSM_HEADS), jnp.float32, minval=1.0, maxval=16.0))
    d_skip = 1.0 + nrm(ks[9], (DEPTH, SSM_HEADS), 0.1)
    ssm_norm_w = 1.0 + nrm(ks[10], (DEPTH, SSM_WIDTH), 0.02)
    w_out = nrm(ks[11], (DEPTH, MIX_WIDTH, D_MODEL), MIX_WIDTH ** -0.5)
    norm2_w = 1.0 + nrm(ks[12], (DEPTH, D_MODEL), 0.02)
    w_up = nrm(ks[13], (DEPTH, D_MODEL, 2 * D_FF), D_MODEL ** -0.5)
    ffn_conv_w = nrm(ks[14], (DEPTH, FFN_CONV, 2 * D_FF), FFN_CONV ** -0.5)
    ffn_conv_b = nrm(ks[15], (DEPTH, 2 * D_FF), 0.02)
    w_down = nrm(ks[16], (DEPTH, D_FF, D_MODEL), D_FF ** -0.5)
    return {"x": x, "norm1_w": norm1_w, "w_in": w_in, "q_norm_w": q_norm_w, "k_norm_w": k_norm_w,
            "ssm_conv_w": ssm_conv_w, "ssm_conv_b": ssm_conv_b, "dt_bias": dt_bias, "a_log": a_log,
            "d_skip": d_skip, "ssm_norm_w": ssm_norm_w, "w_out": w_out, "norm2_w": norm2_w,
            "w_up": w_up, "ffn_conv_w": ffn_conv_w, "ffn_conv_b": ffn_conv_b, "w_down": w_down}


def reference(x, norm1_w, w_in, q_norm_w, k_norm_w, ssm_conv_w, ssm_conv_b, dt_bias, a_log,
              d_skip, ssm_norm_w, w_out, norm2_w, w_up, ffn_conv_w, ffn_conv_b, w_down):
    for i in range(DEPTH):
        x = hybrid_layer(x, norm1_w[i], w_in[i], q_norm_w[i], k_norm_w[i], ssm_conv_w[i],
                         ssm_conv_b[i], dt_bias[i], a_log[i], d_skip[i], ssm_norm_w[i], w_out[i],
                         norm2_w[i], w_up[i], ffn_conv_w[i], ffn_conv_b[i], w_down[i])
    return x
```

```python
import functools

import jax
import jax.numpy as jnp
from jax import lax
from jax.experimental import pallas as pl
from jax.experimental.pallas import tpu as pltpu

F32 = jnp.float32
BF16 = jnp.bfloat16

D_MODEL = 2048
ATTN_WIDTH = 1024
ATTN_HEAD_DIM = 128
ATTN_HEADS = ATTN_WIDTH // ATTN_HEAD_DIM
MOBA_BLOCK = 256
MOBA_TOPK = 3
SSM_WIDTH = 1024
SSM_HEAD_DIM = 64
SSM_HEADS = SSM_WIDTH // SSM_HEAD_DIM
SSM_GROUPS = 2
SSM_HEADS_PER_GROUP = SSM_HEADS // SSM_GROUPS
SSM_GROUP_WIDTH = SSM_WIDTH // SSM_GROUPS
SSM_STATE = 128
SSM_CONV = 4
XBC_WIDTH = SSM_WIDTH + 2 * SSM_GROUPS * SSM_STATE
PROJ_MAIN = 3 * ATTN_WIDTH + SSM_WIDTH + XBC_WIDTH
D_FF = 5632
FFN_CONV = 3
EPS = 1e-6

LANES = 128
SUBLANES = 8
VMEM_LIMIT = 56 * 1024 * 1024

SSD_CHUNK = 256
ROW_TILE = 512
COL_TILE = 512
FFN_HALO = 16
NORM_ROWS = 64

_NT = (((1,), (1,)), ((), ()))


def _rms(x, w):
    return x * lax.rsqrt(jnp.mean(x * x, axis=-1, keepdims=True) + EPS) * w


def _norm_rows(x_ref, nw_ref, h_ref, dst_off, n_rows):
    def body(c, carry):
        r = pl.multiple_of(c * NORM_ROWS, NORM_ROWS)
        h_ref[pl.ds(dst_off + r, NORM_ROWS), :] = _rms(x_ref[pl.ds(r, NORM_ROWS), :], nw_ref[...]).astype(BF16)
        return carry
    lax.fori_loop(0, n_rows // NORM_ROWS, body, 0)


def _in_proj_kernel(x_ref, nw_ref, w_ref, wdt_ref, o_ref, dt_ref, h_ref):
    @pl.when(pl.program_id(1) == 0)
    def _():
        _norm_rows(x_ref, nw_ref, h_ref, 0, x_ref.shape[0])
        dt_ref[...] = jnp.dot(h_ref[...], wdt_ref[...], preferred_element_type=F32)
    o_ref[...] = jnp.dot(h_ref[...], w_ref[...], preferred_element_type=F32)


def _in_proj(x2d, nw, w_main, w_dt):
    t, d = x2d.shape
    n = w_main.shape[1]
    return pl.pallas_call(
        _in_proj_kernel,
        grid=(t // ROW_TILE, n // COL_TILE),
        in_specs=[
            pl.BlockSpec((ROW_TILE, d), lambda i, j: (i, 0)),
            pl.BlockSpec((1, d), lambda i, j: (0, 0)),
            pl.BlockSpec((d, COL_TILE), lambda i, j: (0, j)),
            pl.BlockSpec((d, LANES), lambda i, j: (0, 0)),
        ],
        out_specs=[
            pl.BlockSpec((ROW_TILE, COL_TILE), lambda i, j: (i, j)),
            pl.BlockSpec((ROW_TILE, LANES), lambda i, j: (i, 0)),
        ],
        out_shape=[jax.ShapeDtypeStruct((t, n), F32), jax.ShapeDtypeStruct((t, LANES), F32)],
        scratch_shapes=[pltpu.VMEM((ROW_TILE, d), BF16)],
        compiler_params=pltpu.CompilerParams(
            dimension_semantics=("parallel", "arbitrary"), vmem_limit_bytes=VMEM_LIMIT),
        name="in_proj",
    )(x2d, nw, w_main, w_dt)


def _moba_kernel(q_ref, k_ref, v_ref, qw_ref, kw_ref, o_ref, qf_s, qb_s, k_s, vt_s, km_s, sel_s, *, nb):
    blk = MOBA_BLOCK
    scale = ATTN_HEAD_DIM ** -0.5
    for j in range(nb):
        rows = slice(j * blk, (j + 1) * blk)
        kn = _rms(k_ref[rows, :], kw_ref[...])
        k_s[j] = kn.astype(BF16)
        km_s[j:j + 1, :] = jnp.mean(kn, axis=0, keepdims=True)
        vt_s[j] = v_ref[rows, :].T.astype(BF16)
        qn = _rms(q_ref[rows, :], qw_ref[...]) * scale
        qf_s[j] = qn
        qb_s[j] = qn.astype(BF16)

    key_i = lax.broadcasted_iota(jnp.int32, (blk, blk), 0)
    qry_i = lax.broadcasted_iota(jnp.int32, (blk, blk), 1)
    bid = lax.broadcasted_iota(jnp.int32, (nb, blk), 0)

    def qblock(i, carry):
        qb = qb_s[i]
        gate = lax.dot_general(km_s[...], qf_s[i], _NT, precision=lax.Precision.HIGHEST,
                               preferred_element_type=F32)
        past = bid < i
        g = jnp.where(past, gate, -jnp.inf)
        rank = jnp.zeros((nb, blk), jnp.int32)
        for jp in range(nb):
            row = g[jp:jp + 1, :]
            beats = jnp.where(row > g, 1, jnp.where(row == g, jnp.where(bid > jp, 1, 0), 0))
            rank = rank + beats
        sel = jnp.where(past, jnp.where(rank < MOBA_TOPK, 1.0, 0.0), 0.0)
        for jp in range(nb):
            sel_s[jp] = sel[jp:jp + 1, :]

        s = lax.dot_general(k_s[i], qb, _NT, preferred_element_type=F32)
        s = jnp.where(key_i <= qry_i, s, -jnp.inf)
        m = jnp.max(s, axis=0, keepdims=True)
        p = jnp.exp(s - m)
        l = jnp.sum(p, axis=0, keepdims=True)
        acc = jnp.dot(vt_s[i], p.astype(BF16), preferred_element_type=F32)

        def past_block(j, mla):
            m, l, acc = mla
            s = lax.dot_general(k_s[j], qb, _NT, preferred_element_type=F32)
            s = jnp.where(sel_s[j] > 0.0, s, -jnp.inf)
            m_new = jnp.maximum(m, jnp.max(s, axis=0, keepdims=True))
            alpha = jnp.exp(m - m_new)
            p = jnp.exp(s - m_new)
            l = alpha * l + jnp.sum(p, axis=0, keepdims=True)
            acc = alpha * acc + jnp.dot(vt_s[j], p.astype(BF16), preferred_element_type=F32)
            return m_new, l, acc

        m, l, acc = lax.fori_loop(0, i, past_block, (m, l, acc))
        o_ref[pl.ds(pl.multiple_of(i * blk, blk), blk), :] = (acc / l).T.astype(o_ref.dtype)
        return carry

    lax.fori_loop(0, nb, qblock, 0)


def _moba(proj3, qw, kw):
    b, s, _ = proj3.shape
    nb = s // MOBA_BLOCK
    dh = ATTN_HEAD_DIM
    blk = MOBA_BLOCK
    return pl.pallas_call(
        functools.partial(_moba_kernel, nb=nb),
        grid=(b, ATTN_HEADS),
        in_specs=[
            pl.BlockSpec((None, s, dh), lambda bi, h: (bi, 0, h)),
            pl.BlockSpec((None, s, dh), lambda bi, h: (bi, 0, ATTN_HEADS + h)),
            pl.BlockSpec((None, s, dh), lambda bi, h: (bi, 0, 2 * ATTN_HEADS + h)),
            pl.BlockSpec((1, dh), lambda bi, h: (0, 0)),
            pl.BlockSpec((1, dh), lambda bi, h: (0, 0)),
        ],
        out_specs=pl.BlockSpec((None, s, dh), lambda bi, h: (bi, 0, h)),
        out_shape=jax.ShapeDtypeStruct((b, s, ATTN_WIDTH), BF16),
        scratch_shapes=[
            pltpu.VMEM((nb, blk, dh), F32),
            pltpu.VMEM((nb, blk, dh), BF16),
            pltpu.VMEM((nb, blk, dh), BF16),
            pltpu.VMEM((nb, dh, blk), BF16),
            pltpu.VMEM((nb, dh), F32),
            pltpu.VMEM((nb, 1, blk), F32),
        ],
        compiler_params=pltpu.CompilerParams(
            dimension_semantics=("parallel", "parallel"), vmem_limit_bytes=VMEM_LIMIT),
        name="moba",
    )(proj3, proj3, proj3, qw, kw)


def _pair_cols(arr, i0):
    rows = arr.shape[0]
    lo = lax.broadcasted_iota(jnp.int32, (rows, LANES), 1) < SSM_HEAD_DIM
    a0 = jnp.broadcast_to(arr[:, i0:i0 + 1], (rows, LANES))
    a1 = jnp.broadcast_to(arr[:, i0 + 1:i0 + 2], (rows, LANES))
    return jnp.where(lo, a0, a1)


def _ssd_kernel(xs_ref, bc_ref, z_ref, dt_ref, cw_ref, cb_ref, dtb_ref, alog_ref, dsk_ref, nw_ref, o_ref,
                ext_s, xbc_s, ht_s, y_s, wx_s, dec_s):
    L = xs_ref.shape[0]
    pad = SUBLANES

    @pl.when(pl.program_id(1) == 0)
    def _():
        ext_s[0:pad, :] = jnp.zeros((pad, XBC_WIDTH), F32)
        ht_s[...] = jnp.zeros(ht_s.shape, F32)

    ext_s[pad:pad + L, 0:SSM_WIDTH] = xs_ref[...]
    ext_s[pad:pad + L, SSM_WIDTH:XBC_WIDTH] = bc_ref[...]
    for cblk in range(XBC_WIDTH // LANES):
        cols = slice(cblk * LANES, (cblk + 1) * LANES)
        conv = cb_ref[:, cols] + cw_ref[0:1, cols] * ext_s[pad - 3:pad - 3 + L, cols]
        for j in range(1, SSM_CONV):
            conv = conv + cw_ref[j:j + 1, cols] * ext_s[pad - 3 + j:pad - 3 + j + L, cols]
        xbc_s[:, cols] = conv * jax.nn.sigmoid(conv)
    ext_s[0:pad, :] = ext_s[L:L + pad, :]

    dtv = dt_ref[...] + dtb_ref[...]
    dt = jnp.maximum(dtv, 0.0) + jnp.log1p(jnp.exp(-jnp.abs(dtv)))
    la = dt * (-jnp.exp(alog_ref[...]))
    row_i = lax.broadcasted_iota(jnp.int32, (L, L), 0)
    col_i = lax.broadcasted_iota(jnp.int32, (L, L), 1)
    tril = row_i >= col_i
    acol = jnp.dot(jnp.where(tril, 1.0, 0.0), la, precision=lax.Precision.HIGHEST,
                   preferred_element_type=F32)
    arow = acol.T
    ecol = jnp.exp(acol)
    aend = acol[L - 1:L, :]
    wcol = jnp.exp(aend - acol)
    eend = jnp.exp(aend)
    lo = lax.broadcasted_iota(jnp.int32, (L, LANES), 1) < SSM_HEAD_DIM

    for g in range(SSM_GROUPS):
        b_off = SSM_WIDTH + g * SSM_STATE
        c_off = SSM_WIDTH + SSM_GROUPS * SSM_STATE + g * SSM_STATE
        bg = xbc_s[:, b_off:b_off + SSM_STATE]
        cg = xbc_s[:, c_off:c_off + SSM_STATE].astype(BF16)
        cb = lax.dot_general(cg, bg.astype(BF16), _NT, preferred_element_type=F32)
        cbm = jnp.where(tril, cb, 0.0)
        ht = ht_s[g]
        ch = jnp.dot(cg, ht.astype(BF16), preferred_element_type=F32)
        for pr in range(SSM_HEADS_PER_GROUP // 2):
            i0 = g * SSM_HEADS_PER_GROUP + 2 * pr
            lanes = slice(i0 * SSM_HEAD_DIM, i0 * SSM_HEAD_DIM + LANES)
            gl = slice(pr * LANES, (pr + 1) * LANES)
            xs_p = xbc_s[:, lanes]
            xdt = xs_p * _pair_cols(dt, i0)
            yd = None
            for hh in range(2):
                idx = i0 + hh
                seg = acol[:, idx:idx + 1] - arow[idx:idx + 1, :]
                mm = (cbm * jnp.exp(jnp.minimum(seg, 0.0))).astype(BF16)
                xh = jnp.where(lo, xdt, 0.0) if hh == 0 else jnp.where(lo, 0.0, xdt)
                part = jnp.dot(mm, xh.astype(BF16), preferred_element_type=F32)
                yd = part if yd is None else yd + part
            y_off = ch[:, gl] * _pair_cols(ecol, i0)
            y_s[:, lanes] = yd + y_off + dsk_ref[:, lanes] * xs_p
            wx_s[:, gl] = (xdt * _pair_cols(wcol, i0)).astype(BF16)
            dec_s[:, gl] = _pair_cols(eend, i0)
        ht_s[g] = ht * dec_s[...] + jnp.dot(bg.T.astype(BF16), wx_s[...], preferred_element_type=F32)

    for g in range(SSM_GROUPS):
        cols = slice(g * SSM_GROUP_WIDTH, (g + 1) * SSM_GROUP_WIDTH)
        zz = z_ref[:, cols]
        yg = y_s[:, cols] * (zz * jax.nn.sigmoid(zz))
        o_ref[:, cols] = _rms(yg, nw_ref[:, cols]).astype(o_ref.dtype)


def _ssd(proj3, dt3, conv_w, conv_b, dt_bias, a_log, d_skip, norm_w):
    b, s, _ = proj3.shape
    L = SSD_CHUNK
    full = lambda shape: pl.BlockSpec(shape, lambda bi, c: (0,) * len(shape))
    return pl.pallas_call(
        _ssd_kernel,
        grid=(b, s // L),
        in_specs=[
            pl.BlockSpec((None, L, SSM_WIDTH), lambda bi, c: (bi, c, (3 * ATTN_WIDTH + SSM_WIDTH) // SSM_WIDTH)),
            pl.BlockSpec((None, L, XBC_WIDTH - SSM_WIDTH),
                         lambda bi, c: (bi, c, (3 * ATTN_WIDTH + 2 * SSM_WIDTH) // (XBC_WIDTH - SSM_WIDTH))),
            pl.BlockSpec((None, L, SSM_WIDTH), lambda bi, c: (bi, c, 3 * ATTN_WIDTH // SSM_WIDTH)),
            pl.BlockSpec((None, L, LANES), lambda bi, c: (bi, c, 0)),
            full((SSM_CONV, XBC_WIDTH)),
            full((1, XBC_WIDTH)),
            full((1, LANES)),
            full((1, LANES)),
            full((1, SSM_WIDTH)),
            full((1, SSM_WIDTH)),
        ],
        out_specs=pl.BlockSpec((None, L, SSM_WIDTH), lambda bi, c: (bi, c, 0)),
        out_shape=jax.ShapeDtypeStruct((b, s, SSM_WIDTH), BF16),
        scratch_shapes=[
            pltpu.VMEM((L + 2 * SUBLANES, XBC_WIDTH), F32),
            pltpu.VMEM((L, XBC_WIDTH), F32),
            pltpu.VMEM((SSM_GROUPS, SSM_STATE, SSM_GROUP_WIDTH), F32),
            pltpu.VMEM((L, SSM_WIDTH), F32),
            pltpu.VMEM((L, SSM_GROUP_WIDTH), BF16),
            pltpu.VMEM((1, SSM_GROUP_WIDTH), F32),
        ],
        compiler_params=pltpu.CompilerParams(
            dimension_semantics=("parallel", "arbitrary"), vmem_limit_bytes=VMEM_LIMIT),
        name="ssd",
    )(proj3, proj3, proj3, dt3, conv_w, conv_b, dt_bias, a_log, d_skip, norm_w)


def _out_proj_kernel(x_ref, a_ref, s_ref, wa_ref, ws_ref, o_ref):
    o_ref[...] = (x_ref[...]
                  + jnp.dot(a_ref[...], wa_ref[...], preferred_element_type=F32)
                  + jnp.dot(s_ref[...], ws_ref[...], preferred_element_type=F32))


def _out_proj(x2d, attn2d, ssm2d, w_attn, w_ssm):
    t, d = x2d.shape
    return pl.pallas_call(
        _out_proj_kernel,
        grid=(t // ROW_TILE, d // COL_TILE),
        in_specs=[
            pl.BlockSpec((ROW_TILE, COL_TILE), lambda i, j: (i, j)),
            pl.BlockSpec((ROW_TILE, ATTN_WIDTH), lambda i, j: (i, 0)),
            pl.BlockSpec((ROW_TILE, SSM_WIDTH), lambda i, j: (i, 0)),
            pl.BlockSpec((ATTN_WIDTH, COL_TILE), lambda i, j: (0, j)),
            pl.BlockSpec((SSM_WIDTH, COL_TILE), lambda i, j: (0, j)),
        ],
        out_specs=pl.BlockSpec((ROW_TILE, COL_TILE), lambda i, j: (i, j)),
        out_shape=jax.ShapeDtypeStruct((t, d), F32),
        compiler_params=pltpu.CompilerParams(
            dimension_semantics=("parallel", "parallel"), vmem_limit_bytes=VMEM_LIMIT),
        name="out_proj",
    )(x2d, attn2d, ssm2d, w_attn, w_ssm)


def _conv_glu_kernel(x_ref, halo_ref, xres_ref, nw_ref, wg_ref, wv_ref, cwg_ref, cwv_ref, cbg_ref, cbv_ref,
                     wd_ref, o_ref, h_s, ug_s, uv_s, act_s, *, tiles_per_seq, n_up):
    tm = x_ref.shape[0]
    tf = wg_ref.shape[1]
    i = pl.program_id(0)
    f = pl.program_id(1)

    @pl.when(f == 0)
    def _():
        _norm_rows(x_ref, nw_ref, h_s, FFN_HALO, tm)
        keep = jnp.where(i % tiles_per_seq == 0, 0.0, 1.0)
        h_s[0:FFN_HALO, :] = (_rms(halo_ref[...], nw_ref[...]) * keep).astype(BF16)

    @pl.when(f < n_up)
    def _():
        ug_s[...] = jnp.dot(h_s[...], wg_ref[...], preferred_element_type=F32)
        uv_s[...] = jnp.dot(h_s[...], wv_ref[...], preferred_element_type=F32)

        def conv(u_s, cw_ref, cb_ref):
            out = cb_ref[...] + cw_ref[0:1, :] * u_s[FFN_HALO - 2:FFN_HALO - 2 + tm, :]
            for j in range(1, FFN_CONV):
                out = out + cw_ref[j:j + 1, :] * u_s[FFN_HALO - 2 + j:FFN_HALO - 2 + j + tm, :]
            return out

        gate = conv(ug_s, cwg_ref, cbg_ref)
        val = conv(uv_s, cwv_ref, cbv_ref)
        act_s[f] = (gate * jax.nn.sigmoid(gate) * val).astype(BF16)

    @pl.when(f >= n_up)
    def _():
        acc = xres_ref[...]
        for c in range(n_up):
            acc = acc + jnp.dot(act_s[c], wd_ref[c * tf:(c + 1) * tf, :], preferred_element_type=F32)
        o_ref[...] = acc


def _conv_glu(x2d, nw, w_up, conv_w, conv_b, w_down, seq_len):
    t, d = x2d.shape
    tm, tf, tn = ROW_TILE, COL_TILE, COL_TILE
    n_up = D_FF // tf
    n_down = d // tn
    halo_blocks = tm // FFN_HALO
    up = lambda f: jnp.minimum(f, n_up - 1)
    down = lambda f: jnp.maximum(f - n_up, 0)
    return pl.pallas_call(
        functools.partial(_conv_glu_kernel, tiles_per_seq=seq_len // tm, n_up=n_up),
        grid=(t // tm, n_up + n_down),
        in_specs=[
            pl.BlockSpec((tm, d), lambda i, f: (i, 0)),
            pl.BlockSpec((FFN_HALO, d), lambda i, f: (jnp.maximum(i * halo_blocks - 1, 0), 0)),
            pl.BlockSpec((tm, tn), lambda i, f: (i, down(f))),
            pl.BlockSpec((1, d), lambda i, f: (0, 0)),
            pl.BlockSpec((d, tf), lambda i, f: (0, up(f))),
            pl.BlockSpec((d, tf), lambda i, f: (0, n_up + up(f))),
            pl.BlockSpec((FFN_CONV, tf), lambda i, f: (0, up(f))),
            pl.BlockSpec((FFN_CONV, tf), lambda i, f: (0, n_up + up(f))),
            pl.BlockSpec((1, tf), lambda i, f: (0, up(f))),
            pl.BlockSpec((1, tf), lambda i, f: (0, n_up + up(f))),
            pl.BlockSpec((D_FF, tn), lambda i, f: (0, down(f))),
        ],
        out_specs=pl.BlockSpec((tm, tn), lambda i, f: (i, down(f))),
        out_shape=jax.ShapeDtypeStruct((t, d), F32),
        scratch_shapes=[
            pltpu.VMEM((FFN_HALO + tm, d), BF16),
            pltpu.VMEM((FFN_HALO + tm, tf), F32),
            pltpu.VMEM((FFN_HALO + tm, tf), F32),
            pltpu.VMEM((n_up, tm, tf), BF16),
        ],
        compiler_params=pltpu.CompilerParams(
            dimension_semantics=("parallel", "arbitrary"), vmem_limit_bytes=VMEM_LIMIT),
        name="conv_glu",
    )(x2d, x2d, x2d, nw, w_up, w_up, conv_w, conv_w, conv_b, conv_b, w_down)


def _pad_lanes(v):
    return jnp.pad(v.reshape(1, -1), ((0, 0), (0, LANES - v.shape[-1])))


def _layer(x2d, batch, seq, norm1_w, w_in, q_norm_w, k_norm_w, ssm_conv_w, ssm_conv_b, dt_bias, a_log,
           d_skip, ssm_norm_w, w_out, norm2_w, w_up, ffn_conv_w, ffn_conv_b, w_down):
    w_main = w_in[:, :PROJ_MAIN].astype(BF16)
    w_dt = jnp.pad(w_in[:, PROJ_MAIN:], ((0, 0), (0, LANES - SSM_HEADS))).astype(BF16)
    proj, dt_raw = _in_proj(x2d, norm1_w.reshape(1, -1), w_main, w_dt)
    proj3 = proj.reshape(batch, seq, PROJ_MAIN)

    attn = _moba(proj3, q_norm_w.reshape(1, -1), k_norm_w.reshape(1, -1))
    ssm = _ssd(proj3, dt_raw.reshape(batch, seq, LANES), ssm_conv_w, ssm_conv_b.reshape(1, -1),
               _pad_lanes(dt_bias), _pad_lanes(a_log),
               jnp.repeat(d_skip, SSM_HEAD_DIM).reshape(1, -1), ssm_norm_w.reshape(1, -1))

    w_out_b = w_out.astype(BF16)
    x1 = _out_proj(x2d, attn.reshape(-1, ATTN_WIDTH), ssm.reshape(-1, SSM_WIDTH),
                   w_out_b[:ATTN_WIDTH], w_out_b[ATTN_WIDTH:])
    return _conv_glu(x1, norm2_w.reshape(1, -1), w_up.astype(BF16), ffn_conv_w, ffn_conv_b.reshape(1, -1),
                     w_down.astype(BF16), seq)


def kernel(x, norm1_w, w_in, q_norm_w, k_norm_w, ssm_conv_w, ssm_conv_b, dt_bias, a_log, d_skip, ssm_norm_w,
           w_out, norm2_w, w_up, ffn_conv_w, ffn_conv_b, w_down):
    batch, seq, d = x.shape
    x2d = x.reshape(batch * seq, d)
    for i in range(norm1_w.shape[0]):
        x2d = _layer(x2d, batch, seq, norm1_w[i], w_in[i], q_norm_w[i], k_norm_w[i], ssm_conv_w[i],
                     ssm_conv_b[i], dt_bias[i], a_log[i], d_skip[i], ssm_norm_w[i], w_out[i], norm2_w[i],
                     w_up[i], ffn_conv_w[i], ffn_conv_b[i], w_down[i])
    return x2d.reshape(batch, seq, d)
```

```python
import functools

import jax
import jax.numpy as jnp
from jax import lax
from jax.experimental import pallas as pl
from jax.experimental.pallas import tpu as pltpu

F32 = jnp.float32
BF16 = jnp.bfloat16

D_MODEL = 2048
ATTN_WIDTH = 1024
ATTN_HEAD_DIM = 128
ATTN_HEADS = ATTN_WIDTH // ATTN_HEAD_DIM
MOBA_BLOCK = 256
MOBA_TOPK = 3
SSM_WIDTH = 1024
SSM_HEAD_DIM = 64
SSM_HEADS = SSM_WIDTH // SSM_HEAD_DIM
SSM_GROUPS = 2
SSM_HEADS_PER_GROUP = SSM_HEADS // SSM_GROUPS
SSM_GROUP_WIDTH = SSM_WIDTH // SSM_GROUPS
SSM_STATE = 128
SSM_CONV = 4
XBC_WIDTH = SSM_WIDTH + 2 * SSM_GROUPS * SSM_STATE
PROJ_MAIN = 3 * ATTN_WIDTH + SSM_WIDTH + XBC_WIDTH
D_FF = 5632
FFN_CONV = 3
EPS = 1e-6

LANES = 128
SUBLANES = 8
VMEM_LIMIT = 56 * 1024 * 1024

SSD_CHUNK = 256
ROW_TILE = 512
IN_ROW_TILE = 1024
COL_TILE = 512
FFN_HALO = 16
NORM_ROWS = 64
CONV_ROWS = 64

_NT = (((1,), (1,)), ((), ()))


def _rms(x, w):
    return x * lax.rsqrt(jnp.mean(x * x, axis=-1, keepdims=True) + EPS) * w


def _norm_rows(x_ref, nw_ref, h_ref, dst_off, n_rows):
    def body(c, carry):
        r = pl.multiple_of(c * NORM_ROWS, NORM_ROWS)
        h_ref[pl.ds(dst_off + r, NORM_ROWS), :] = _rms(x_ref[pl.ds(r, NORM_ROWS), :], nw_ref[...]).astype(BF16)
        return carry
    lax.fori_loop(0, n_rows // NORM_ROWS, body, 0)


def _in_proj_kernel(x_ref, nw_ref, w_ref, wdt_ref, o_ref, dt_ref, h_ref):
    @pl.when(pl.program_id(1) == 0)
    def _():
        _norm_rows(x_ref, nw_ref, h_ref, 0, x_ref.shape[0])
        dt_ref[...] = jnp.dot(h_ref[...], wdt_ref[...], preferred_element_type=F32)
    o_ref[...] = jnp.dot(h_ref[...], w_ref[...], preferred_element_type=F32)


def _in_proj(x2d, nw, w_main, w_dt):
    t, d = x2d.shape
    n = w_main.shape[1]
    tm = IN_ROW_TILE
    return pl.pallas_call(
        _in_proj_kernel,
        grid=(t // tm, n // COL_TILE),
        in_specs=[
            pl.BlockSpec((tm, d), lambda i, j: (i, 0)),
            pl.BlockSpec((1, d), lambda i, j: (0, 0)),
            pl.BlockSpec((d, COL_TILE), lambda i, j: (0, j)),
            pl.BlockSpec((d, LANES), lambda i, j: (0, 0)),
        ],
        out_specs=[
            pl.BlockSpec((tm, COL_TILE), lambda i, j: (i, j)),
            pl.BlockSpec((tm, LANES), lambda i, j: (i, 0)),
        ],
        out_shape=[jax.ShapeDtypeStruct((t, n), F32), jax.ShapeDtypeStruct((t, LANES), F32)],
        scratch_shapes=[pltpu.VMEM((tm, d), BF16)],
        compiler_params=pltpu.CompilerParams(
            dimension_semantics=("parallel", "arbitrary"), vmem_limit_bytes=VMEM_LIMIT),
        name="in_proj",
    )(x2d, nw, w_main, w_dt)


def _pair_schedule(nb, width):
    remaining = {i: list(range(i)) for i in range(1, nb)}
    qi, kj = [], []
    while any(remaining.values()):
        live = sorted((i for i in remaining if remaining[i]), key=lambda i: -len(remaining[i]))
        if len(live) < width:
            return None
        for i in live[:width]:
            qi.append(i)
            kj.append(remaining[i].pop())
    return qi, kj


def _moba_schedule(nb):
    for width in (4, 2, 1):
        sched = _pair_schedule(nb, width)
        if sched is not None:
            return width, sched
    raise ValueError(f"no MoBA pair schedule for {nb} blocks")


def _moba_kernel(qi_ref, kj_ref, q_ref, k_ref, v_ref, qw_ref, kw_ref, o_ref,
                 qf_s, qb_s, k_s, vt_s, km_s, sel_s, m_s, l_s, acc_s, *, nb, width, n_groups):
    blk = MOBA_BLOCK
    scale = ATTN_HEAD_DIM ** -0.5
    for j in range(nb):
        rows = slice(j * blk, (j + 1) * blk)
        kn = _rms(k_ref[rows, :], kw_ref[...])
        k_s[j] = kn.astype(BF16)
        km_s[j:j + 1, :] = jnp.mean(kn, axis=0, keepdims=True)
        vt_s[j] = v_ref[rows, :].T.astype(BF16)
        qn = _rms(q_ref[rows, :], qw_ref[...]) * scale
        qf_s[j] = qn
        qb_s[j] = qn.astype(BF16)

    key_i = lax.broadcasted_iota(jnp.int32, (blk, blk), 0)
    qry_i = lax.broadcasted_iota(jnp.int32, (blk, blk), 1)
    bid = lax.broadcasted_iota(jnp.int32, (nb, blk), 0)

    def init_block(i):
        gate = lax.dot_general(km_s[...], qf_s[i], _NT, precision=lax.Precision.HIGHEST,
                               preferred_element_type=F32)
        past = bid < i
        g = jnp.where(past, gate, -jnp.inf)
        rank = jnp.zeros((nb, blk), jnp.int32)
        for jp in range(nb):
            row = g[jp:jp + 1, :]
            rank = rank + jnp.where(row > g, 1, jnp.where(row == g, jnp.where(bid > jp, 1, 0), 0))
        sel = jnp.where(past, jnp.where(rank < MOBA_TOPK, 1.0, 0.0), 0.0)
        for jp in range(nb):
            sel_s[i * nb + jp] = sel[jp:jp + 1, :]
        s = lax.dot_general(k_s[i], qb_s[i], _NT, preferred_element_type=F32)
        s = jnp.where(key_i <= qry_i, s, -jnp.inf)
        m = jnp.max(s, axis=0, keepdims=True)
        p = jnp.exp(s - m)
        m_s[i] = m
        l_s[i] = jnp.sum(p, axis=0, keepdims=True)
        acc_s[i] = jnp.dot(vt_s[i], p.astype(BF16), preferred_element_type=F32)

    def init_two(t, carry):
        init_block(2 * t)
        init_block(2 * t + 1)
        return carry

    lax.fori_loop(0, nb // 2, init_two, 0)

    def pair_group(t, carry):
        pairs = [(qi_ref[t * width + u], kj_ref[t * width + u]) for u in range(width)]
        m_old = [m_s[i] for i, _ in pairs]
        scores = [lax.dot_general(k_s[j], qb_s[i], _NT, preferred_element_type=F32) for i, j in pairs]
        upd = []
        for (i, j), m0, s in zip(pairs, m_old, scores):
            s = jnp.where(sel_s[i * nb + j] > 0.0, s, -jnp.inf)
            m1 = jnp.maximum(m0, jnp.max(s, axis=0, keepdims=True))
            p = jnp.exp(s - m1)
            pv = jnp.dot(vt_s[j], p.astype(BF16), preferred_element_type=F32)
            upd.append((i, m1, jnp.exp(m0 - m1), jnp.sum(p, axis=0, keepdims=True), pv))
        for i, m1, alpha, psum, pv in upd:
            m_s[i] = m1
            l_s[i] = alpha * l_s[i] + psum
            acc_s[i] = alpha * acc_s[i] + pv
        return carry

    lax.fori_loop(0, n_groups, pair_group, 0)

    def finish_two(t, carry):
        for i in (2 * t, 2 * t + 1):
            o_ref[pl.ds(pl.multiple_of(i * blk, blk), blk), :] = (acc_s[i] / l_s[i]).T.astype(o_ref.dtype)
        return carry

    lax.fori_loop(0, nb // 2, finish_two, 0)


def _moba(proj3, qw, kw):
    b, s, _ = proj3.shape
    nb = s // MOBA_BLOCK
    assert nb % 2 == 0
    dh = ATTN_HEAD_DIM
    blk = MOBA_BLOCK
    width, (qi, kj) = _moba_schedule(nb)
    smem = pl.BlockSpec(memory_space=pltpu.SMEM)
    return pl.pallas_call(
        functools.partial(_moba_kernel, nb=nb, width=width, n_groups=len(qi) // width),
        grid=(b, ATTN_HEADS),
        in_specs=[
            smem,
            smem,
            pl.BlockSpec((None, s, dh), lambda bi, h: (bi, 0, h)),
            pl.BlockSpec((None, s, dh), lambda bi, h: (bi, 0, ATTN_HEADS + h)),
            pl.BlockSpec((None, s, dh), lambda bi, h: (bi, 0, 2 * ATTN_HEADS + h)),
            pl.BlockSpec((1, dh), lambda bi, h: (0, 0)),
            pl.BlockSpec((1, dh), lambda bi, h: (0, 0)),
        ],
        out_specs=pl.BlockSpec((None, s, dh), lambda bi, h: (bi, 0, h)),
        out_shape=jax.ShapeDtypeStruct((b, s, ATTN_WIDTH), BF16),
        scratch_shapes=[
            pltpu.VMEM((nb, blk, dh), F32),
            pltpu.VMEM((nb, blk, dh), BF16),
            pltpu.VMEM((nb, blk, dh), BF16),
            pltpu.VMEM((nb, dh, blk), BF16),
            pltpu.VMEM((nb, dh), F32),
            pltpu.VMEM((nb * nb, 1, blk), F32),
            pltpu.VMEM((nb, 1, blk), F32),
            pltpu.VMEM((nb, 1, blk), F32),
            pltpu.VMEM((nb, dh, blk), F32),
        ],
        compiler_params=pltpu.CompilerParams(
            dimension_semantics=("parallel", "parallel"), vmem_limit_bytes=VMEM_LIMIT),
        name="moba",
    )(jnp.asarray(qi, jnp.int32), jnp.asarray(kj, jnp.int32), proj3, proj3, proj3, qw, kw)


def _pair_cols(arr, i0):
    rows = arr.shape[0]
    lo = lax.broadcasted_iota(jnp.int32, (rows, LANES), 1) < SSM_HEAD_DIM
    a0 = jnp.broadcast_to(arr[:, i0:i0 + 1], (rows, LANES))
    a1 = jnp.broadcast_to(arr[:, i0 + 1:i0 + 2], (rows, LANES))
    return jnp.where(lo, a0, a1)


def _ssd_kernel(xs_ref, bc_ref, z_ref, dt_ref, cw_ref, cb_ref, dtb_ref, alog_ref, dsk_ref, nw_ref, o_ref,
                ext_s, xbc_s, ht_s, y_s, wx_s, dec_s):
    L = xs_ref.shape[0]
    pad = SUBLANES

    @pl.when(pl.program_id(1) == 0)
    def _():
        ext_s[0:pad, :] = jnp.zeros((pad, XBC_WIDTH), F32)
        ht_s[...] = jnp.zeros(ht_s.shape, F32)

    ext_s[pad:pad + L, 0:SSM_WIDTH] = xs_ref[...]
    ext_s[pad:pad + L, SSM_WIDTH:XBC_WIDTH] = bc_ref[...]
    for cblk in range(XBC_WIDTH // LANES):
        cols = slice(cblk * LANES, (cblk + 1) * LANES)
        conv = cb_ref[:, cols] + cw_ref[0:1, cols] * ext_s[pad - 3:pad - 3 + L, cols]
        for j in range(1, SSM_CONV):
            conv = conv + cw_ref[j:j + 1, cols] * ext_s[pad - 3 + j:pad - 3 + j + L, cols]
        xbc_s[:, cols] = conv * jax.nn.sigmoid(conv)
    ext_s[0:pad, :] = ext_s[L:L + pad, :]

    dtv = dt_ref[...] + dtb_ref[...]
    dt = jnp.maximum(dtv, 0.0) + jnp.log1p(jnp.exp(-jnp.abs(dtv)))
    la = dt * (-jnp.exp(alog_ref[...]))
    row_i = lax.broadcasted_iota(jnp.int32, (L, L), 0)
    col_i = lax.broadcasted_iota(jnp.int32, (L, L), 1)
    tril = row_i >= col_i
    acol = jnp.dot(jnp.where(tril, 1.0, 0.0), la, precision=lax.Precision.HIGHEST,
                   preferred_element_type=F32)
    arow = acol.T
    ecol = jnp.exp(acol)
    aend = acol[L - 1:L, :]
    wcol = jnp.exp(aend - acol)
    eend = jnp.exp(aend)
    lo = lax.broadcasted_iota(jnp.int32, (L, LANES), 1) < SSM_HEAD_DIM

    for g in range(SSM_GROUPS):
        b_off = SSM_WIDTH + g * SSM_STATE
        c_off = SSM_WIDTH + SSM_GROUPS * SSM_STATE + g * SSM_STATE
        bg = xbc_s[:, b_off:b_off + SSM_STATE]
        cg = xbc_s[:, c_off:c_off + SSM_STATE].astype(BF16)
        cb = lax.dot_general(cg, bg.astype(BF16), _NT, preferred_element_type=F32)
        cbm = jnp.where(tril, cb, 0.0)
        ht = ht_s[g]
        ch = jnp.dot(cg, ht.astype(BF16), preferred_element_type=F32)
        for pr in range(SSM_HEADS_PER_GROUP // 2):
            i0 = g * SSM_HEADS_PER_GROUP + 2 * pr
            lanes = slice(i0 * SSM_HEAD_DIM, i0 * SSM_HEAD_DIM + LANES)
            gl = slice(pr * LANES, (pr + 1) * LANES)
            xs_p = xbc_s[:, lanes]
            xdt = xs_p * _pair_cols(dt, i0)
            yd = None
            for hh in range(2):
                idx = i0 + hh
                seg = acol[:, idx:idx + 1] - arow[idx:idx + 1, :]
                mm = (cbm * jnp.exp(jnp.minimum(seg, 0.0))).astype(BF16)
                xh = jnp.where(lo, xdt, 0.0) if hh == 0 else jnp.where(lo, 0.0, xdt)
                part = jnp.dot(mm, xh.astype(BF16), preferred_element_type=F32)
                yd = part if yd is None else yd + part
            y_off = ch[:, gl] * _pair_cols(ecol, i0)
            y_s[:, lanes] = yd + y_off + dsk_ref[:, lanes] * xs_p
            wx_s[:, gl] = (xdt * _pair_cols(wcol, i0)).astype(BF16)
            dec_s[:, gl] = _pair_cols(eend, i0)
        ht_s[g] = ht * dec_s[...] + jnp.dot(bg.T.astype(BF16), wx_s[...], preferred_element_type=F32)

    for g in range(SSM_GROUPS):
        cols = slice(g * SSM_GROUP_WIDTH, (g + 1) * SSM_GROUP_WIDTH)
        zz = z_ref[:, cols]
        yg = y_s[:, cols] * (zz * jax.nn.sigmoid(zz))
        o_ref[:, cols] = _rms(yg, nw_ref[:, cols]).astype(o_ref.dtype)


def _ssd(proj3, dt3, conv_w, conv_b, dt_bias, a_log, d_skip, norm_w):
    b, s, _ = proj3.shape
    L = SSD_CHUNK
    full = lambda shape: pl.BlockSpec(shape, lambda bi, c: (0,) * len(shape))
    return pl.pallas_call(
        _ssd_kernel,
        grid=(b, s // L),
        in_specs=[
            pl.BlockSpec((None, L, SSM_WIDTH), lambda bi, c: (bi, c, (3 * ATTN_WIDTH + SSM_WIDTH) // SSM_WIDTH)),
            pl.BlockSpec((None, L, XBC_WIDTH - SSM_WIDTH),
                         lambda bi, c: (bi, c, (3 * ATTN_WIDTH + 2 * SSM_WIDTH) // (XBC_WIDTH - SSM_WIDTH))),
            pl.BlockSpec((None, L, SSM_WIDTH), lambda bi, c: (bi, c, 3 * ATTN_WIDTH // SSM_WIDTH)),
            pl.BlockSpec((None, L, LANES), lambda bi, c: (bi, c, 0)),
            full((SSM_CONV, XBC_WIDTH)),
            full((1, XBC_WIDTH)),
            full((1, LANES)),
            full((1, LANES)),
            full((1, SSM_WIDTH)),
            full((1, SSM_WIDTH)),
        ],
        out_specs=pl.BlockSpec((None, L, SSM_WIDTH), lambda bi, c: (bi, c, 0)),
        out_shape=jax.ShapeDtypeStruct((b, s, SSM_WIDTH), BF16),
        scratch_shapes=[
            pltpu.VMEM((L + 2 * SUBLANES, XBC_WIDTH), F32),
            pltpu.VMEM((L, XBC_WIDTH), F32),
            pltpu.VMEM((SSM_GROUPS, SSM_STATE, SSM_GROUP_WIDTH), F32),
            pltpu.VMEM((L, SSM_WIDTH), F32),
            pltpu.VMEM((L, SSM_GROUP_WIDTH), BF16),
            pltpu.VMEM((1, SSM_GROUP_WIDTH), F32),
        ],
        compiler_params=pltpu.CompilerParams(
            dimension_semantics=("parallel", "arbitrary"), vmem_limit_bytes=VMEM_LIMIT),
        name="ssd",
    )(proj3, proj3, proj3, dt3, conv_w, conv_b, dt_bias, a_log, d_skip, norm_w)


def _out_proj_kernel(x_ref, a_ref, s_ref, wa_ref, ws_ref, o_ref):
    o_ref[...] = (x_ref[...]
                  + jnp.dot(a_ref[...], wa_ref[...], preferred_element_type=F32)
                  + jnp.dot(s_ref[...], ws_ref[...], preferred_element_type=F32))


def _out_proj(x2d, attn2d, ssm2d, w_attn, w_ssm):
    t, d = x2d.shape
    return pl.pallas_call(
        _out_proj_kernel,
        grid=(t // ROW_TILE,),
        in_specs=[
            pl.BlockSpec((ROW_TILE, d), lambda i: (i, 0)),
            pl.BlockSpec((ROW_TILE, ATTN_WIDTH), lambda i: (i, 0)),
            pl.BlockSpec((ROW_TILE, SSM_WIDTH), lambda i: (i, 0)),
            pl.BlockSpec((ATTN_WIDTH, d), lambda i: (0, 0)),
            pl.BlockSpec((SSM_WIDTH, d), lambda i: (0, 0)),
        ],
        out_specs=pl.BlockSpec((ROW_TILE, d), lambda i: (i, 0)),
        out_shape=jax.ShapeDtypeStruct((t, d), F32),
        compiler_params=pltpu.CompilerParams(
            dimension_semantics=("parallel",), vmem_limit_bytes=VMEM_LIMIT),
        name="out_proj",
    )(x2d, attn2d, ssm2d, w_attn, w_ssm)


def _conv_glu_kernel(x_ref, halo_ref, xres_ref, nw_ref, wg_ref, wv_ref, cwg_ref, cwv_ref, cbg_ref, cbv_ref,
                     wd_ref, o_ref, h_s, ug_s, uv_s, act_s, *, tiles_per_seq, n_up):
    tm = x_ref.shape[0]
    tf = wg_ref.shape[1]
    i = pl.program_id(0)
    f = pl.program_id(1)

    def up_dots(slot):
        ug_s[slot] = jnp.dot(h_s[...], wg_ref[...], preferred_element_type=F32)
        uv_s[slot] = jnp.dot(h_s[...], wv_ref[...], preferred_element_type=F32)

    def conv(u_s, slot, cw_ref, cb_ref, r0, cols):
        first = FFN_HALO - (FFN_CONV - 1) + r0
        out = cb_ref[:, cols] + cw_ref[0:1, cols] * u_s[slot, first:first + CONV_ROWS, cols]
        for j in range(1, FFN_CONV):
            out = out + cw_ref[j:j + 1, cols] * u_s[slot, first + j:first + j + CONV_ROWS, cols]
        return out

    def conv_act(slot, chunk):
        for c0 in range(0, tf, LANES):
            cols = slice(c0, c0 + LANES)
            for r0 in range(0, tm, CONV_ROWS):
                gate = conv(ug_s, slot, cwg_ref, cbg_ref, r0, cols)
                val = conv(uv_s, slot, cwv_ref, cbv_ref, r0, cols)
                act_s[chunk, r0:r0 + CONV_ROWS, cols] = (gate * jax.nn.sigmoid(gate) * val).astype(BF16)

    @pl.when(f == 0)
    def _():
        _norm_rows(x_ref, nw_ref, h_s, FFN_HALO, tm)
        keep = jnp.where(i % tiles_per_seq == 0, 0.0, 1.0)
        h_s[0:FFN_HALO, :] = (_rms(halo_ref[...], nw_ref[...]) * keep).astype(BF16)
        up_dots(0)

    for parity in (0, 1):
        @pl.when((f >= 1) & (f < n_up) & (f % 2 == parity))
        def _():
            conv_act(1 - parity, f - 1)
            up_dots(parity)

    @pl.when(f == n_up)
    def _():
        conv_act((n_up - 1) % 2, n_up - 1)

    @pl.when(f >= n_up)
    def _():
        acc = xres_ref[...]
        for c in range(n_up):
            acc = acc + jnp.dot(act_s[c], wd_ref[c * tf:(c + 1) * tf, :], preferred_element_type=F32)
        o_ref[...] = acc


def _conv_glu(x2d, nw, w_up, conv_w, conv_b, w_down, seq_len):
    t, d = x2d.shape
    tm, tf, tn = ROW_TILE, COL_TILE, COL_TILE
    n_up = D_FF // tf
    n_down = d // tn
    halo_blocks = tm // FFN_HALO
    up = lambda f: jnp.minimum(f, n_up - 1)
    cv = lambda f: jnp.clip(f - 1, 0, n_up - 1)
    down = lambda f: jnp.maximum(f - n_up, 0)
    return pl.pallas_call(
        functools.partial(_conv_glu_kernel, tiles_per_seq=seq_len // tm, n_up=n_up),
        grid=(t // tm, n_up + n_down),
        in_specs=[
            pl.BlockSpec((tm, d), lambda i, f: (i, 0)),
            pl.BlockSpec((FFN_HALO, d), lambda i, f: (jnp.maximum(i * halo_blocks - 1, 0), 0)),
            pl.BlockSpec((tm, tn), lambda i, f: (i, down(f))),
            pl.BlockSpec((1, d), lambda i, f: (0, 0)),
            pl.BlockSpec((d, tf), lambda i, f: (0, up(f))),
            pl.BlockSpec((d, tf), lambda i, f: (0, n_up + up(f))),
            pl.BlockSpec((FFN_CONV, tf), lambda i, f: (0, cv(f))),
            pl.BlockSpec((FFN_CONV, tf), lambda i, f: (0, n_up + cv(f))),
            pl.BlockSpec((1, tf), lambda i, f: (0, cv(f))),
            pl.BlockSpec((1, tf), lambda i, f: (0, n_up + cv(f))),
            pl.BlockSpec((D_FF, tn), lambda i, f: (0, down(f))),
        ],
        out_specs=pl.BlockSpec((tm, tn), lambda i, f: (i, down(f))),
        out_shape=jax.ShapeDtypeStruct((t, d), F32),
        scratch_shapes=[
            pltpu.VMEM((FFN_HALO + tm, d), BF16),
            pltpu.VMEM((2, FFN_HALO + tm, tf), F32),
            pltpu.VMEM((2, FFN_HALO + tm, tf), F32),
            pltpu.VMEM((n_up, tm, tf), BF16),
        ],
        compiler_params=pltpu.CompilerParams(
            dimension_semantics=("parallel", "arbitrary"), vmem_limit_bytes=VMEM_LIMIT),
        name="conv_glu",
    )(x2d, x2d, x2d, nw, w_up, w_up, conv_w, conv_w, conv_b, conv_b, w_down)


def _pad_lanes(v):
    return jnp.pad(v.reshape(1, -1), ((0, 0), (0, LANES - v.shape[-1])))


def _layer(x2d, batch, seq, norm1_w, w_in, q_norm_w, k_norm_w, ssm_conv_w, ssm_conv_b, dt_bias, a_log,
           d_skip, ssm_norm_w, w_out, norm2_w, w_up, ffn_conv_w, ffn_conv_b, w_down):
    w_main = w_in[:, :PROJ_MAIN].astype(BF16)
    w_dt = jnp.pad(w_in[:, PROJ_MAIN:], ((0, 0), (0, LANES - SSM_HEADS))).astype(BF16)
    proj, dt_raw = _in_proj(x2d, norm1_w.reshape(1, -1), w_main, w_dt)
    proj3 = proj.reshape(batch, seq, PROJ_MAIN)

    attn = _moba(proj3, q_norm_w.reshape(1, -1), k_norm_w.reshape(1, -1))
    ssm = _ssd(proj3, dt_raw.reshape(batch, seq, LANES), ssm_conv_w, ssm_conv_b.reshape(1, -1),
               _pad_lanes(dt_bias), _pad_lanes(a_log),
               jnp.repeat(d_skip, SSM_HEAD_DIM).reshape(1, -1), ssm_norm_w.reshape(1, -1))

    w_out_b = w_out.astype(BF16)
    x1 = _out_proj(x2d, attn.reshape(-1, ATTN_WIDTH), ssm.reshape(-1, SSM_WIDTH),
                   w_out_b[:ATTN_WIDTH], w_out_b[ATTN_WIDTH:])
    return _conv_glu(x1, norm2_w.reshape(1, -1), w_up.astype(BF16), ffn_conv_w, ffn_conv_b.reshape(1, -1),
                     w_down.astype(BF16), seq)


def kernel(x, norm1_w, w_in, q_norm_w, k_norm_w, ssm_conv_w, ssm_conv_b, dt_bias, a_log, d_skip, ssm_norm_w,
           w_out, norm2_w, w_up, ffn_conv_w, ffn_conv_b, w_down):
    batch, seq, d = x.shape
    x2d = x.reshape(batch * seq, d)
    for i in range(norm1_w.shape[0]):
        x2d = _layer(x2d, batch, seq, norm1_w[i], w_in[i], q_norm_w[i], k_norm_w[i], ssm_conv_w[i],
                     ssm_conv_b[i], dt_bias[i], a_log[i], d_skip[i], ssm_norm_w[i], w_out[i], norm2_w[i],
                     w_up[i], ffn_conv_w[i], ffn_conv_b[i], w_down[i])
    return x2d.reshape(batch, seq, d)
```

```python
import functools

import jax
import jax.numpy as jnp
from jax import lax
from jax.experimental import pallas as pl
from jax.experimental.pallas import tpu as pltpu

F32 = jnp.float32
BF16 = jnp.bfloat16

D_MODEL = 2048
ATTN_WIDTH = 1024
ATTN_HEAD_DIM = 128
ATTN_HEADS = ATTN_WIDTH // ATTN_HEAD_DIM
MOBA_BLOCK = 256
MOBA_TOPK = 3
SSM_WIDTH = 1024
SSM_HEAD_DIM = 64
SSM_HEADS = SSM_WIDTH // SSM_HEAD_DIM
SSM_GROUPS = 2
SSM_HEADS_PER_GROUP = SSM_HEADS // SSM_GROUPS
SSM_GROUP_WIDTH = SSM_WIDTH // SSM_GROUPS
SSM_STATE = 128
SSM_CONV = 4
XBC_WIDTH = SSM_WIDTH + 2 * SSM_GROUPS * SSM_STATE
PROJ_MAIN = 3 * ATTN_WIDTH + SSM_WIDTH + XBC_WIDTH
D_FF = 5632
FFN_CONV = 3
EPS = 1e-6
LOG2_E = 1.4426950408889634

LANES = 128
SUBLANES = 8
VMEM_LIMIT = 56 * 1024 * 1024

SSD_CHUNK = 256
ROW_TILE = 512
IN_ROW_TILE = 1024
COL_TILE = 512
FFN_HALO = 16
NORM_ROWS = 64
CONV_ROWS = 16

_NT = (((1,), (1,)), ((), ()))


def _rms(x, w):
    return x * lax.rsqrt(jnp.mean(x * x, axis=-1, keepdims=True) + EPS) * w


def _norm_rows(x_ref, nw_ref, h_ref, dst_off, n_rows):
    def body(c, carry):
        r = pl.multiple_of(c * NORM_ROWS, NORM_ROWS)
        h_ref[pl.ds(dst_off + r, NORM_ROWS), :] = _rms(x_ref[pl.ds(r, NORM_ROWS), :], nw_ref[...]).astype(BF16)
        return carry
    lax.fori_loop(0, n_rows // NORM_ROWS, body, 0)


def _in_proj_kernel(x_ref, nw_ref, w_ref, wdt_ref, o_ref, dt_ref, h_ref):
    @pl.when(pl.program_id(1) == 0)
    def _():
        _norm_rows(x_ref, nw_ref, h_ref, 0, x_ref.shape[0])
        dt_ref[...] = jnp.dot(h_ref[...], wdt_ref[...], preferred_element_type=F32)
    o_ref[...] = jnp.dot(h_ref[...], w_ref[...], preferred_element_type=F32)


def _in_proj(x2d, nw, w_main, w_dt):
    t, d = x2d.shape
    n = w_main.shape[1]
    tm = IN_ROW_TILE
    return pl.pallas_call(
        _in_proj_kernel,
        grid=(t // tm, n // COL_TILE),
        in_specs=[
            pl.BlockSpec((tm, d), lambda i, j: (i, 0)),
            pl.BlockSpec((1, d), lambda i, j: (0, 0)),
            pl.BlockSpec((d, COL_TILE), lambda i, j: (0, j)),
            pl.BlockSpec((d, LANES), lambda i, j: (0, 0)),
        ],
        out_specs=[
            pl.BlockSpec((tm, COL_TILE), lambda i, j: (i, j)),
            pl.BlockSpec((tm, LANES), lambda i, j: (i, 0)),
        ],
        out_shape=[jax.ShapeDtypeStruct((t, n), F32), jax.ShapeDtypeStruct((t, LANES), F32)],
        scratch_shapes=[pltpu.VMEM((tm, d), BF16)],
        compiler_params=pltpu.CompilerParams(
            dimension_semantics=("parallel", "arbitrary"), vmem_limit_bytes=VMEM_LIMIT),
        name="in_proj",
    )(x2d, nw, w_main, w_dt)


def _pair_schedule(nb, width):
    remaining = {i: list(range(i)) for i in range(1, nb)}
    qi, kj = [], []
    while any(remaining.values()):
        live = sorted((i for i in remaining if remaining[i]), key=lambda i: -len(remaining[i]))
        if len(live) < width:
            return None
        for i in live[:width]:
            qi.append(i)
            kj.append(remaining[i].pop())
    return qi, kj


def _moba_schedule(nb):
    for width in (4, 2, 1):
        sched = _pair_schedule(nb, width)
        if sched is not None and width <= nb - 1:
            qi, kj = sched
            if (len(qi) // width) % 2:
                qi = qi + list(range(1, width + 1))
                kj = kj + [nb - 1] * width
            return width, (qi, kj)
    raise ValueError(f"no MoBA pair schedule for {nb} blocks")


def _moba_kernel(qi_ref, kj_ref, q_ref, k_ref, v_ref, qw_ref, kw_ref, o_ref,
                 qf_s, qb_s, k_s, vt_s, km_s, sel_s, m_s, l_s, acc_s,
                 sc_a, sc_b, p_a, p_b, al_a, al_b, *, nb, width, n_groups):
    blk = MOBA_BLOCK
    scale = ATTN_HEAD_DIM ** -0.5 * LOG2_E
    for j in range(nb):
        rows = slice(j * blk, (j + 1) * blk)
        kn = _rms(k_ref[rows, :], kw_ref[...])
        k_s[j] = kn.astype(BF16)
        km_s[j:j + 1, :] = jnp.mean(kn, axis=0, keepdims=True)
        vt_s[j] = v_ref[rows, :].T.astype(BF16)
        qn = _rms(q_ref[rows, :], qw_ref[...]) * scale
        qf_s[j] = qn
        qb_s[j] = qn.astype(BF16)

    key_i = lax.broadcasted_iota(jnp.int32, (blk, blk), 0)
    qry_i = lax.broadcasted_iota(jnp.int32, (blk, blk), 1)
    bid = lax.broadcasted_iota(jnp.int32, (nb, blk), 0)

    def init_group(t, carry):
        ids = [t * width + u for u in range(width)]
        gates = [lax.dot_general(km_s[...], qf_s[i], _NT, precision=lax.Precision.HIGHEST,
                                 preferred_element_type=F32) for i in ids]
        owns = [lax.dot_general(k_s[i], qb_s[i], _NT, preferred_element_type=F32) for i in ids]
        for i, gate in zip(ids, gates):
            past = bid < i
            g = jnp.where(past, gate, -jnp.inf)
            rank = jnp.zeros((nb, blk), jnp.int32)
            for jp in range(nb):
                row = g[jp:jp + 1, :]
                rank = rank + jnp.where(row > g, 1, jnp.where(row == g, jnp.where(bid > jp, 1, 0), 0))
            sel = jnp.where(past, jnp.where(rank < MOBA_TOPK, 1.0, 0.0), 0.0)
            for jp in range(nb):
                sel_s[i * nb + jp] = sel[jp:jp + 1, :]
        for i, s in zip(ids, owns):
            s = jnp.where(key_i <= qry_i, s, -jnp.inf)
            m = jnp.max(s, axis=0, keepdims=True)
            p = jnp.exp2(s - m)
            m_s[i] = m
            l_s[i] = jnp.sum(p, axis=0, keepdims=True)
            acc_s[i] = jnp.dot(vt_s[i], p.astype(BF16), preferred_element_type=F32)
        return carry

    lax.fori_loop(0, nb // width, init_group, 0)

    def group_pairs(g):
        return [(qi_ref[g * width + u], kj_ref[g * width + u]) for u in range(width)]

    def score_group(g, sc):
        for u, (i, j) in enumerate(group_pairs(g)):
            sc[u] = lax.dot_general(k_s[j], qb_s[i], _NT, preferred_element_type=F32)

    def softmax_group(g, sc, p_buf, al_buf):
        pairs = group_pairs(g)
        m_old = [m_s[i] for i, _ in pairs]
        upd = []
        for u, ((i, j), m0) in enumerate(zip(pairs, m_old)):
            s = jnp.where(sel_s[i * nb + j] > 0.0, sc[u], -jnp.inf)
            m1 = jnp.maximum(m0, jnp.max(s, axis=0, keepdims=True))
            p = jnp.exp2(s - m1)
            p_buf[u] = p.astype(BF16)
            alpha = jnp.exp2(m0 - m1)
            al_buf[u] = alpha
            upd.append((i, m1, alpha, jnp.sum(p, axis=0, keepdims=True)))
        for i, m1, alpha, psum in upd:
            m_s[i] = m1
            l_s[i] = alpha * l_s[i] + psum

    def pv_group(g, p_buf, al_buf):
        pairs = group_pairs(g)
        pvs = [jnp.dot(vt_s[j], p_buf[u], preferred_element_type=F32) for u, (_, j) in enumerate(pairs)]
        for u, ((i, _), pv) in enumerate(zip(pairs, pvs)):
            acc_s[i] = al_buf[u] * acc_s[i] + pv

    def step(g, sc_cur, sc_next, p_cur, al_cur, p_prev, al_prev):
        pv_group(jnp.maximum(g - 1, 0), p_prev, al_prev)
        score_group(jnp.minimum(g + 1, n_groups - 1), sc_next)
        softmax_group(g, sc_cur, p_cur, al_cur)

    p_b[...] = jnp.zeros(p_b.shape, BF16)
    al_b[...] = jnp.ones(al_b.shape, F32)
    score_group(0, sc_a)

    def two_steps(t, carry):
        step(2 * t, sc_a, sc_b, p_a, al_a, p_b, al_b)
        step(2 * t + 1, sc_b, sc_a, p_b, al_b, p_a, al_a)
        return carry

    lax.fori_loop(0, n_groups // 2, two_steps, 0)
    pv_group(n_groups - 1, p_b, al_b)

    def finish_two(t, carry):
        for i in (2 * t, 2 * t + 1):
            o_ref[pl.ds(pl.multiple_of(i * blk, blk), blk), :] = (acc_s[i] / l_s[i]).T.astype(o_ref.dtype)
        return carry

    lax.fori_loop(0, nb // 2, finish_two, 0)


def _moba(proj3, qw, kw):
    b, s, _ = proj3.shape
    nb = s // MOBA_BLOCK
    assert nb % 2 == 0
    dh = ATTN_HEAD_DIM
    blk = MOBA_BLOCK
    width, (qi, kj) = _moba_schedule(nb)
    assert nb % width == 0 and (len(qi) // width) % 2 == 0
    smem = pl.BlockSpec(memory_space=pltpu.SMEM)
    return pl.pallas_call(
        functools.partial(_moba_kernel, nb=nb, width=width, n_groups=len(qi) // width),
        grid=(b, ATTN_HEADS),
        in_specs=[
            smem,
            smem,
            pl.BlockSpec((None, s, dh), lambda bi, h: (bi, 0, h)),
            pl.BlockSpec((None, s, dh), lambda bi, h: (bi, 0, ATTN_HEADS + h)),
            pl.BlockSpec((None, s, dh), lambda bi, h: (bi, 0, 2 * ATTN_HEADS + h)),
            pl.BlockSpec((1, dh), lambda bi, h: (0, 0)),
            pl.BlockSpec((1, dh), lambda bi, h: (0, 0)),
        ],
        out_specs=pl.BlockSpec((None, s, dh), lambda bi, h: (bi, 0, h)),
        out_shape=jax.ShapeDtypeStruct((b, s, ATTN_WIDTH), BF16),
        scratch_shapes=[
            pltpu.VMEM((nb, blk, dh), F32),
            pltpu.VMEM((nb, blk, dh), BF16),
            pltpu.VMEM((nb, blk, dh), BF16),
            pltpu.VMEM((nb, dh, blk), BF16),
            pltpu.VMEM((nb, dh), F32),
            pltpu.VMEM((nb * nb, 1, blk), F32),
            pltpu.VMEM((nb, 1, blk), F32),
            pltpu.VMEM((nb, 1, blk), F32),
            pltpu.VMEM((nb, dh, blk), F32),
            pltpu.VMEM((width, blk, blk), F32),
            pltpu.VMEM((width, blk, blk), F32),
            pltpu.VMEM((width, blk, blk), BF16),
            pltpu.VMEM((width, blk, blk), BF16),
            pltpu.VMEM((width, 1, blk), F32),
            pltpu.VMEM((width, 1, blk), F32),
        ],
        compiler_params=pltpu.CompilerParams(
            dimension_semantics=("parallel", "parallel"), vmem_limit_bytes=VMEM_LIMIT),
        name="moba",
    )(jnp.asarray(qi, jnp.int32), jnp.asarray(kj, jnp.int32), proj3, proj3, proj3, qw, kw)


def _pair_cols(arr, i0):
    rows = arr.shape[0]
    lo = lax.broadcasted_iota(jnp.int32, (rows, LANES), 1) < SSM_HEAD_DIM
    a0 = jnp.broadcast_to(arr[:, i0:i0 + 1], (rows, LANES))
    a1 = jnp.broadcast_to(arr[:, i0 + 1:i0 + 2], (rows, LANES))
    return jnp.where(lo, a0, a1)


def _ssd_kernel(xs_ref, bc_ref, z_ref, dt_ref, cw_ref, cb_ref, dtb_ref, alog_ref, dsk_ref, nw_ref, o_ref,
                ext_s, xbc_s, ht_s, y_s, wx_s, dec_s):
    L = xs_ref.shape[0]
    pad = SUBLANES

    @pl.when(pl.program_id(1) == 0)
    def _():
        ext_s[0:pad, :] = jnp.zeros((pad, XBC_WIDTH), F32)
        ht_s[...] = jnp.zeros(ht_s.shape, F32)

    ext_s[pad:pad + L, 0:SSM_WIDTH] = xs_ref[...]
    ext_s[pad:pad + L, SSM_WIDTH:XBC_WIDTH] = bc_ref[...]
    for cblk in range(XBC_WIDTH // LANES):
        cols = slice(cblk * LANES, (cblk + 1) * LANES)
        conv = cb_ref[:, cols] + cw_ref[0:1, cols] * ext_s[pad - 3:pad - 3 + L, cols]
        for j in range(1, SSM_CONV):
            conv = conv + cw_ref[j:j + 1, cols] * ext_s[pad - 3 + j:pad - 3 + j + L, cols]
        xbc_s[:, cols] = conv * jax.nn.sigmoid(conv)
    ext_s[0:pad, :] = ext_s[L:L + pad, :]

    dtv = dt_ref[...] + dtb_ref[...]
    dt = jnp.maximum(dtv, 0.0) + jnp.log1p(jnp.exp(-jnp.abs(dtv)))
    la = dt * (-jnp.exp(alog_ref[...]))
    row_i = lax.broadcasted_iota(jnp.int32, (L, L), 0)
    col_i = lax.broadcasted_iota(jnp.int32, (L, L), 1)
    tril = row_i >= col_i
    acol = jnp.dot(jnp.where(tril, 1.0, 0.0), la, precision=lax.Precision.HIGHEST,
                   preferred_element_type=F32)
    arow = acol.T
    ecol = jnp.exp(acol)
    aend = acol[L - 1:L, :]
    wcol = jnp.exp(aend - acol)
    eend = jnp.exp(aend)
    lo = lax.broadcasted_iota(jnp.int32, (L, LANES), 1) < SSM_HEAD_DIM

    for g in range(SSM_GROUPS):
        b_off = SSM_WIDTH + g * SSM_STATE
        c_off = SSM_WIDTH + SSM_GROUPS * SSM_STATE + g * SSM_STATE
        bg = xbc_s[:, b_off:b_off + SSM_STATE]
        cg = xbc_s[:, c_off:c_off + SSM_STATE].astype(BF16)
        cb = lax.dot_general(cg, bg.astype(BF16), _NT, preferred_element_type=F32)
        cbm = jnp.where(tril, cb, 0.0)
        ht = ht_s[g]
        ch = jnp.dot(cg, ht.astype(BF16), preferred_element_type=F32)
        for pr in range(SSM_HEADS_PER_GROUP // 2):
            i0 = g * SSM_HEADS_PER_GROUP + 2 * pr
            lanes = slice(i0 * SSM_HEAD_DIM, i0 * SSM_HEAD_DIM + LANES)
            gl = slice(pr * LANES, (pr + 1) * LANES)
            xs_p = xbc_s[:, lanes]
            xdt = xs_p * _pair_cols(dt, i0)
            yd = None
            for hh in range(2):
                idx = i0 + hh
                seg = acol[:, idx:idx + 1] - arow[idx:idx + 1, :]
                mm = (cbm * jnp.exp(jnp.minimum(seg, 0.0))).astype(BF16)
                xh = jnp.where(lo, xdt, 0.0) if hh == 0 else jnp.where(lo, 0.0, xdt)
                part = jnp.dot(mm, xh.astype(BF16), preferred_element_type=F32)
                yd = part if yd is None else yd + part
            y_off = ch[:, gl] * _pair_cols(ecol, i0)
            y_s[:, lanes] = yd + y_off + dsk_ref[:, lanes] * xs_p
            wx_s[:, gl] = (xdt * _pair_cols(wcol, i0)).astype(BF16)
            dec_s[:, gl] = _pair_cols(eend, i0)
        ht_s[g] = ht * dec_s[...] + jnp.dot(bg.T.astype(BF16), wx_s[...], preferred_element_type=F32)

    for g in range(SSM_GROUPS):
        cols = slice(g * SSM_GROUP_WIDTH, (g + 1) * SSM_GROUP_WIDTH)
        zz = z_ref[:, cols]
        yg = y_s[:, cols] * (zz * jax.nn.sigmoid(zz))
        o_ref[:, cols] = _rms(yg, nw_ref[:, cols]).astype(o_ref.dtype)


def _ssd(proj3, dt3, conv_w, conv_b, dt_bias, a_log, d_skip, norm_w):
    b, s, _ = proj3.shape
    L = SSD_CHUNK
    full = lambda shape: pl.BlockSpec(shape, lambda bi, c: (0,) * len(shape))
    return pl.pallas_call(
        _ssd_kernel,
        grid=(b, s // L),
        in_specs=[
            pl.BlockSpec((None, L, SSM_WIDTH), lambda bi, c: (bi, c, (3 * ATTN_WIDTH + SSM_WIDTH) // SSM_WIDTH)),
            pl.BlockSpec((None, L, XBC_WIDTH - SSM_WIDTH),
                         lambda bi, c: (bi, c, (3 * ATTN_WIDTH + 2 * SSM_WIDTH) // (XBC_WIDTH - SSM_WIDTH))),
            pl.BlockSpec((None, L, SSM_WIDTH), lambda bi, c: (bi, c, 3 * ATTN_WIDTH // SSM_WIDTH)),
            pl.BlockSpec((None, L, LANES), lambda bi, c: (bi, c, 0)),
            full((SSM_CONV, XBC_WIDTH)),
            full((1, XBC_WIDTH)),
            full((1, LANES)),
            full((1, LANES)),
            full((1, SSM_WIDTH)),
            full((1, SSM_WIDTH)),
        ],
        out_specs=pl.BlockSpec((None, L, SSM_WIDTH), lambda bi, c: (bi, c, 0)),
        out_shape=jax.ShapeDtypeStruct((b, s, SSM_WIDTH), BF16),
        scratch_shapes=[
            pltpu.VMEM((L + 2 * SUBLANES, XBC_WIDTH), F32),
            pltpu.VMEM((L, XBC_WIDTH), F32),
            pltpu.VMEM((SSM_GROUPS, SSM_STATE, SSM_GROUP_WIDTH), F32),
            pltpu.VMEM((L, SSM_WIDTH), F32),
            pltpu.VMEM((L, SSM_GROUP_WIDTH), BF16),
            pltpu.VMEM((1, SSM_GROUP_WIDTH), F32),
        ],
        compiler_params=pltpu.CompilerParams(
            dimension_semantics=("parallel", "arbitrary"), vmem_limit_bytes=VMEM_LIMIT),
        name="ssd",
    )(proj3, proj3, proj3, dt3, conv_w, conv_b, dt_bias, a_log, d_skip, norm_w)


def _out_proj_kernel(x_ref, a_ref, s_ref, wa_ref, ws_ref, o_ref):
    o_ref[...] = (x_ref[...]
                  + jnp.dot(a_ref[...], wa_ref[...], preferred_element_type=F32)
                  + jnp.dot(s_ref[...], ws_ref[...], preferred_element_type=F32))


def _out_proj(x2d, attn2d, ssm2d, w_attn, w_ssm):
    t, d = x2d.shape
    return pl.pallas_call(
        _out_proj_kernel,
        grid=(t // ROW_TILE,),
        in_specs=[
            pl.BlockSpec((ROW_TILE, d), lambda i: (i, 0)),
            pl.BlockSpec((ROW_TILE, ATTN_WIDTH), lambda i: (i, 0)),
            pl.BlockSpec((ROW_TILE, SSM_WIDTH), lambda i: (i, 0)),
            pl.BlockSpec((ATTN_WIDTH, d), lambda i: (0, 0)),
            pl.BlockSpec((SSM_WIDTH, d), lambda i: (0, 0)),
        ],
        out_specs=pl.BlockSpec((ROW_TILE, d), lambda i: (i, 0)),
        out_shape=jax.ShapeDtypeStruct((t, d), F32),
        compiler_params=pltpu.CompilerParams(
            dimension_semantics=("parallel",), vmem_limit_bytes=VMEM_LIMIT),
        name="out_proj",
    )(x2d, attn2d, ssm2d, w_attn, w_ssm)


def _conv_glu_kernel(x_ref, halo_ref, xres_ref, nw_ref, wg_ref, wv_ref, cwg_ref, cwv_ref, cbg_ref, cbv_ref,
                     wd_ref, o_ref, h_s, ug_s, uv_s, act_s, *, tiles_per_seq, n_up):
    tm = x_ref.shape[0]
    tf = wg_ref.shape[1]
    i = pl.program_id(0)
    f = pl.program_id(1)

    def up_dots(slot):
        ug_s[slot] = jnp.dot(h_s[...], wg_ref[...], preferred_element_type=F32)
        uv_s[slot] = jnp.dot(h_s[...], wv_ref[...], preferred_element_type=F32)

    def conv(u_s, slot, cw_ref, cb_ref, r0):
        first = FFN_HALO - (FFN_CONV - 1) + r0
        out = cb_ref[...] + cw_ref[0:1, :] * u_s[slot, first:first + CONV_ROWS, :]
        for j in range(1, FFN_CONV):
            out = out + cw_ref[j:j + 1, :] * u_s[slot, first + j:first + j + CONV_ROWS, :]
        return out

    def conv_act(slot, chunk):
        for r0 in range(0, tm, CONV_ROWS):
            gate = conv(ug_s, slot, cwg_ref, cbg_ref, r0)
            val = conv(uv_s, slot, cwv_ref, cbv_ref, r0)
            act_s[chunk, r0:r0 + CONV_ROWS, :] = (gate * jax.nn.sigmoid(gate) * val).astype(BF16)

    @pl.when(f == 0)
    def _():
        _norm_rows(x_ref, nw_ref, h_s, FFN_HALO, tm)
        keep = jnp.where(i % tiles_per_seq == 0, 0.0, 1.0)
        h_s[0:FFN_HALO, :] = (_rms(halo_ref[...], nw_ref[...]) * keep).astype(BF16)
        up_dots(0)

    for parity in (0, 1):
        @pl.when((f >= 1) & (f < n_up) & (f % 2 == parity))
        def _():
            conv_act(1 - parity, f - 1)
            up_dots(parity)

    @pl.when(f == n_up)
    def _():
        conv_act((n_up - 1) % 2, n_up - 1)

    @pl.when(f >= n_up)
    def _():
        acc = xres_ref[...]
        for c in range(n_up):
            acc = acc + jnp.dot(act_s[c], wd_ref[c * tf:(c + 1) * tf, :], preferred_element_type=F32)
        o_ref[...] = acc


def _conv_glu(x2d, nw, w_up, conv_w, conv_b, w_down, seq_len):
    t, d = x2d.shape
    tm, tf, tn = ROW_TILE, COL_TILE, COL_TILE
    n_up = D_FF // tf
    n_down = d // tn
    halo_blocks = tm // FFN_HALO
    up = lambda f: jnp.minimum(f, n_up - 1)
    cv = lambda f: jnp.clip(f - 1, 0, n_up - 1)
    down = lambda f: jnp.maximum(f - n_up, 0)
    return pl.pallas_call(
        functools.partial(_conv_glu_kernel, tiles_per_seq=seq_len // tm, n_up=n_up),
        grid=(t // tm, n_up + n_down),
        in_specs=[
            pl.BlockSpec((tm, d), lambda i, f: (i, 0)),
            pl.BlockSpec((FFN_HALO, d), lambda i, f: (jnp.maximum(i * halo_blocks - 1, 0), 0)),
            pl.BlockSpec((tm, tn), lambda i, f: (i, down(f))),
            pl.BlockSpec((1, d), lambda i, f: (0, 0)),
            pl.BlockSpec((d, tf), lambda i, f: (0, up(f))),
            pl.BlockSpec((d, tf), lambda i, f: (0, n_up + up(f))),
            pl.BlockSpec((FFN_CONV, tf), lambda i, f: (0, cv(f))),
            pl.BlockSpec((FFN_CONV, tf), lambda i, f: (0, n_up + cv(f))),
            pl.BlockSpec((1, tf), lambda i, f: (0, cv(f))),
            pl.BlockSpec((1, tf), lambda i, f: (0, n_up + cv(f))),
            pl.BlockSpec((D_FF, tn), lambda i, f: (0, down(f))),
        ],
        out_specs=pl.BlockSpec((tm, tn), lambda i, f: (i, down(f))),
        out_shape=jax.ShapeDtypeStruct((t, d), F32),
        scratch_shapes=[
            pltpu.VMEM((FFN_HALO + tm, d), BF16),
            pltpu.VMEM((2, FFN_HALO + tm, tf), F32),
            pltpu.VMEM((2, FFN_HALO + tm, tf), F32),
            pltpu.VMEM((n_up, tm, tf), BF16),
        ],
        compiler_params=pltpu.CompilerParams(
            dimension_semantics=("parallel", "arbitrary"), vmem_limit_bytes=VMEM_LIMIT),
        name="conv_glu",
    )(x2d, x2d, x2d, nw, w_up, w_up, conv_w, conv_w, conv_b, conv_b, w_down)


def _pad_lanes(v):
    return jnp.pad(v.reshape(1, -1), ((0, 0), (0, LANES - v.shape[-1])))


def _layer(x2d, batch, seq, norm1_w, w_in, q_norm_w, k_norm_w, ssm_conv_w, ssm_conv_b, dt_bias, a_log,
           d_skip, ssm_norm_w, w_out, norm2_w, w_up, ffn_conv_w, ffn_conv_b, w_down):
    w_main = w_in[:, :PROJ_MAIN].astype(BF16)
    w_dt = jnp.pad(w_in[:, PROJ_MAIN:], ((0, 0), (0, LANES - SSM_HEADS))).astype(BF16)
    proj, dt_raw = _in_proj(x2d, norm1_w.reshape(1, -1), w_main, w_dt)
    proj3 = proj.reshape(batch, seq, PROJ_MAIN)

    attn = _moba(proj3, q_norm_w.reshape(1, -1), k_norm_w.reshape(1, -1))
    ssm = _ssd(proj3, dt_raw.reshape(batch, seq, LANES), ssm_conv_w, ssm_conv_b.reshape(1, -1),
               _pad_lanes(dt_bias), _pad_lanes(a_log),
               jnp.repeat(d_skip, SSM_HEAD_DIM).reshape(1, -1), ssm_norm_w.reshape(1, -1))

    w_out_b = w_out.astype(BF16)
    x1 = _out_proj(x2d, attn.reshape(-1, ATTN_WIDTH), ssm.reshape(-1, SSM_WIDTH),
                   w_out_b[:ATTN_WIDTH], w_out_b[ATTN_WIDTH:])
    return _conv_glu(x1, norm2_w.reshape(1, -1), w_up.astype(BF16), ffn_conv_w, ffn_conv_b.reshape(1, -1),
                     w_down.astype(BF16), seq)


def kernel(x, norm1_w, w_in, q_norm_w, k_norm_w, ssm_conv_w, ssm_conv_b, dt_bias, a_log, d_skip, ssm_norm_w,
           w_out, norm2_w, w_up, ffn_conv_w, ffn_conv_b, w_down):
    batch, seq, d = x.shape
    x2d = x.reshape(batch * seq, d)
    for i in range(norm1_w.shape[0]):
        x2d = _layer(x2d, batch, seq, norm1_w[i], w_in[i], q_norm_w[i], k_norm_w[i], ssm_conv_w[i],
                     ssm_conv_b[i], dt_bias[i], a_log[i], d_skip[i], ssm_norm_w[i], w_out[i], norm2_w[i],
                     w_up[i], ffn_conv_w[i], ffn_conv_b[i], w_down[i])
    return x2d.reshape(batch, seq, d)
```

```python
import functools

import jax
import jax.numpy as jnp
from jax import lax
from jax.experimental import pallas as pl
from jax.experimental.pallas import tpu as pltpu

F32 = jnp.float32
BF16 = jnp.bfloat16

D_MODEL = 2048
ATTN_WIDTH = 1024
ATTN_HEAD_DIM = 128
ATTN_HEADS = ATTN_WIDTH // ATTN_HEAD_DIM
MOBA_BLOCK = 256
MOBA_TOPK = 3
SSM_WIDTH = 1024
SSM_HEAD_DIM = 64
SSM_HEADS = SSM_WIDTH // SSM_HEAD_DIM
SSM_GROUPS = 2
SSM_HEADS_PER_GROUP = SSM_HEADS // SSM_GROUPS
SSM_GROUP_WIDTH = SSM_WIDTH // SSM_GROUPS
SSM_STATE = 128
SSM_CONV = 4
XBC_WIDTH = SSM_WIDTH + 2 * SSM_GROUPS * SSM_STATE
PROJ_MAIN = 3 * ATTN_WIDTH + SSM_WIDTH + XBC_WIDTH
D_FF = 5632
FFN_CONV = 3
EPS = 1e-6
LOG2_E = 1.4426950408889634

LANES = 128
MXU_COLS = 256
SUBLANES = 8
VMEM_LIMIT = 56 * 1024 * 1024

SSD_CHUNK = 256
ROW_TILE = 512
IN_ROW_TILE = 1024
COL_TILE = 512
FFN_HALO = 16
NORM_ROWS = 64
CONV_ROWS = 64

_NT = (((1,), (1,)), ((), ()))


def _rms(x, w):
    return x * lax.rsqrt(jnp.mean(x * x, axis=-1, keepdims=True) + EPS) * w


def _norm_rows(x_ref, nw_ref, h_ref, dst_off, n_rows):
    def body(c, carry):
        r = pl.multiple_of(c * NORM_ROWS, NORM_ROWS)
        h_ref[pl.ds(dst_off + r, NORM_ROWS), :] = _rms(x_ref[pl.ds(r, NORM_ROWS), :], nw_ref[...]).astype(BF16)
        return carry
    lax.fori_loop(0, n_rows // NORM_ROWS, body, 0)


def _in_proj_kernel(x_ref, nw_ref, w_ref, wdt_ref, o_ref, dt_ref, h_ref):
    @pl.when(pl.program_id(1) == 0)
    def _():
        _norm_rows(x_ref, nw_ref, h_ref, 0, x_ref.shape[0])
        dt_ref[...] = jnp.dot(h_ref[...], wdt_ref[...], preferred_element_type=F32)
    o_ref[...] = jnp.dot(h_ref[...], w_ref[...], preferred_element_type=F32)


def _in_proj(x2d, nw, w_main, w_dt):
    t, d = x2d.shape
    n = w_main.shape[1]
    tm = IN_ROW_TILE
    return pl.pallas_call(
        _in_proj_kernel,
        grid=(t // tm, n // COL_TILE),
        in_specs=[
            pl.BlockSpec((tm, d), lambda i, j: (i, 0)),
            pl.BlockSpec((1, d), lambda i, j: (0, 0)),
            pl.BlockSpec((d, COL_TILE), lambda i, j: (0, j)),
            pl.BlockSpec((d, LANES), lambda i, j: (0, 0)),
        ],
        out_specs=[
            pl.BlockSpec((tm, COL_TILE), lambda i, j: (i, j)),
            pl.BlockSpec((tm, LANES), lambda i, j: (i, 0)),
        ],
        out_shape=[jax.ShapeDtypeStruct((t, n), F32), jax.ShapeDtypeStruct((t, LANES), F32)],
        scratch_shapes=[pltpu.VMEM((tm, d), BF16)],
        compiler_params=pltpu.CompilerParams(
            dimension_semantics=("parallel", "arbitrary"), vmem_limit_bytes=VMEM_LIMIT),
        name="in_proj",
    )(x2d, nw, w_main, w_dt)


def _pair_schedule(nb, width):
    remaining = {i: list(range(i)) for i in range(1, nb)}
    qi, kj = [], []
    while any(remaining.values()):
        live = sorted((i for i in remaining if remaining[i]), key=lambda i: -len(remaining[i]))
        if len(live) < width:
            return None
        for i in live[:width]:
            qi.append(i)
            kj.append(remaining[i].pop())
    return qi, kj


def _moba_schedule(nb):
    for width in (4, 2, 1):
        sched = _pair_schedule(nb, width)
        if sched is not None and width <= nb - 1:
            qi, kj = sched
            if (len(qi) // width) % 2:
                qi = qi + list(range(1, width + 1))
                kj = kj + [nb - 1] * width
            return width, (qi, kj)
    raise ValueError(f"no MoBA pair schedule for {nb} blocks")


def _moba_kernel(qi_ref, kj_ref, q_ref, k_ref, v_ref, qw_ref, kw_ref, o_ref,
                 qf_s, qb_s, k_s, vt_s, km_s, sel_s, m_s, l_s, acc_s,
                 sc_a, sc_b, p_a, p_b, al_a, al_b, *, nb, width, n_groups):
    blk = MOBA_BLOCK
    scale = ATTN_HEAD_DIM ** -0.5 * LOG2_E
    for j in range(nb):
        rows = slice(j * blk, (j + 1) * blk)
        kn = _rms(k_ref[rows, :], kw_ref[...])
        k_s[j] = kn.astype(BF16)
        km_s[j:j + 1, :] = jnp.mean(kn, axis=0, keepdims=True)
        vt_s[j] = v_ref[rows, :].T.astype(BF16)
        qn = _rms(q_ref[rows, :], qw_ref[...]) * scale
        qf_s[j] = qn
        qb_s[j] = qn.astype(BF16)

    key_i = lax.broadcasted_iota(jnp.int32, (blk, blk), 0)
    qry_i = lax.broadcasted_iota(jnp.int32, (blk, blk), 1)
    bid = lax.broadcasted_iota(jnp.int32, (nb, blk), 0)

    def init_group(t, carry):
        ids = [t * width + u for u in range(width)]
        gates = [lax.dot_general(km_s[...], qf_s[i], _NT, precision=lax.Precision.HIGHEST,
                                 preferred_element_type=F32) for i in ids]
        owns = [lax.dot_general(k_s[i], qb_s[i], _NT, preferred_element_type=F32) for i in ids]
        for i, gate in zip(ids, gates):
            past = bid < i
            g = jnp.where(past, gate, -jnp.inf)
            rank = jnp.zeros((nb, blk), jnp.int32)
            for jp in range(nb):
                row = g[jp:jp + 1, :]
                rank = rank + jnp.where(row > g, 1, jnp.where(row == g, jnp.where(bid > jp, 1, 0), 0))
            sel = jnp.where(past, jnp.where(rank < MOBA_TOPK, 1.0, 0.0), 0.0)
            for jp in range(nb):
                sel_s[i * nb + jp] = sel[jp:jp + 1, :]
        for i, s in zip(ids, owns):
            s = jnp.where(key_i <= qry_i, s, -jnp.inf)
            m = jnp.max(s, axis=0, keepdims=True)
            p = jnp.exp2(s - m)
            m_s[i] = m
            l_s[i] = jnp.sum(p, axis=0, keepdims=True)
            acc_s[i] = jnp.dot(vt_s[i], p.astype(BF16), preferred_element_type=F32)
        return carry

    lax.fori_loop(0, nb // width, init_group, 0)

    def group_pairs(g):
        return [(qi_ref[g * width + u], kj_ref[g * width + u]) for u in range(width)]

    def score_group(g, sc):
        for u, (i, j) in enumerate(group_pairs(g)):
            sc[u] = lax.dot_general(k_s[j], qb_s[i], _NT, preferred_element_type=F32)

    def softmax_group(g, sc, p_buf, al_buf):
        pairs = group_pairs(g)
        m_old = [m_s[i] for i, _ in pairs]
        upd = []
        for u, ((i, j), m0) in enumerate(zip(pairs, m_old)):
            s = jnp.where(sel_s[i * nb + j] > 0.0, sc[u], -jnp.inf)
            m1 = jnp.maximum(m0, jnp.max(s, axis=0, keepdims=True))
            p = jnp.exp2(s - m1)
            p_buf[u] = p.astype(BF16)
            alpha = jnp.exp2(m0 - m1)
            al_buf[u] = alpha
            upd.append((i, m1, alpha, jnp.sum(p, axis=0, keepdims=True)))
        for i, m1, alpha, psum in upd:
            m_s[i] = m1
            l_s[i] = alpha * l_s[i] + psum

    def pv_group(g, p_buf, al_buf):
        pairs = group_pairs(g)
        pvs = [jnp.dot(vt_s[j], p_buf[u], preferred_element_type=F32) for u, (_, j) in enumerate(pairs)]
        for u, ((i, _), pv) in enumerate(zip(pairs, pvs)):
            acc_s[i] = al_buf[u] * acc_s[i] + pv

    def step(g, sc_cur, sc_next, p_cur, al_cur, p_prev, al_prev):
        pv_group(jnp.maximum(g - 1, 0), p_prev, al_prev)
        score_group(jnp.minimum(g + 1, n_groups - 1), sc_next)
        softmax_group(g, sc_cur, p_cur, al_cur)

    p_b[...] = jnp.zeros(p_b.shape, BF16)
    al_b[...] = jnp.ones(al_b.shape, F32)
    score_group(0, sc_a)

    def two_steps(t, carry):
        step(2 * t, sc_a, sc_b, p_a, al_a, p_b, al_b)
        step(2 * t + 1, sc_b, sc_a, p_b, al_b, p_a, al_a)
        return carry

    lax.fori_loop(0, n_groups // 2, two_steps, 0)
    pv_group(n_groups - 1, p_b, al_b)

    def finish_two(t, carry):
        for i in (2 * t, 2 * t + 1):
            o_ref[pl.ds(pl.multiple_of(i * blk, blk), blk), :] = (acc_s[i] / l_s[i]).T.astype(o_ref.dtype)
        return carry

    lax.fori_loop(0, nb // 2, finish_two, 0)


def _moba(proj3, qw, kw):
    b, s, _ = proj3.shape
    nb = s // MOBA_BLOCK
    assert nb % 2 == 0
    dh = ATTN_HEAD_DIM
    blk = MOBA_BLOCK
    width, (qi, kj) = _moba_schedule(nb)
    assert nb % width == 0 and (len(qi) // width) % 2 == 0
    smem = pl.BlockSpec(memory_space=pltpu.SMEM)
    return pl.pallas_call(
        functools.partial(_moba_kernel, nb=nb, width=width, n_groups=len(qi) // width),
        grid=(b, ATTN_HEADS),
        in_specs=[
            smem,
            smem,
            pl.BlockSpec((None, s, dh), lambda bi, h: (bi, 0, h)),
            pl.BlockSpec((None, s, dh), lambda bi, h: (bi, 0, ATTN_HEADS + h)),
            pl.BlockSpec((None, s, dh), lambda bi, h: (bi, 0, 2 * ATTN_HEADS + h)),
            pl.BlockSpec((1, dh), lambda bi, h: (0, 0)),
            pl.BlockSpec((1, dh), lambda bi, h: (0, 0)),
        ],
        out_specs=pl.BlockSpec((None, s, dh), lambda bi, h: (bi, 0, h)),
        out_shape=jax.ShapeDtypeStruct((b, s, ATTN_WIDTH), BF16),
        scratch_shapes=[
            pltpu.VMEM((nb, blk, dh), F32),
            pltpu.VMEM((nb, blk, dh), BF16),
            pltpu.VMEM((nb, blk, dh), BF16),
            pltpu.VMEM((nb, dh, blk), BF16),
            pltpu.VMEM((nb, dh), F32),
            pltpu.VMEM((nb * nb, 1, blk), F32),
            pltpu.VMEM((nb, 1, blk), F32),
            pltpu.VMEM((nb, 1, blk), F32),
            pltpu.VMEM((nb, dh, blk), F32),
            pltpu.VMEM((width, blk, blk), F32),
            pltpu.VMEM((width, blk, blk), F32),
            pltpu.VMEM((width, blk, blk), BF16),
            pltpu.VMEM((width, blk, blk), BF16),
            pltpu.VMEM((width, 1, blk), F32),
            pltpu.VMEM((width, 1, blk), F32),
        ],
        compiler_params=pltpu.CompilerParams(
            dimension_semantics=("parallel", "parallel"), vmem_limit_bytes=VMEM_LIMIT),
        name="moba",
    )(jnp.asarray(qi, jnp.int32), jnp.asarray(kj, jnp.int32), proj3, proj3, proj3, qw, kw)


def _pair_cols(arr, i0):
    rows = arr.shape[0]
    lo = lax.broadcasted_iota(jnp.int32, (rows, LANES), 1) < SSM_HEAD_DIM
    a0 = jnp.broadcast_to(arr[:, i0:i0 + 1], (rows, LANES))
    a1 = jnp.broadcast_to(arr[:, i0 + 1:i0 + 2], (rows, LANES))
    return jnp.where(lo, a0, a1)


def _ssd_kernel(xs_ref, bc_ref, z_ref, dt_ref, cw_ref, cb_ref, dtb_ref, alog_ref, dsk_ref, nw_ref, o_ref,
                ext_s, xbc_s, ht_s, y_s, wx_s, dec_s):
    L = xs_ref.shape[0]
    pad = SUBLANES

    @pl.when(pl.program_id(1) == 0)
    def _():
        ext_s[0:pad, :] = jnp.zeros((pad, XBC_WIDTH), F32)
        ht_s[...] = jnp.zeros(ht_s.shape, F32)

    ext_s[pad:pad + L, 0:SSM_WIDTH] = xs_ref[...]
    ext_s[pad:pad + L, SSM_WIDTH:XBC_WIDTH] = bc_ref[...]
    for cblk in range(XBC_WIDTH // LANES):
        cols = slice(cblk * LANES, (cblk + 1) * LANES)
        conv = cb_ref[:, cols] + cw_ref[0:1, cols] * ext_s[pad - 3:pad - 3 + L, cols]
        for j in range(1, SSM_CONV):
            conv = conv + cw_ref[j:j + 1, cols] * ext_s[pad - 3 + j:pad - 3 + j + L, cols]
        xbc_s[:, cols] = conv * jax.nn.sigmoid(conv)
    ext_s[0:pad, :] = ext_s[L:L + pad, :]

    dtv = dt_ref[...] + dtb_ref[...]
    dt = jnp.maximum(dtv, 0.0) + jnp.log1p(jnp.exp(-jnp.abs(dtv)))
    la = dt * (-jnp.exp(alog_ref[...]))
    row_i = lax.broadcasted_iota(jnp.int32, (L, L), 0)
    col_i = lax.broadcasted_iota(jnp.int32, (L, L), 1)
    tril = row_i >= col_i
    acol = jnp.dot(jnp.where(tril, 1.0, 0.0), la, precision=lax.Precision.HIGHEST,
                   preferred_element_type=F32)
    arow = acol.T
    ecol = jnp.exp(acol)
    aend = acol[L - 1:L, :]
    wcol = jnp.exp(aend - acol)
    eend = jnp.exp(aend)
    lo = lax.broadcasted_iota(jnp.int32, (L, LANES), 1) < SSM_HEAD_DIM

    for g in range(SSM_GROUPS):
        b_off = SSM_WIDTH + g * SSM_STATE
        c_off = SSM_WIDTH + SSM_GROUPS * SSM_STATE + g * SSM_STATE
        bg = xbc_s[:, b_off:b_off + SSM_STATE]
        cg = xbc_s[:, c_off:c_off + SSM_STATE].astype(BF16)
        cb = lax.dot_general(cg, bg.astype(BF16), _NT, preferred_element_type=F32)
        cbm = jnp.where(tril, cb, 0.0)
        ht = ht_s[g]
        ch = jnp.dot(cg, ht.astype(BF16), preferred_element_type=F32)
        for pr in range(SSM_HEADS_PER_GROUP // 2):
            i0 = g * SSM_HEADS_PER_GROUP + 2 * pr
            lanes = slice(i0 * SSM_HEAD_DIM, i0 * SSM_HEAD_DIM + LANES)
            gl = slice(pr * LANES, (pr + 1) * LANES)
            xs_p = xbc_s[:, lanes]
            xdt = xs_p * _pair_cols(dt, i0)
            yd = None
            for hh in range(2):
                idx = i0 + hh
                seg = acol[:, idx:idx + 1] - arow[idx:idx + 1, :]
                mm = (cbm * jnp.exp(jnp.minimum(seg, 0.0))).astype(BF16)
                xh = jnp.where(lo, xdt, 0.0) if hh == 0 else jnp.where(lo, 0.0, xdt)
                part = jnp.dot(mm, xh.astype(BF16), preferred_element_type=F32)
                yd = part if yd is None else yd + part
            y_off = ch[:, gl] * _pair_cols(ecol, i0)
            y_s[:, lanes] = yd + y_off + dsk_ref[:, lanes] * xs_p
            wx_s[:, gl] = (xdt * _pair_cols(wcol, i0)).astype(BF16)
            dec_s[:, gl] = _pair_cols(eend, i0)
        ht_s[g] = ht * dec_s[...] + jnp.dot(bg.T.astype(BF16), wx_s[...], preferred_element_type=F32)

    for g in range(SSM_GROUPS):
        cols = slice(g * SSM_GROUP_WIDTH, (g + 1) * SSM_GROUP_WIDTH)
        zz = z_ref[:, cols]
        yg = y_s[:, cols] * (zz * jax.nn.sigmoid(zz))
        o_ref[:, cols] = _rms(yg, nw_ref[:, cols]).astype(o_ref.dtype)


def _ssd(proj3, dt3, conv_w, conv_b, dt_bias, a_log, d_skip, norm_w):
    b, s, _ = proj3.shape
    L = SSD_CHUNK
    full = lambda shape: pl.BlockSpec(shape, lambda bi, c: (0,) * len(shape))
    return pl.pallas_call(
        _ssd_kernel,
        grid=(b, s // L),
        in_specs=[
            pl.BlockSpec((None, L, SSM_WIDTH), lambda bi, c: (bi, c, (3 * ATTN_WIDTH + SSM_WIDTH) // SSM_WIDTH)),
            pl.BlockSpec((None, L, XBC_WIDTH - SSM_WIDTH),
                         lambda bi, c: (bi, c, (3 * ATTN_WIDTH + 2 * SSM_WIDTH) // (XBC_WIDTH - SSM_WIDTH))),
            pl.BlockSpec((None, L, SSM_WIDTH), lambda bi, c: (bi, c, 3 * ATTN_WIDTH // SSM_WIDTH)),
            pl.BlockSpec((None, L, LANES), lambda bi, c: (bi, c, 0)),
            full((SSM_CONV, XBC_WIDTH)),
            full((1, XBC_WIDTH)),
            full((1, LANES)),
            full((1, LANES)),
            full((1, SSM_WIDTH)),
            full((1, SSM_WIDTH)),
        ],
        out_specs=pl.BlockSpec((None, L, SSM_WIDTH), lambda bi, c: (bi, c, 0)),
        out_shape=jax.ShapeDtypeStruct((b, s, SSM_WIDTH), BF16),
        scratch_shapes=[
            pltpu.VMEM((L + 2 * SUBLANES, XBC_WIDTH), F32),
            pltpu.VMEM((L, XBC_WIDTH), F32),
            pltpu.VMEM((SSM_GROUPS, SSM_STATE, SSM_GROUP_WIDTH), F32),
            pltpu.VMEM((L, SSM_WIDTH), F32),
            pltpu.VMEM((L, SSM_GROUP_WIDTH), BF16),
            pltpu.VMEM((1, SSM_GROUP_WIDTH), F32),
        ],
        compiler_params=pltpu.CompilerParams(
            dimension_semantics=("parallel", "arbitrary"), vmem_limit_bytes=VMEM_LIMIT),
        name="ssd",
    )(proj3, proj3, proj3, dt3, conv_w, conv_b, dt_bias, a_log, d_skip, norm_w)


def _out_proj_kernel(x_ref, a_ref, s_ref, wa_ref, ws_ref, o_ref):
    o_ref[...] = (x_ref[...]
                  + jnp.dot(a_ref[...], wa_ref[...], preferred_element_type=F32)
                  + jnp.dot(s_ref[...], ws_ref[...], preferred_element_type=F32))


def _out_proj(x2d, attn2d, ssm2d, w_attn, w_ssm):
    t, d = x2d.shape
    return pl.pallas_call(
        _out_proj_kernel,
        grid=(t // ROW_TILE,),
        in_specs=[
            pl.BlockSpec((ROW_TILE, d), lambda i: (i, 0)),
            pl.BlockSpec((ROW_TILE, ATTN_WIDTH), lambda i: (i, 0)),
            pl.BlockSpec((ROW_TILE, SSM_WIDTH), lambda i: (i, 0)),
            pl.BlockSpec((ATTN_WIDTH, d), lambda i: (0, 0)),
            pl.BlockSpec((SSM_WIDTH, d), lambda i: (0, 0)),
        ],
        out_specs=pl.BlockSpec((ROW_TILE, d), lambda i: (i, 0)),
        out_shape=jax.ShapeDtypeStruct((t, d), F32),
        compiler_params=pltpu.CompilerParams(
            dimension_semantics=("parallel",), vmem_limit_bytes=VMEM_LIMIT),
        name="out_proj",
    )(x2d, attn2d, ssm2d, w_attn, w_ssm)


def _conv_glu_kernel(x_ref, halo_ref, xres_ref, nw_ref, wg_ref, wv_ref, cwg_ref, cwv_ref, cbg_ref, cbv_ref,
                     wd_ref, o_ref, h_s, hp_s, ug0_s, uv0_s, ug1_s, uv1_s, act_s, op_s, *, tiles_per_seq, n_up):
    tm = x_ref.shape[0]
    tf = wg_ref.shape[1]
    span = tm // SUBLANES
    halo = FFN_HALO
    i = pl.program_id(0)
    f = pl.program_id(1)

    u_bufs = ((ug0_s, uv0_s), (ug1_s, uv1_s))

    n_parts = 2 * (tf // MXU_COLS)

    def up_dot_part(slot, part):
        u_s, w_ref = ((u_bufs[slot][0], wg_ref), (u_bufs[slot][1], wv_ref))[part // (tf // MXU_COLS)]
        cols = slice((part % (tf // MXU_COLS)) * MXU_COLS, (part % (tf // MXU_COLS) + 1) * MXU_COLS)
        u_s[:, cols] = jnp.dot(hp_s[...], w_ref[:, cols], preferred_element_type=F32)

    def up_fix(slot):
        for u_s in u_bufs[slot]:
            before = [jnp.concatenate([u_s[halo - d:halo - d + 1, :],
                                       u_s[halo + tm - d * SUBLANES:halo + tm - d * SUBLANES + SUBLANES - 1, :]],
                                      axis=0) for d in (1, 2)]
            u_s[halo - SUBLANES:halo, :] = before[0]
            u_s[halo - 2 * SUBLANES:halo - SUBLANES, :] = before[1]

    def conv(u_s, cw_ref, cb_ref, r0, cols):
        out = cb_ref[:, cols]
        for j in range(FFN_CONV):
            first = halo - (FFN_CONV - 1 - j) * SUBLANES + r0
            out = out + cw_ref[j:j + 1, cols] * u_s[first:first + CONV_ROWS, cols]
        return out

    def conv_act(slot, chunk, part=0, parts=1):
        ug_s, uv_s = u_bufs[slot]
        for c0 in range(part * (tf // parts), (part + 1) * (tf // parts), LANES):
            cols = slice(c0, c0 + LANES)
            for r0 in range(0, tm, CONV_ROWS):
                gate = conv(ug_s, cwg_ref, cbg_ref, r0, cols)
                val = conv(uv_s, cwv_ref, cbv_ref, r0, cols)
                act_s[chunk, r0:r0 + CONV_ROWS, cols] = (gate * jax.nn.sigmoid(gate) * val).astype(BF16)

    @pl.when(f == 0)
    def _():
        _norm_rows(x_ref, nw_ref, h_s, 0, tm)
        keep = jnp.where(i % tiles_per_seq == 0, 0.0, 1.0)
        hp_s[0:halo, :] = (_rms(halo_ref[...], nw_ref[...]) * keep).astype(BF16)
        pos = lax.broadcasted_iota(jnp.int32, (tm, tm), 0)
        row = lax.broadcasted_iota(jnp.int32, (tm, tm), 1)
        perm = jnp.where(row == (pos % SUBLANES) * span + pos // SUBLANES, 1.0, 0.0).astype(BF16)
        for c0 in range(0, h_s.shape[1], tf):
            hp_s[halo:, c0:c0 + tf] = jnp.dot(perm, h_s[:, c0:c0 + tf],
                                              preferred_element_type=F32).astype(BF16)
        for part in range(n_parts):
            up_dot_part(0, part)
        up_fix(0)

    for parity in (0, 1):
        @pl.when((f >= 1) & (f < n_up) & (f % 2 == parity))
        def _():
            for part in range(n_parts):
                up_dot_part(parity, part)
                conv_act(1 - parity, f - 1, part, n_parts)
            up_fix(parity)

    @pl.when(f == n_up)
    def _():
        conv_act((n_up - 1) % 2, n_up - 1)

    @pl.when(f >= n_up)
    def _():
        acc = jnp.dot(act_s[0], wd_ref[0:tf, :], preferred_element_type=F32)
        for c in range(1, n_up):
            acc = acc + jnp.dot(act_s[c], wd_ref[c * tf:(c + 1) * tf, :], preferred_element_type=F32)
        for c in range(op_s.shape[0]):
            op_s[c] = acc[:, c * LANES:(c + 1) * LANES]
        for a in range(SUBLANES):
            rows = slice(a * span, (a + 1) * span)
            for c in range(op_s.shape[0]):
                cols = slice(c * LANES, (c + 1) * LANES)
                o_ref[rows, cols] = xres_ref[rows, cols] + op_s[c, pl.ds(a, span, stride=SUBLANES), :]


def _conv_glu(x2d, nw, w_up, conv_w, conv_b, w_down, seq_len):
    t, d = x2d.shape
    tm, tf, tn = ROW_TILE, COL_TILE, COL_TILE
    n_up = D_FF // tf
    n_down = d // tn
    halo_blocks = tm // FFN_HALO
    up = lambda f: jnp.minimum(f, n_up - 1)
    cv = lambda f: jnp.clip(f - 1, 0, n_up - 1)
    down = lambda f: jnp.maximum(f - n_up, 0)
    return pl.pallas_call(
        functools.partial(_conv_glu_kernel, tiles_per_seq=seq_len // tm, n_up=n_up),
        grid=(t // tm, n_up + n_down),
        in_specs=[
            pl.BlockSpec((tm, d), lambda i, f: (i, 0)),
            pl.BlockSpec((FFN_HALO, d), lambda i, f: (jnp.maximum(i * halo_blocks - 1, 0), 0)),
            pl.BlockSpec((tm, tn), lambda i, f: (i, down(f))),
            pl.BlockSpec((1, d), lambda i, f: (0, 0)),
            pl.BlockSpec((d, tf), lambda i, f: (0, up(f))),
            pl.BlockSpec((d, tf), lambda i, f: (0, n_up + up(f))),
            pl.BlockSpec((FFN_CONV, tf), lambda i, f: (0, cv(f))),
            pl.BlockSpec((FFN_CONV, tf), lambda i, f: (0, n_up + cv(f))),
            pl.BlockSpec((1, tf), lambda i, f: (0, cv(f))),
            pl.BlockSpec((1, tf), lambda i, f: (0, n_up + cv(f))),
            pl.BlockSpec((D_FF, tn), lambda i, f: (0, down(f))),
        ],
        out_specs=pl.BlockSpec((tm, tn), lambda i, f: (i, down(f))),
        out_shape=jax.ShapeDtypeStruct((t, d), F32),
        scratch_shapes=[
            pltpu.VMEM((tm, d), BF16),
            pltpu.VMEM((FFN_HALO + tm, d), BF16),
            pltpu.VMEM((FFN_HALO + tm, tf), F32),
            pltpu.VMEM((FFN_HALO + tm, tf), F32),
            pltpu.VMEM((FFN_HALO + tm, tf), F32),
            pltpu.VMEM((FFN_HALO + tm, tf), F32),
            pltpu.VMEM((n_up, tm, tf), BF16),
            pltpu.VMEM((tn // LANES, tm, LANES), F32),
        ],
        compiler_params=pltpu.CompilerParams(
            dimension_semantics=("parallel", "arbitrary"), vmem_limit_bytes=VMEM_LIMIT),
        name="conv_glu",
    )(x2d, x2d, x2d, nw, w_up, w_up, conv_w, conv_w, conv_b, conv_b, w_down)


def _pad_lanes(v):
    return jnp.pad(v.reshape(1, -1), ((0, 0), (0, LANES - v.shape[-1])))


def _layer(x2d, batch, seq, norm1_w, w_in, q_norm_w, k_norm_w, ssm_conv_w, ssm_conv_b, dt_bias, a_log,
           d_skip, ssm_norm_w, w_out, norm2_w, w_up, ffn_conv_w, ffn_conv_b, w_down):
    w_main = w_in[:, :PROJ_MAIN].astype(BF16)
    w_dt = jnp.pad(w_in[:, PROJ_MAIN:], ((0, 0), (0, LANES - SSM_HEADS))).astype(BF16)
    proj, dt_raw = _in_proj(x2d, norm1_w.reshape(1, -1), w_main, w_dt)
    proj3 = proj.reshape(batch, seq, PROJ_MAIN)

    attn = _moba(proj3, q_norm_w.reshape(1, -1), k_norm_w.reshape(1, -1))
    ssm = _ssd(proj3, dt_raw.reshape(batch, seq, LANES), ssm_conv_w, ssm_conv_b.reshape(1, -1),
               _pad_lanes(dt_bias), _pad_lanes(a_log),
               jnp.repeat(d_skip, SSM_HEAD_DIM).reshape(1, -1), ssm_norm_w.reshape(1, -1))

    w_out_b = w_out.astype(BF16)
    x1 = _out_proj(x2d, attn.reshape(-1, ATTN_WIDTH), ssm.reshape(-1, SSM_WIDTH),
                   w_out_b[:ATTN_WIDTH], w_out_b[ATTN_WIDTH:])
    return _conv_glu(x1, norm2_w.reshape(1, -1), w_up.astype(BF16), ffn_conv_w, ffn_conv_b.reshape(1, -1),
                     w_down.astype(BF16), seq)


def kernel(x, norm1_w, w_in, q_norm_w, k_norm_w, ssm_conv_w, ssm_conv_b, dt_bias, a_log, d_skip, ssm_norm_w,
           w_out, norm2_w, w_up, ffn_conv_w, ffn_conv_b, w_down):
    batch, seq, d = x.shape
    x2d = x.reshape(batch * seq, d)
    for i in range(norm1_w.shape[0]):
        x2d = _layer(x2d, batch, seq, norm1_w[i], w_in[i], q_norm_w[i], k_norm_w[i], ssm_conv_w[i],
                     ssm_conv_b[i], dt_bias[i], a_log[i], d_skip[i], ssm_norm_w[i], w_out[i], norm2_w[i],
                     w_up[i], ffn_conv_w[i], ffn_conv_b[i], w_down[i])
    return x2d.reshape(batch, seq, d)
```

```python
import functools

import jax
import jax.numpy as jnp
from jax import lax
from jax.experimental import pallas as pl
from jax.experimental.pallas import tpu as pltpu

F32 = jnp.float32
BF16 = jnp.bfloat16

D_MODEL = 2048
ATTN_WIDTH = 1024
ATTN_HEAD_DIM = 128
ATTN_HEADS = ATTN_WIDTH // ATTN_HEAD_DIM
MOBA_BLOCK = 256
MOBA_TOPK = 3
SSM_WIDTH = 1024
SSM_HEAD_DIM = 64
SSM_HEADS = SSM_WIDTH // SSM_HEAD_DIM
SSM_GROUPS = 2
SSM_HEADS_PER_GROUP = SSM_HEADS // SSM_GROUPS
SSM_GROUP_WIDTH = SSM_WIDTH // SSM_GROUPS
SSM_STATE = 128
SSM_CONV = 4
XBC_WIDTH = SSM_WIDTH + 2 * SSM_GROUPS * SSM_STATE
PROJ_MAIN = 3 * ATTN_WIDTH + SSM_WIDTH + XBC_WIDTH
D_FF = 5632
FFN_CONV = 3
EPS = 1e-6
LOG2_E = 1.4426950408889634

LANES = 128
MXU_COLS = 256
SUBLANES = 8
VMEM_LIMIT = 56 * 1024 * 1024

SSD_CHUNK = 256
ROW_TILE = 512
IN_ROW_TILE = 1024
COL_TILE = 512
FFN_HALO = 16
NORM_ROWS = 64
CONV_ROWS = 64

_NT = (((1,), (1,)), ((), ()))


def _rms(x, w):
    return x * lax.rsqrt(jnp.mean(x * x, axis=-1, keepdims=True) + EPS) * w


def _norm_rows(x_ref, nw_ref, h_ref, dst_off, n_rows):
    def body(c, carry):
        r = pl.multiple_of(c * NORM_ROWS, NORM_ROWS)
        h_ref[pl.ds(dst_off + r, NORM_ROWS), :] = _rms(x_ref[pl.ds(r, NORM_ROWS), :], nw_ref[...]).astype(BF16)
        return carry
    lax.fori_loop(0, n_rows // NORM_ROWS, body, 0)


def _in_proj_kernel(x_ref, nw_ref, w_ref, wdt_ref, o_ref, dt_ref, h_ref):
    @pl.when(pl.program_id(1) == 0)
    def _():
        _norm_rows(x_ref, nw_ref, h_ref, 0, x_ref.shape[0])
        dt_ref[...] = jnp.dot(h_ref[...], wdt_ref[...], preferred_element_type=F32)
    o_ref[...] = jnp.dot(h_ref[...], w_ref[...], preferred_element_type=F32)


def _in_proj(x2d, nw, w_main, w_dt):
    t, d = x2d.shape
    n = w_main.shape[0] * COL_TILE
    tm = IN_ROW_TILE
    return pl.pallas_call(
        _in_proj_kernel,
        grid=(t // tm, n // COL_TILE),
        in_specs=[
            pl.BlockSpec((tm, d), lambda i, j: (i, 0)),
            pl.BlockSpec((1, d), lambda i, j: (0, 0)),
            pl.BlockSpec((None, d, COL_TILE), lambda i, j: (j, 0, 0)),
            pl.BlockSpec((d, LANES), lambda i, j: (0, 0)),
        ],
        out_specs=[
            pl.BlockSpec((tm, COL_TILE), lambda i, j: (i, j)),
            pl.BlockSpec((tm, LANES), lambda i, j: (i, 0)),
        ],
        out_shape=[jax.ShapeDtypeStruct((t, n), F32), jax.ShapeDtypeStruct((t, LANES), F32)],
        scratch_shapes=[pltpu.VMEM((tm, d), BF16)],
        compiler_params=pltpu.CompilerParams(
            dimension_semantics=("parallel", "arbitrary"), vmem_limit_bytes=VMEM_LIMIT),
        name="in_proj",
    )(x2d, nw, w_main, w_dt)


def _pair_schedule(nb, width):
    remaining = {i: list(range(i)) for i in range(1, nb)}
    qi, kj = [], []
    while any(remaining.values()):
        live = sorted((i for i in remaining if remaining[i]), key=lambda i: -len(remaining[i]))
        if len(live) < width:
            return None
        for i in live[:width]:
            qi.append(i)
            kj.append(remaining[i].pop())
    return qi, kj


def _moba_schedule(nb):
    for width in (4, 2, 1):
        sched = _pair_schedule(nb, width)
        if sched is not None and width <= nb - 1:
            qi, kj = sched
            if (len(qi) // width) % 2:
                qi = qi + list(range(1, width + 1))
                kj = kj + [nb - 1] * width
            return width, (qi, kj)
    raise ValueError(f"no MoBA pair schedule for {nb} blocks")


def _moba_kernel(qi_ref, kj_ref, q_ref, k_ref, v_ref, qw_ref, kw_ref, o_ref,
                 qf_s, qb_s, k_s, vt_s, km_s, sel_s, m_s, l_s, acc_s,
                 sc_a, sc_b, p_a, p_b, al_a, al_b, *, nb, width, n_groups):
    blk = MOBA_BLOCK
    scale = ATTN_HEAD_DIM ** -0.5 * LOG2_E
    for j in range(nb):
        rows = slice(j * blk, (j + 1) * blk)
        kn = _rms(k_ref[rows, :], kw_ref[...])
        k_s[j] = kn.astype(BF16)
        km_s[j:j + 1, :] = jnp.mean(kn, axis=0, keepdims=True)
        vt_s[j] = v_ref[rows, :].T.astype(BF16)
        qn = _rms(q_ref[rows, :], qw_ref[...]) * scale
        qf_s[j] = qn
        qb_s[j] = qn.astype(BF16)

    key_i = lax.broadcasted_iota(jnp.int32, (blk, blk), 0)
    qry_i = lax.broadcasted_iota(jnp.int32, (blk, blk), 1)
    bid = lax.broadcasted_iota(jnp.int32, (nb, blk), 0)

    def init_group(t, carry):
        ids = [t * width + u for u in range(width)]
        gates = [lax.dot_general(km_s[...], qf_s[i], _NT, precision=lax.Precision.HIGHEST,
                                 preferred_element_type=F32) for i in ids]
        owns = [lax.dot_general(k_s[i], qb_s[i], _NT, preferred_element_type=F32) for i in ids]
        for i, gate in zip(ids, gates):
            past = bid < i
            g = jnp.where(past, gate, -jnp.inf)
            rank = jnp.zeros((nb, blk), jnp.int32)
            for jp in range(nb):
                row = g[jp:jp + 1, :]
                rank = rank + jnp.where(row > g, 1, jnp.where(row == g, jnp.where(bid > jp, 1, 0), 0))
            sel = jnp.where(past, jnp.where(rank < MOBA_TOPK, 1.0, 0.0), 0.0)
            for jp in range(nb):
                sel_s[i * nb + jp] = sel[jp:jp + 1, :]
        for i, s in zip(ids, owns):
            s = jnp.where(key_i <= qry_i, s, -jnp.inf)
            m = jnp.max(s, axis=0, keepdims=True)
            p = jnp.exp2(s - m)
            m_s[i] = m
            l_s[i] = jnp.sum(p, axis=0, keepdims=True)
            acc_s[i] = jnp.dot(vt_s[i], p.astype(BF16), preferred_element_type=F32)
        return carry

    lax.fori_loop(0, nb // width, init_group, 0)

    def group_pairs(g):
        return [(qi_ref[g * width + u], kj_ref[g * width + u]) for u in range(width)]

    def score_group(g, sc):
        for u, (i, j) in enumerate(group_pairs(g)):
            sc[u] = lax.dot_general(k_s[j], qb_s[i], _NT, preferred_element_type=F32)

    def softmax_group(g, sc, p_buf, al_buf):
        pairs = group_pairs(g)
        m_old = [m_s[i] for i, _ in pairs]
        upd = []
        for u, ((i, j), m0) in enumerate(zip(pairs, m_old)):
            s = jnp.where(sel_s[i * nb + j] > 0.0, sc[u], -jnp.inf)
            m1 = jnp.maximum(m0, jnp.max(s, axis=0, keepdims=True))
            p = jnp.exp2(s - m1)
            p_buf[u] = p.astype(BF16)
            alpha = jnp.exp2(m0 - m1)
            al_buf[u] = alpha
            upd.append((i, m1, alpha, jnp.sum(p, axis=0, keepdims=True)))
        for i, m1, alpha, psum in upd:
            m_s[i] = m1
            l_s[i] = alpha * l_s[i] + psum

    def pv_group(g, p_buf, al_buf):
        pairs = group_pairs(g)
        pvs = [jnp.dot(vt_s[j], p_buf[u], preferred_element_type=F32) for u, (_, j) in enumerate(pairs)]
        for u, ((i, _), pv) in enumerate(zip(pairs, pvs)):
            acc_s[i] = al_buf[u] * acc_s[i] + pv

    def step(g, sc_cur, sc_next, p_cur, al_cur, p_prev, al_prev):
        pv_group(jnp.maximum(g - 1, 0), p_prev, al_prev)
        score_group(jnp.minimum(g + 1, n_groups - 1), sc_next)
        softmax_group(g, sc_cur, p_cur, al_cur)

    p_b[...] = jnp.zeros(p_b.shape, BF16)
    al_b[...] = jnp.ones(al_b.shape, F32)
    score_group(0, sc_a)

    def two_steps(t, carry):
        step(2 * t, sc_a, sc_b, p_a, al_a, p_b, al_b)
        step(2 * t + 1, sc_b, sc_a, p_b, al_b, p_a, al_a)
        return carry

    lax.fori_loop(0, n_groups // 2, two_steps, 0)
    pv_group(n_groups - 1, p_b, al_b)

    def finish_two(t, carry):
        for i in (2 * t, 2 * t + 1):
            o_ref[pl.ds(pl.multiple_of(i * blk, blk), blk), :] = (acc_s[i] / l_s[i]).T.astype(o_ref.dtype)
        return carry

    lax.fori_loop(0, nb // 2, finish_two, 0)


def _moba(proj3, qw, kw):
    b, s, _ = proj3.shape
    nb = s // MOBA_BLOCK
    assert nb % 2 == 0
    dh = ATTN_HEAD_DIM
    blk = MOBA_BLOCK
    width, (qi, kj) = _moba_schedule(nb)
    assert nb % width == 0 and (len(qi) // width) % 2 == 0
    smem = pl.BlockSpec(memory_space=pltpu.SMEM)
    return pl.pallas_call(
        functools.partial(_moba_kernel, nb=nb, width=width, n_groups=len(qi) // width),
        grid=(b, ATTN_HEADS),
        in_specs=[
            smem,
            smem,
            pl.BlockSpec((None, s, dh), lambda bi, h: (bi, 0, h)),
            pl.BlockSpec((None, s, dh), lambda bi, h: (bi, 0, ATTN_HEADS + h)),
            pl.BlockSpec((None, s, dh), lambda bi, h: (bi, 0, 2 * ATTN_HEADS + h)),
            pl.BlockSpec((1, dh), lambda bi, h: (0, 0)),
            pl.BlockSpec((1, dh), lambda bi, h: (0, 0)),
        ],
        out_specs=pl.BlockSpec((None, s, dh), lambda bi, h: (bi, 0, h)),
        out_shape=jax.ShapeDtypeStruct((b, s, ATTN_WIDTH), BF16),
        scratch_shapes=[
            pltpu.VMEM((nb, blk, dh), F32),
            pltpu.VMEM((nb, blk, dh), BF16),
            pltpu.VMEM((nb, blk, dh), BF16),
            pltpu.VMEM((nb, dh, blk), BF16),
            pltpu.VMEM((nb, dh), F32),
            pltpu.VMEM((nb * nb, 1, blk), F32),
            pltpu.VMEM((nb, 1, blk), F32),
            pltpu.VMEM((nb, 1, blk), F32),
            pltpu.VMEM((nb, dh, blk), F32),
            pltpu.VMEM((width, blk, blk), F32),
            pltpu.VMEM((width, blk, blk), F32),
            pltpu.VMEM((width, blk, blk), BF16),
            pltpu.VMEM((width, blk, blk), BF16),
            pltpu.VMEM((width, 1, blk), F32),
            pltpu.VMEM((width, 1, blk), F32),
        ],
        compiler_params=pltpu.CompilerParams(
            dimension_semantics=("parallel", "parallel"), vmem_limit_bytes=VMEM_LIMIT),
        name="moba",
    )(jnp.asarray(qi, jnp.int32), jnp.asarray(kj, jnp.int32), proj3, proj3, proj3, qw, kw)


def _pair_cols(arr, i0):
    rows = arr.shape[0]
    lo = lax.broadcasted_iota(jnp.int32, (rows, LANES), 1) < SSM_HEAD_DIM
    a0 = jnp.broadcast_to(arr[:, i0:i0 + 1], (rows, LANES))
    a1 = jnp.broadcast_to(arr[:, i0 + 1:i0 + 2], (rows, LANES))
    return jnp.where(lo, a0, a1)


def _ssd_kernel(xs_ref, bc_ref, z_ref, dt_ref, cw_ref, cb_ref, dtb_ref, alog_ref, dsk_ref, nw_ref, o_ref,
                ext_s, xbc_s, ht_s, y_s, wx_s, dec_s):
    L = xs_ref.shape[0]
    pad = SUBLANES

    @pl.when(pl.program_id(1) == 0)
    def _():
        ext_s[0:pad, :] = jnp.zeros((pad, XBC_WIDTH), F32)
        ht_s[...] = jnp.zeros(ht_s.shape, F32)

    ext_s[pad:pad + L, 0:SSM_WIDTH] = xs_ref[...]
    ext_s[pad:pad + L, SSM_WIDTH:XBC_WIDTH] = bc_ref[...]
    for cblk in range(XBC_WIDTH // LANES):
        cols = slice(cblk * LANES, (cblk + 1) * LANES)
        conv = cb_ref[:, cols] + cw_ref[0:1, cols] * ext_s[pad - 3:pad - 3 + L, cols]
        for j in range(1, SSM_CONV):
            conv = conv + cw_ref[j:j + 1, cols] * ext_s[pad - 3 + j:pad - 3 + j + L, cols]
        xbc_s[:, cols] = conv * jax.nn.sigmoid(conv)
    ext_s[0:pad, :] = ext_s[L:L + pad, :]

    dtv = dt_ref[...] + dtb_ref[...]
    dt = jnp.maximum(dtv, 0.0) + jnp.log1p(jnp.exp(-jnp.abs(dtv)))
    la = dt * (-jnp.exp(alog_ref[...]))
    row_i = lax.broadcasted_iota(jnp.int32, (L, L), 0)
    col_i = lax.broadcasted_iota(jnp.int32, (L, L), 1)
    tril = row_i >= col_i
    acol = jnp.dot(jnp.where(tril, 1.0, 0.0), la, precision=lax.Precision.HIGHEST,
                   preferred_element_type=F32)
    arow = acol.T
    ecol = jnp.exp(acol)
    aend = acol[L - 1:L, :]
    wcol = jnp.exp(aend - acol)
    eend = jnp.exp(aend)
    lo = lax.broadcasted_iota(jnp.int32, (L, LANES), 1) < SSM_HEAD_DIM

    for g in range(SSM_GROUPS):
        b_off = SSM_WIDTH + g * SSM_STATE
        c_off = SSM_WIDTH + SSM_GROUPS * SSM_STATE + g * SSM_STATE
        bg = xbc_s[:, b_off:b_off + SSM_STATE]
        cg = xbc_s[:, c_off:c_off + SSM_STATE].astype(BF16)
        cb = lax.dot_general(cg, bg.astype(BF16), _NT, preferred_element_type=F32)
        cbm = jnp.where(tril, cb, 0.0)
        ht = ht_s[g]
        ch = jnp.dot(cg, ht.astype(BF16), preferred_element_type=F32)
        for pr in range(SSM_HEADS_PER_GROUP // 2):
            i0 = g * SSM_HEADS_PER_GROUP + 2 * pr
            lanes = slice(i0 * SSM_HEAD_DIM, i0 * SSM_HEAD_DIM + LANES)
            gl = slice(pr * LANES, (pr + 1) * LANES)
            xs_p = xbc_s[:, lanes]
            xdt = xs_p * _pair_cols(dt, i0)
            yd = None
            for hh in range(2):
                idx = i0 + hh
                seg = acol[:, idx:idx + 1] - arow[idx:idx + 1, :]
                mm = (cbm * jnp.exp(jnp.minimum(seg, 0.0))).astype(BF16)
                xh = jnp.where(lo, xdt, 0.0) if hh == 0 else jnp.where(lo, 0.0, xdt)
                part = jnp.dot(mm, xh.astype(BF16), preferred_element_type=F32)
                yd = part if yd is None else yd + part
            y_off = ch[:, gl] * _pair_cols(ecol, i0)
            y_s[:, lanes] = yd + y_off + dsk_ref[:, lanes] * xs_p
            wx_s[:, gl] = (xdt * _pair_cols(wcol, i0)).astype(BF16)
            dec_s[:, gl] = _pair_cols(eend, i0)
        ht_s[g] = ht * dec_s[...] + jnp.dot(bg.T.astype(BF16), wx_s[...], preferred_element_type=F32)

    for g in range(SSM_GROUPS):
        cols = slice(g * SSM_GROUP_WIDTH, (g + 1) * SSM_GROUP_WIDTH)
        zz = z_ref[:, cols]
        yg = y_s[:, cols] * (zz * jax.nn.sigmoid(zz))
        o_ref[:, cols] = _rms(yg, nw_ref[:, cols]).astype(o_ref.dtype)


def _ssd(proj3, dt3, conv_w, conv_b, dt_bias, a_log, d_skip, norm_w):
    b, s, _ = proj3.shape
    L = SSD_CHUNK
    full = lambda shape: pl.BlockSpec(shape, lambda bi, c: (0,) * len(shape))
    return pl.pallas_call(
        _ssd_kernel,
        grid=(b, s // L),
        in_specs=[
            pl.BlockSpec((None, L, SSM_WIDTH), lambda bi, c: (bi, c, (3 * ATTN_WIDTH + SSM_WIDTH) // SSM_WIDTH)),
            pl.BlockSpec((None, L, XBC_WIDTH - SSM_WIDTH),
                         lambda bi, c: (bi, c, (3 * ATTN_WIDTH + 2 * SSM_WIDTH) // (XBC_WIDTH - SSM_WIDTH))),
            pl.BlockSpec((None, L, SSM_WIDTH), lambda bi, c: (bi, c, 3 * ATTN_WIDTH // SSM_WIDTH)),
            pl.BlockSpec((None, L, LANES), lambda bi, c: (bi, c, 0)),
            full((SSM_CONV, XBC_WIDTH)),
            full((1, XBC_WIDTH)),
            full((1, LANES)),
            full((1, LANES)),
            full((1, SSM_WIDTH)),
            full((1, SSM_WIDTH)),
        ],
        out_specs=pl.BlockSpec((None, L, SSM_WIDTH), lambda bi, c: (bi, c, 0)),
        out_shape=jax.ShapeDtypeStruct((b, s, SSM_WIDTH), BF16),
        scratch_shapes=[
            pltpu.VMEM((L + 2 * SUBLANES, XBC_WIDTH), F32),
            pltpu.VMEM((L, XBC_WIDTH), F32),
            pltpu.VMEM((SSM_GROUPS, SSM_STATE, SSM_GROUP_WIDTH), F32),
            pltpu.VMEM((L, SSM_WIDTH), F32),
            pltpu.VMEM((L, SSM_GROUP_WIDTH), BF16),
            pltpu.VMEM((1, SSM_GROUP_WIDTH), F32),
        ],
        compiler_params=pltpu.CompilerParams(
            dimension_semantics=("parallel", "arbitrary"), vmem_limit_bytes=VMEM_LIMIT),
        name="ssd",
    )(proj3, proj3, proj3, dt3, conv_w, conv_b, dt_bias, a_log, d_skip, norm_w)


def _out_proj_kernel(x_ref, a_ref, s_ref, wa_ref, ws_ref, o_ref):
    o_ref[...] = (x_ref[...]
                  + jnp.dot(a_ref[...], wa_ref[...], preferred_element_type=F32)
                  + jnp.dot(s_ref[...], ws_ref[...], preferred_element_type=F32))


def _out_proj(x2d, attn2d, ssm2d, w_attn, w_ssm):
    t, d = x2d.shape
    return pl.pallas_call(
        _out_proj_kernel,
        grid=(t // ROW_TILE,),
        in_specs=[
            pl.BlockSpec((ROW_TILE, d), lambda i: (i, 0)),
            pl.BlockSpec((ROW_TILE, ATTN_WIDTH), lambda i: (i, 0)),
            pl.BlockSpec((ROW_TILE, SSM_WIDTH), lambda i: (i, 0)),
            pl.BlockSpec((ATTN_WIDTH, d), lambda i: (0, 0)),
            pl.BlockSpec((SSM_WIDTH, d), lambda i: (0, 0)),
        ],
        out_specs=pl.BlockSpec((ROW_TILE, d), lambda i: (i, 0)),
        out_shape=jax.ShapeDtypeStruct((t, d), F32),
        compiler_params=pltpu.CompilerParams(
            dimension_semantics=("parallel",), vmem_limit_bytes=VMEM_LIMIT),
        name="out_proj",
    )(x2d, attn2d, ssm2d, w_attn, w_ssm)


def _conv_glu_kernel(x_ref, halo_ref, xres_ref, nw_ref, wg_ref, wv_ref, cwg_ref, cwv_ref, cbg_ref, cbv_ref,
                     wd_ref, o_ref, h_s, hp_s, ug0_s, uv0_s, ug1_s, uv1_s, act_s, op_s, *, tiles_per_seq, n_up):
    tm = x_ref.shape[0]
    tf = wg_ref.shape[1]
    span = tm // SUBLANES
    halo = FFN_HALO
    i = pl.program_id(0)
    f = pl.program_id(1)

    u_bufs = ((ug0_s, uv0_s), (ug1_s, uv1_s))

    n_parts = 2 * (tf // MXU_COLS)

    def up_dot_part(slot, part):
        u_s, w_ref = ((u_bufs[slot][0], wg_ref), (u_bufs[slot][1], wv_ref))[part // (tf // MXU_COLS)]
        cols = slice((part % (tf // MXU_COLS)) * MXU_COLS, (part % (tf // MXU_COLS) + 1) * MXU_COLS)
        u_s[:, cols] = jnp.dot(hp_s[...], w_ref[:, cols], preferred_element_type=F32)

    def up_fix(slot):
        for u_s in u_bufs[slot]:
            before = [jnp.concatenate([u_s[halo - d:halo - d + 1, :],
                                       u_s[halo + tm - d * SUBLANES:halo + tm - d * SUBLANES + SUBLANES - 1, :]],
                                      axis=0) for d in (1, 2)]
            u_s[halo - SUBLANES:halo, :] = before[0]
            u_s[halo - 2 * SUBLANES:halo - SUBLANES, :] = before[1]

    def conv(u_s, cw_ref, cb_ref, r0, cols):
        out = cb_ref[:, cols]
        for j in range(FFN_CONV):
            first = halo - (FFN_CONV - 1 - j) * SUBLANES + r0
            out = out + cw_ref[j:j + 1, cols] * u_s[first:first + CONV_ROWS, cols]
        return out

    def conv_act(slot, chunk, part=0, parts=1):
        ug_s, uv_s = u_bufs[slot]
        for c0 in range(part * (tf // parts), (part + 1) * (tf // parts), LANES):
            cols = slice(c0, c0 + LANES)
            for r0 in range(0, tm, CONV_ROWS):
                gate = conv(ug_s, cwg_ref, cbg_ref, r0, cols)
                val = conv(uv_s, cwv_ref, cbv_ref, r0, cols)
                act_s[chunk, r0:r0 + CONV_ROWS, cols] = (gate * jax.nn.sigmoid(gate) * val).astype(BF16)

    @pl.when(f == 0)
    def _():
        _norm_rows(x_ref, nw_ref, h_s, 0, tm)
        keep = jnp.where(i % tiles_per_seq == 0, 0.0, 1.0)
        hp_s[0:halo, :] = (_rms(halo_ref[...], nw_ref[...]) * keep).astype(BF16)
        pos = lax.broadcasted_iota(jnp.int32, (tm, tm), 0)
        row = lax.broadcasted_iota(jnp.int32, (tm, tm), 1)
        perm = jnp.where(row == (pos % SUBLANES) * span + pos // SUBLANES, 1.0, 0.0).astype(BF16)
        for c0 in range(0, h_s.shape[1], tf):
            hp_s[halo:, c0:c0 + tf] = jnp.dot(perm, h_s[:, c0:c0 + tf],
                                              preferred_element_type=F32).astype(BF16)
        for part in range(n_parts):
            up_dot_part(0, part)
        up_fix(0)

    for parity in (0, 1):
        @pl.when((f >= 1) & (f < n_up) & (f % 2 == parity))
        def _():
            for part in range(n_parts):
                up_dot_part(parity, part)
                conv_act(1 - parity, f - 1, part, n_parts)
            up_fix(parity)

    @pl.when(f == n_up)
    def _():
        conv_act((n_up - 1) % 2, n_up - 1)

    @pl.when(f >= n_up)
    def _():
        acc = jnp.dot(act_s[0], wd_ref[0:tf, :], preferred_element_type=F32)
        for c in range(1, n_up):
            acc = acc + jnp.dot(act_s[c], wd_ref[c * tf:(c + 1) * tf, :], preferred_element_type=F32)
        for c in range(op_s.shape[0]):
            op_s[c] = acc[:, c * LANES:(c + 1) * LANES]
        for a in range(SUBLANES):
            rows = slice(a * span, (a + 1) * span)
            for c in range(op_s.shape[0]):
                cols = slice(c * LANES, (c + 1) * LANES)
                o_ref[rows, cols] = xres_ref[rows, cols] + op_s[c, pl.ds(a, span, stride=SUBLANES), :]


def _conv_glu(x2d, nw, w_up, conv_w, conv_b, w_down, seq_len):
    t, d = x2d.shape
    tm, tf, tn = ROW_TILE, COL_TILE, COL_TILE
    n_up = D_FF // tf
    n_down = d // tn
    halo_blocks = tm // FFN_HALO
    up = lambda f: jnp.minimum(f, n_up - 1)
    cv = lambda f: jnp.clip(f - 1, 0, n_up - 1)
    down = lambda f: jnp.maximum(f - n_up, 0)
    return pl.pallas_call(
        functools.partial(_conv_glu_kernel, tiles_per_seq=seq_len // tm, n_up=n_up),
        grid=(t // tm, n_up + n_down),
        in_specs=[
            pl.BlockSpec((tm, d), lambda i, f: (i, 0)),
            pl.BlockSpec((FFN_HALO, d), lambda i, f: (jnp.maximum(i * halo_blocks - 1, 0), 0)),
            pl.BlockSpec((tm, tn), lambda i, f: (i, down(f))),
            pl.BlockSpec((1, d), lambda i, f: (0, 0)),
            pl.BlockSpec((None, d, tf), lambda i, f: (up(f), 0, 0)),
            pl.BlockSpec((None, d, tf), lambda i, f: (n_up + up(f), 0, 0)),
            pl.BlockSpec((FFN_CONV, tf), lambda i, f: (0, cv(f))),
            pl.BlockSpec((FFN_CONV, tf), lambda i, f: (0, n_up + cv(f))),
            pl.BlockSpec((1, tf), lambda i, f: (0, cv(f))),
            pl.BlockSpec((1, tf), lambda i, f: (0, n_up + cv(f))),
            pl.BlockSpec((None, D_FF, tn), lambda i, f: (down(f), 0, 0)),
        ],
        out_specs=pl.BlockSpec((tm, tn), lambda i, f: (i, down(f))),
        out_shape=jax.ShapeDtypeStruct((t, d), F32),
        scratch_shapes=[
            pltpu.VMEM((tm, d), BF16),
            pltpu.VMEM((FFN_HALO + tm, d), BF16),
            pltpu.VMEM((FFN_HALO + tm, tf), F32),
            pltpu.VMEM((FFN_HALO + tm, tf), F32),
            pltpu.VMEM((FFN_HALO + tm, tf), F32),
            pltpu.VMEM((FFN_HALO + tm, tf), F32),
            pltpu.VMEM((n_up, tm, tf), BF16),
            pltpu.VMEM((tn // LANES, tm, LANES), F32),
        ],
        compiler_params=pltpu.CompilerParams(
            dimension_semantics=("parallel", "arbitrary"), vmem_limit_bytes=VMEM_LIMIT),
        name="conv_glu",
    )(x2d, x2d, x2d, nw, w_up, w_up, conv_w, conv_w, conv_b, conv_b, w_down)


def _pad_lanes(v):
    return jnp.pad(v.reshape(1, -1), ((0, 0), (0, LANES - v.shape[-1])))


def _block_major(w, tile):
    k, n = w.shape
    return w.reshape(k, n // tile, tile).transpose(1, 0, 2)


def _layer(x2d, batch, seq, norm1_w, w_in, q_norm_w, k_norm_w, ssm_conv_w, ssm_conv_b, dt_bias, a_log,
           d_skip, ssm_norm_w, w_out, norm2_w, w_up, ffn_conv_w, ffn_conv_b, w_down):
    w_main = _block_major(w_in[:, :PROJ_MAIN].astype(BF16), COL_TILE)
    w_dt = jnp.pad(w_in[:, PROJ_MAIN:], ((0, 0), (0, LANES - SSM_HEADS))).astype(BF16)
    proj, dt_raw = _in_proj(x2d, norm1_w.reshape(1, -1), w_main, w_dt)
    proj3 = proj.reshape(batch, seq, PROJ_MAIN)

    attn = _moba(proj3, q_norm_w.reshape(1, -1), k_norm_w.reshape(1, -1))
    ssm = _ssd(proj3, dt_raw.reshape(batch, seq, LANES), ssm_conv_w, ssm_conv_b.reshape(1, -1),
               _pad_lanes(dt_bias), _pad_lanes(a_log),
               jnp.repeat(d_skip, SSM_HEAD_DIM).reshape(1, -1), ssm_norm_w.reshape(1, -1))

    w_out_b = w_out.astype(BF16)
    x1 = _out_proj(x2d, attn.reshape(-1, ATTN_WIDTH), ssm.reshape(-1, SSM_WIDTH),
                   w_out_b[:ATTN_WIDTH], w_out_b[ATTN_WIDTH:])
    return _conv_glu(x1, norm2_w.reshape(1, -1), _block_major(w_up.astype(BF16), COL_TILE), ffn_conv_w,
                     ffn_conv_b.reshape(1, -1), _block_major(w_down.astype(BF16), COL_TILE), seq)


def kernel(x, norm1_w, w_in, q_norm_w, k_norm_w, ssm_conv_w, ssm_conv_b, dt_bias, a_log, d_skip, ssm_norm_w,
           w_out, norm2_w, w_up, ffn_conv_w, ffn_conv_b, w_down):
    batch, seq, d = x.shape
    x2d = x.reshape(batch * seq, d)
    for i in range(norm1_w.shape[0]):
        x2d = _layer(x2d, batch, seq, norm1_w[i], w_in[i], q_norm_w[i], k_norm_w[i], ssm_conv_w[i],
                     ssm_conv_b[i], dt_bias[i], a_log[i], d_skip[i], ssm_norm_w[i], w_out[i], norm2_w[i],
                     w_up[i], ffn_conv_w[i], ffn_conv_b[i], w_down[i])
    return x2d.reshape(batch, seq, d)
```

```python
import functools

import jax
import jax.numpy as jnp
from jax import lax
from jax.experimental import pallas as pl
from jax.experimental.pallas import tpu as pltpu

F32 = jnp.float32
BF16 = jnp.bfloat16

D_MODEL = 2048
ATTN_WIDTH = 1024
ATTN_HEAD_DIM = 128
ATTN_HEADS = ATTN_WIDTH // ATTN_HEAD_DIM
MOBA_BLOCK = 256
MOBA_TOPK = 3
SSM_WIDTH = 1024
SSM_HEAD_DIM = 64
SSM_HEADS = SSM_WIDTH // SSM_HEAD_DIM
SSM_GROUPS = 2
SSM_HEADS_PER_GROUP = SSM_HEADS // SSM_GROUPS
SSM_GROUP_WIDTH = SSM_WIDTH // SSM_GROUPS
SSM_STATE = 128
SSM_CONV = 4
XBC_WIDTH = SSM_WIDTH + 2 * SSM_GROUPS * SSM_STATE
PROJ_MAIN = 3 * ATTN_WIDTH + SSM_WIDTH + XBC_WIDTH
D_FF = 5632
FFN_CONV = 3
EPS = 1e-6
LOG2_E = 1.4426950408889634

LANES = 128
MXU_COLS = 256
SUBLANES = 8
VMEM_LIMIT = 56 * 1024 * 1024

SSD_CHUNK = 256
ROW_TILE = 512
IN_ROW_TILE = 256
COL_TILE = 512
FFN_HALO = 16
NORM_ROWS = 64
CONV_ROWS = 64

_NT = (((1,), (1,)), ((), ()))


def _rms(x, w):
    return x * lax.rsqrt(jnp.mean(x * x, axis=-1, keepdims=True) + EPS) * w


def _norm_rows(x_ref, nw_ref, h_ref, dst_off, n_rows):
    def body(c, carry):
        r = pl.multiple_of(c * NORM_ROWS, NORM_ROWS)
        h_ref[pl.ds(dst_off + r, NORM_ROWS), :] = _rms(x_ref[pl.ds(r, NORM_ROWS), :], nw_ref[...]).astype(BF16)
        return carry
    lax.fori_loop(0, n_rows // NORM_ROWS, body, 0)


def _in_proj_kernel(x_ref, nw_ref, w_ref, wdt_ref, o_ref, dt_ref, h_ref):
    _norm_rows(x_ref, nw_ref, h_ref, 0, x_ref.shape[0])
    dt_ref[...] = jnp.dot(h_ref[...], wdt_ref[...], preferred_element_type=F32)
    for c0 in range(0, o_ref.shape[1], COL_TILE):
        o_ref[:, c0:c0 + COL_TILE] = jnp.dot(h_ref[...], w_ref[:, c0:c0 + COL_TILE], preferred_element_type=F32)


def _in_proj(x2d, nw, w_all, w_dt):
    t, d = x2d.shape
    tm = IN_ROW_TILE
    return pl.pallas_call(
        _in_proj_kernel,
        grid=(t // tm,),
        in_specs=[
            pl.BlockSpec((tm, d), lambda i: (i, 0)),
            pl.BlockSpec((1, d), lambda i: (0, 0)),
            pl.BlockSpec(w_all.shape, lambda i: (0, 0), pipeline_mode=pl.Buffered(1)),
            pl.BlockSpec((d, LANES), lambda i: (0, 0)),
        ],
        out_specs=[
            pl.BlockSpec((tm, PROJ_MAIN), lambda i: (i, 0)),
            pl.BlockSpec((tm, LANES), lambda i: (i, 0)),
        ],
        out_shape=[jax.ShapeDtypeStruct((t, PROJ_MAIN), F32), jax.ShapeDtypeStruct((t, LANES), F32)],
        scratch_shapes=[pltpu.VMEM((tm, d), BF16)],
        compiler_params=pltpu.CompilerParams(
            dimension_semantics=("parallel",), vmem_limit_bytes=VMEM_LIMIT),
        name="in_proj",
    )(x2d, nw, w_all, w_dt)


def _pair_schedule(nb, width):
    remaining = {i: list(range(i)) for i in range(1, nb)}
    qi, kj = [], []
    while any(remaining.values()):
        live = sorted((i for i in remaining if remaining[i]), key=lambda i: -len(remaining[i]))
        if len(live) < width:
            return None
        for i in live[:width]:
            qi.append(i)
            kj.append(remaining[i].pop())
    return qi, kj


def _moba_schedule(nb):
    for width in (4, 2, 1):
        sched = _pair_schedule(nb, width)
        if sched is not None and width <= nb - 1:
            qi, kj = sched
            if (len(qi) // width) % 2:
                qi = qi + list(range(1, width + 1))
                kj = kj + [nb - 1] * width
            return width, (qi, kj)
    raise ValueError(f"no MoBA pair schedule for {nb} blocks")


def _moba_kernel(qi_ref, kj_ref, q_ref, k_ref, v_ref, qw_ref, kw_ref, o_ref,
                 qf_s, qb_s, k_s, vt_s, km_s, sel_s, m_s, l_s, acc_s,
                 sc_a, sc_b, p_a, p_b, al_a, al_b, *, nb, width, n_groups):
    blk = MOBA_BLOCK
    scale = ATTN_HEAD_DIM ** -0.5 * LOG2_E
    for j in range(nb):
        rows = slice(j * blk, (j + 1) * blk)
        kn = _rms(k_ref[rows, :], kw_ref[...])
        k_s[j] = kn.astype(BF16)
        km_s[j:j + 1, :] = jnp.mean(kn, axis=0, keepdims=True)
        vt_s[j] = v_ref[rows, :].T.astype(BF16)
        qn = _rms(q_ref[rows, :], qw_ref[...]) * scale
        qf_s[j] = qn
        qb_s[j] = qn.astype(BF16)

    key_i = lax.broadcasted_iota(jnp.int32, (blk, blk), 0)
    qry_i = lax.broadcasted_iota(jnp.int32, (blk, blk), 1)
    bid = lax.broadcasted_iota(jnp.int32, (nb, blk), 0)

    def init_group(t, carry):
        ids = [t * width + u for u in range(width)]
        gates = [lax.dot_general(km_s[...], qf_s[i], _NT, precision=lax.Precision.HIGHEST,
                                 preferred_element_type=F32) for i in ids]
        owns = [lax.dot_general(k_s[i], qb_s[i], _NT, preferred_element_type=F32) for i in ids]
        for i, gate in zip(ids, gates):
            past = bid < i
            g = jnp.where(past, gate, -jnp.inf)
            rank = jnp.zeros((nb, blk), jnp.int32)
            for jp in range(nb):
                row = g[jp:jp + 1, :]
                rank = rank + jnp.where(row > g, 1, jnp.where(row == g, jnp.where(bid > jp, 1, 0), 0))
            sel = jnp.where(past, jnp.where(rank < MOBA_TOPK, 1.0, 0.0), 0.0)
            for jp in range(nb):
                sel_s[i * nb + jp] = sel[jp:jp + 1, :]
        for i, s in zip(ids, owns):
            s = jnp.where(key_i <= qry_i, s, -jnp.inf)
            m = jnp.max(s, axis=0, keepdims=True)
            p = jnp.exp2(s - m)
            m_s[i] = m
            l_s[i] = jnp.sum(p, axis=0, keepdims=True)
            acc_s[i] = jnp.dot(vt_s[i], p.astype(BF16), preferred_element_type=F32)
        return carry

    lax.fori_loop(0, nb // width, init_group, 0)

    def group_pairs(g):
        return [(qi_ref[g * width + u], kj_ref[g * width + u]) for u in range(width)]

    def score_group(g, sc):
        for u, (i, j) in enumerate(group_pairs(g)):
            sc[u] = lax.dot_general(k_s[j], qb_s[i], _NT, preferred_element_type=F32)

    def softmax_group(g, sc, p_buf, al_buf):
        pairs = group_pairs(g)
        m_old = [m_s[i] for i, _ in pairs]
        upd = []
        for u, ((i, j), m0) in enumerate(zip(pairs, m_old)):
            s = jnp.where(sel_s[i * nb + j] > 0.0, sc[u], -jnp.inf)
            m1 = jnp.maximum(m0, jnp.max(s, axis=0, keepdims=True))
            p = jnp.exp2(s - m1)
            p_buf[u] = p.astype(BF16)
            alpha = jnp.exp2(m0 - m1)
            al_buf[u] = alpha
            upd.append((i, m1, alpha, jnp.sum(p, axis=0, keepdims=True)))
        for i, m1, alpha, psum in upd:
            m_s[i] = m1
            l_s[i] = alpha * l_s[i] + psum

    def pv_group(g, p_buf, al_buf):
        pairs = group_pairs(g)
        pvs = [jnp.dot(vt_s[j], p_buf[u], preferred_element_type=F32) for u, (_, j) in enumerate(pairs)]
        for u, ((i, _), pv) in enumerate(zip(pairs, pvs)):
            acc_s[i] = al_buf[u] * acc_s[i] + pv

    def step(g, sc_cur, sc_next, p_cur, al_cur, p_prev, al_prev):
        pv_group(jnp.maximum(g - 1, 0), p_prev, al_prev)
        score_group(jnp.minimum(g + 1, n_groups - 1), sc_next)
        softmax_group(g, sc_cur, p_cur, al_cur)

    p_b[...] = jnp.zeros(p_b.shape, BF16)
    al_b[...] = jnp.ones(al_b.shape, F32)
    score_group(0, sc_a)

    def two_steps(t, carry):
        step(2 * t, sc_a, sc_b, p_a, al_a, p_b, al_b)
        step(2 * t + 1, sc_b, sc_a, p_b, al_b, p_a, al_a)
        return carry

    lax.fori_loop(0, n_groups // 2, two_steps, 0)
    pv_group(n_groups - 1, p_b, al_b)

    def finish_two(t, carry):
        for i in (2 * t, 2 * t + 1):
            o_ref[pl.ds(pl.multiple_of(i * blk, blk), blk), :] = (acc_s[i] / l_s[i]).T.astype(o_ref.dtype)
        return carry

    lax.fori_loop(0, nb // 2, finish_two, 0)


def _moba(proj3, qw, kw):
    b, s, _ = proj3.shape
    nb = s // MOBA_BLOCK
    assert nb % 2 == 0
    dh = ATTN_HEAD_DIM
    blk = MOBA_BLOCK
    width, (qi, kj) = _moba_schedule(nb)
    assert nb % width == 0 and (len(qi) // width) % 2 == 0
    smem = pl.BlockSpec(memory_space=pltpu.SMEM)
    return pl.pallas_call(
        functools.partial(_moba_kernel, nb=nb, width=width, n_groups=len(qi) // width),
        grid=(b, ATTN_HEADS),
        in_specs=[
            smem,
            smem,
            pl.BlockSpec((None, s, dh), lambda bi, h: (bi, 0, h)),
            pl.BlockSpec((None, s, dh), lambda bi, h: (bi, 0, ATTN_HEADS + h)),
            pl.BlockSpec((None, s, dh), lambda bi, h: (bi, 0, 2 * ATTN_HEADS + h)),
            pl.BlockSpec((1, dh), lambda bi, h: (0, 0)),
            pl.BlockSpec((1, dh), lambda bi, h: (0, 0)),
        ],
        out_specs=pl.BlockSpec((None, s, dh), lambda bi, h: (bi, 0, h)),
        out_shape=jax.ShapeDtypeStruct((b, s, ATTN_WIDTH), BF16),
        scratch_shapes=[
            pltpu.VMEM((nb, blk, dh), F32),
            pltpu.VMEM((nb, blk, dh), BF16),
            pltpu.VMEM((nb, blk, dh), BF16),
            pltpu.VMEM((nb, dh, blk), BF16),
            pltpu.VMEM((nb, dh), F32),
            pltpu.VMEM((nb * nb, 1, blk), F32),
            pltpu.VMEM((nb, 1, blk), F32),
            pltpu.VMEM((nb, 1, blk), F32),
            pltpu.VMEM((nb, dh, blk), F32),
            pltpu.VMEM((width, blk, blk), F32),
            pltpu.VMEM((width, blk, blk), F32),
            pltpu.VMEM((width, blk, blk), BF16),
            pltpu.VMEM((width, blk, blk), BF16),
            pltpu.VMEM((width, 1, blk), F32),
            pltpu.VMEM((width, 1, blk), F32),
        ],
        compiler_params=pltpu.CompilerParams(
            dimension_semantics=("parallel", "parallel"), vmem_limit_bytes=VMEM_LIMIT),
        name="moba",
    )(jnp.asarray(qi, jnp.int32), jnp.asarray(kj, jnp.int32), proj3, proj3, proj3, qw, kw)


def _pair_cols(arr, i0):
    rows = arr.shape[0]
    lo = lax.broadcasted_iota(jnp.int32, (rows, LANES), 1) < SSM_HEAD_DIM
    a0 = jnp.broadcast_to(arr[:, i0:i0 + 1], (rows, LANES))
    a1 = jnp.broadcast_to(arr[:, i0 + 1:i0 + 2], (rows, LANES))
    return jnp.where(lo, a0, a1)


def _ssd_kernel(xs_ref, bc_ref, z_ref, dt_ref, cw_ref, cb_ref, dtb_ref, alog_ref, dsk_ref, nw_ref, o_ref,
                ext_s, xbc_s, ht_s, y_s, wx_s, dec_s):
    L = xs_ref.shape[0]
    pad = SUBLANES

    @pl.when(pl.program_id(1) == 0)
    def _():
        ext_s[0:pad, :] = jnp.zeros((pad, XBC_WIDTH), F32)
        ht_s[...] = jnp.zeros(ht_s.shape, F32)

    ext_s[pad:pad + L, 0:SSM_WIDTH] = xs_ref[...]
    ext_s[pad:pad + L, SSM_WIDTH:XBC_WIDTH] = bc_ref[...]
    for cblk in range(XBC_WIDTH // LANES):
        cols = slice(cblk * LANES, (cblk + 1) * LANES)
        conv = cb_ref[:, cols] + cw_ref[0:1, cols] * ext_s[pad - 3:pad - 3 + L, cols]
        for j in range(1, SSM_CONV):
            conv = conv + cw_ref[j:j + 1, cols] * ext_s[pad - 3 + j:pad - 3 + j + L, cols]
        xbc_s[:, cols] = conv * jax.nn.sigmoid(conv)
    ext_s[0:pad, :] = ext_s[L:L + pad, :]

    dtv = dt_ref[...] + dtb_ref[...]
    dt = jnp.maximum(dtv, 0.0) + jnp.log1p(jnp.exp(-jnp.abs(dtv)))
    la = dt * (-jnp.exp(alog_ref[...]))
    row_i = lax.broadcasted_iota(jnp.int32, (L, L), 0)
    col_i = lax.broadcasted_iota(jnp.int32, (L, L), 1)
    tril = row_i >= col_i
    acol = jnp.dot(jnp.where(tril, 1.0, 0.0), la, precision=lax.Precision.HIGHEST,
                   preferred_element_type=F32)
    arow = acol.T
    ecol = jnp.exp(acol)
    aend = acol[L - 1:L, :]
    wcol = jnp.exp(aend - acol)
    eend = jnp.exp(aend)
    lo = lax.broadcasted_iota(jnp.int32, (L, LANES), 1) < SSM_HEAD_DIM

    for g in range(SSM_GROUPS):
        b_off = SSM_WIDTH + g * SSM_STATE
        c_off = SSM_WIDTH + SSM_GROUPS * SSM_STATE + g * SSM_STATE
        bg = xbc_s[:, b_off:b_off + SSM_STATE]
        cg = xbc_s[:, c_off:c_off + SSM_STATE].astype(BF16)
        cb = lax.dot_general(cg, bg.astype(BF16), _NT, preferred_element_type=F32)
        cbm = jnp.where(tril, cb, 0.0)
        ht = ht_s[g]
        ch = jnp.dot(cg, ht.astype(BF16), preferred_element_type=F32)
        for pr in range(SSM_HEADS_PER_GROUP // 2):
            i0 = g * SSM_HEADS_PER_GROUP + 2 * pr
            lanes = slice(i0 * SSM_HEAD_DIM, i0 * SSM_HEAD_DIM + LANES)
            gl = slice(pr * LANES, (pr + 1) * LANES)
            xs_p = xbc_s[:, lanes]
            xdt = xs_p * _pair_cols(dt, i0)
            yd = None
            for hh in range(2):
                idx = i0 + hh
                seg = acol[:, idx:idx + 1] - arow[idx:idx + 1, :]
                mm = (cbm * jnp.exp(jnp.minimum(seg, 0.0))).astype(BF16)
                xh = jnp.where(lo, xdt, 0.0) if hh == 0 else jnp.where(lo, 0.0, xdt)
                part = jnp.dot(mm, xh.astype(BF16), preferred_element_type=F32)
                yd = part if yd is None else yd + part
            y_off = ch[:, gl] * _pair_cols(ecol, i0)
            y_s[:, lanes] = yd + y_off + dsk_ref[:, lanes] * xs_p
            wx_s[:, gl] = (xdt * _pair_cols(wcol, i0)).astype(BF16)
            dec_s[:, gl] = _pair_cols(eend, i0)
        ht_s[g] = ht * dec_s[...] + jnp.dot(bg.T.astype(BF16), wx_s[...], preferred_element_type=F32)

    for g in range(SSM_GROUPS):
        cols = slice(g * SSM_GROUP_WIDTH, (g + 1) * SSM_GROUP_WIDTH)
        zz = z_ref[:, cols]
        yg = y_s[:, cols] * (zz * jax.nn.sigmoid(zz))
        o_ref[:, cols] = _rms(yg, nw_ref[:, cols]).astype(o_ref.dtype)


def _ssd(proj3, dt3, conv_w, conv_b, dt_bias, a_log, d_skip, norm_w):
    b, s, _ = proj3.shape
    L = SSD_CHUNK
    full = lambda shape: pl.BlockSpec(shape, lambda bi, c: (0,) * len(shape))
    return pl.pallas_call(
        _ssd_kernel,
        grid=(b, s // L),
        in_specs=[
            pl.BlockSpec((None, L, SSM_WIDTH), lambda bi, c: (bi, c, (3 * ATTN_WIDTH + SSM_WIDTH) // SSM_WIDTH)),
            pl.BlockSpec((None, L, XBC_WIDTH - SSM_WIDTH),
                         lambda bi, c: (bi, c, (3 * ATTN_WIDTH + 2 * SSM_WIDTH) // (XBC_WIDTH - SSM_WIDTH))),
            pl.BlockSpec((None, L, SSM_WIDTH), lambda bi, c: (bi, c, 3 * ATTN_WIDTH // SSM_WIDTH)),
            pl.BlockSpec((None, L, LANES), lambda bi, c: (bi, c, 0)),
            full((SSM_CONV, XBC_WIDTH)),
            full((1, XBC_WIDTH)),
            full((1, LANES)),
            full((1, LANES)),
            full((1, SSM_WIDTH)),
            full((1, SSM_WIDTH)),
        ],
        out_specs=pl.BlockSpec((None, L, SSM_WIDTH), lambda bi, c: (bi, c, 0)),
        out_shape=jax.ShapeDtypeStruct((b, s, SSM_WIDTH), BF16),
        scratch_shapes=[
            pltpu.VMEM((L + 2 * SUBLANES, XBC_WIDTH), F32),
            pltpu.VMEM((L, XBC_WIDTH), F32),
            pltpu.VMEM((SSM_GROUPS, SSM_STATE, SSM_GROUP_WIDTH), F32),
            pltpu.VMEM((L, SSM_WIDTH), F32),
            pltpu.VMEM((L, SSM_GROUP_WIDTH), BF16),
            pltpu.VMEM((1, SSM_GROUP_WIDTH), F32),
        ],
        compiler_params=pltpu.CompilerParams(
            dimension_semantics=("parallel", "arbitrary"), vmem_limit_bytes=VMEM_LIMIT),
        name="ssd",
    )(proj3, proj3, proj3, dt3, conv_w, conv_b, dt_bias, a_log, d_skip, norm_w)


def _out_proj_kernel(x_ref, a_ref, s_ref, wa_ref, ws_ref, o_ref):
    o_ref[...] = (x_ref[...]
                  + jnp.dot(a_ref[...], wa_ref[...], preferred_element_type=F32)
                  + jnp.dot(s_ref[...], ws_ref[...], preferred_element_type=F32))


def _out_proj(x2d, attn2d, ssm2d, w_out):
    t, d = x2d.shape
    return pl.pallas_call(
        _out_proj_kernel,
        grid=(t // ROW_TILE,),
        in_specs=[
            pl.BlockSpec((ROW_TILE, d), lambda i: (i, 0)),
            pl.BlockSpec((ROW_TILE, ATTN_WIDTH), lambda i: (i, 0)),
            pl.BlockSpec((ROW_TILE, SSM_WIDTH), lambda i: (i, 0)),
            pl.BlockSpec((ATTN_WIDTH, d), lambda i: (0, 0)),
            pl.BlockSpec((SSM_WIDTH, d), lambda i: (ATTN_WIDTH // SSM_WIDTH, 0)),
        ],
        out_specs=pl.BlockSpec((ROW_TILE, d), lambda i: (i, 0)),
        out_shape=jax.ShapeDtypeStruct((t, d), F32),
        compiler_params=pltpu.CompilerParams(
            dimension_semantics=("parallel",), vmem_limit_bytes=VMEM_LIMIT),
        name="out_proj",
    )(x2d, attn2d, ssm2d, w_out, w_out)


def _conv_glu_kernel(x_ref, halo_ref, xres_ref, nw_ref, wg_ref, wv_ref, cwg_ref, cwv_ref, cbg_ref, cbv_ref,
                     wd_ref, o_ref, h_s, hp_s, ug0_s, uv0_s, ug1_s, uv1_s, act_s, op_s, *, tiles_per_seq, n_up):
    tm = x_ref.shape[0]
    tf = wg_ref.shape[1]
    span = tm // SUBLANES
    halo = FFN_HALO
    i = pl.program_id(0)
    f = pl.program_id(1)

    u_bufs = ((ug0_s, uv0_s), (ug1_s, uv1_s))

    n_parts = 2 * (tf // MXU_COLS)

    def up_dot_part(slot, part):
        u_s, w_ref = ((u_bufs[slot][0], wg_ref), (u_bufs[slot][1], wv_ref))[part // (tf // MXU_COLS)]
        cols = slice((part % (tf // MXU_COLS)) * MXU_COLS, (part % (tf // MXU_COLS) + 1) * MXU_COLS)
        u_s[:, cols] = jnp.dot(hp_s[...], w_ref[:, cols], preferred_element_type=F32)

    def up_fix(slot):
        for u_s in u_bufs[slot]:
            before = [jnp.concatenate([u_s[halo - d:halo - d + 1, :],
                                       u_s[halo + tm - d * SUBLANES:halo + tm - d * SUBLANES + SUBLANES - 1, :]],
                                      axis=0) for d in (1, 2)]
            u_s[halo - SUBLANES:halo, :] = before[0]
            u_s[halo - 2 * SUBLANES:halo - SUBLANES, :] = before[1]

    def conv(u_s, cw_ref, cb_ref, r0, cols):
        out = cb_ref[:, cols]
        for j in range(FFN_CONV):
            first = halo - (FFN_CONV - 1 - j) * SUBLANES + r0
            out = out + cw_ref[j:j + 1, cols] * u_s[first:first + CONV_ROWS, cols]
        return out

    def conv_act(slot, chunk, part=0, parts=1):
        ug_s, uv_s = u_bufs[slot]
        for c0 in range(part * (tf // parts), (part + 1) * (tf // parts), LANES):
            cols = slice(c0, c0 + LANES)
            for r0 in range(0, tm, CONV_ROWS):
                gate = conv(ug_s, cwg_ref, cbg_ref, r0, cols)
                val = conv(uv_s, cwv_ref, cbv_ref, r0, cols)
                act_s[chunk, r0:r0 + CONV_ROWS, cols] = (gate * jax.nn.sigmoid(gate) * val).astype(BF16)

    @pl.when(f == 0)
    def _():
        _norm_rows(x_ref, nw_ref, h_s, 0, tm)
        keep = jnp.where(i % tiles_per_seq == 0, 0.0, 1.0)
        hp_s[0:halo, :] = (_rms(halo_ref[...], nw_ref[...]) * keep).astype(BF16)
        pos = lax.broadcasted_iota(jnp.int32, (tm, tm), 0)
        row = lax.broadcasted_iota(jnp.int32, (tm, tm), 1)
        perm = jnp.where(row == (pos % SUBLANES) * span + pos // SUBLANES, 1.0, 0.0).astype(BF16)
        for c0 in range(0, h_s.shape[1], tf):
            hp_s[halo:, c0:c0 + tf] = jnp.dot(perm, h_s[:, c0:c0 + tf],
                                              preferred_element_type=F32).astype(BF16)
        for part in range(n_parts):
            up_dot_part(0, part)
        up_fix(0)

    for parity in (0, 1):
        @pl.when((f >= 1) & (f < n_up) & (f % 2 == parity))
        def _():
            for part in range(n_parts):
                up_dot_part(parity, part)
                conv_act(1 - parity, f - 1, part, n_parts)
            up_fix(parity)

    @pl.when(f == n_up)
    def _():
        conv_act((n_up - 1) % 2, n_up - 1)

    @pl.when(f >= n_up)
    def _():
        acc = jnp.dot(act_s[0], wd_ref[0:tf, :], preferred_element_type=F32)
        for c in range(1, n_up):
            acc = acc + jnp.dot(act_s[c], wd_ref[c * tf:(c + 1) * tf, :], preferred_element_type=F32)
        for c in range(op_s.shape[0]):
            op_s[c] = acc[:, c * LANES:(c + 1) * LANES]
        for a in range(SUBLANES):
            rows = slice(a * span, (a + 1) * span)
            for c in range(op_s.shape[0]):
                cols = slice(c * LANES, (c + 1) * LANES)
                o_ref[rows, cols] = xres_ref[rows, cols] + op_s[c, pl.ds(a, span, stride=SUBLANES), :]


def _conv_glu(x2d, nw, w_up, conv_w, conv_b, w_down, seq_len):
    t, d = x2d.shape
    tm, tf, tn = ROW_TILE, COL_TILE, COL_TILE
    n_up = D_FF // tf
    n_down = d // tn
    halo_blocks = tm // FFN_HALO
    up = lambda f: jnp.minimum(f, n_up - 1)
    cv = lambda f: jnp.clip(f - 1, 0, n_up - 1)
    down = lambda f: jnp.maximum(f - n_up, 0)
    return pl.pallas_call(
        functools.partial(_conv_glu_kernel, tiles_per_seq=seq_len // tm, n_up=n_up),
        grid=(t // tm, n_up + n_down),
        in_specs=[
            pl.BlockSpec((tm, d), lambda i, f: (i, 0)),
            pl.BlockSpec((FFN_HALO, d), lambda i, f: (jnp.maximum(i * halo_blocks - 1, 0), 0)),
            pl.BlockSpec((tm, tn), lambda i, f: (i, down(f))),
            pl.BlockSpec((1, d), lambda i, f: (0, 0)),
            pl.BlockSpec((d, tf), lambda i, f: (0, up(f))),
            pl.BlockSpec((d, tf), lambda i, f: (0, n_up + up(f))),
            pl.BlockSpec((FFN_CONV, tf), lambda i, f: (0, cv(f))),
            pl.BlockSpec((FFN_CONV, tf), lambda i, f: (0, n_up + cv(f))),
            pl.BlockSpec((1, tf), lambda i, f: (0, cv(f))),
            pl.BlockSpec((1, tf), lambda i, f: (0, n_up + cv(f))),
            pl.BlockSpec((D_FF, tn), lambda i, f: (0, down(f))),
        ],
        out_specs=pl.BlockSpec((tm, tn), lambda i, f: (i, down(f))),
        out_shape=jax.ShapeDtypeStruct((t, d), F32),
        scratch_shapes=[
            pltpu.VMEM((tm, d), BF16),
            pltpu.VMEM((FFN_HALO + tm, d), BF16),
            pltpu.VMEM((FFN_HALO + tm, tf), F32),
            pltpu.VMEM((FFN_HALO + tm, tf), F32),
            pltpu.VMEM((FFN_HALO + tm, tf), F32),
            pltpu.VMEM((FFN_HALO + tm, tf), F32),
            pltpu.VMEM((n_up, tm, tf), BF16),
            pltpu.VMEM((tn // LANES, tm, LANES), F32),
        ],
        compiler_params=pltpu.CompilerParams(
            dimension_semantics=("parallel", "arbitrary"), vmem_limit_bytes=VMEM_LIMIT),
        name="conv_glu",
    )(x2d, x2d, x2d, nw, w_up, w_up, conv_w, conv_w, conv_b, conv_b, w_down)


def _pad_lanes(v):
    return jnp.pad(v.reshape(1, -1), ((0, 0), (0, LANES - v.shape[-1])))


def _layer(x2d, batch, seq, norm1_w, w_in, q_norm_w, k_norm_w, ssm_conv_w, ssm_conv_b, dt_bias, a_log,
           d_skip, ssm_norm_w, w_out, norm2_w, w_up, ffn_conv_w, ffn_conv_b, w_down):
    w_dt = jnp.pad(w_in[:, PROJ_MAIN:], ((0, 0), (0, LANES - SSM_HEADS))).astype(BF16)
    proj, dt_raw = _in_proj(x2d, norm1_w.reshape(1, -1), w_in.astype(BF16), w_dt)
    proj3 = proj.reshape(batch, seq, PROJ_MAIN)

    attn = _moba(proj3, q_norm_w.reshape(1, -1), k_norm_w.reshape(1, -1))
    ssm = _ssd(proj3, dt_raw.reshape(batch, seq, LANES), ssm_conv_w, ssm_conv_b.reshape(1, -1),
               _pad_lanes(dt_bias), _pad_lanes(a_log),
               jnp.repeat(d_skip, SSM_HEAD_DIM).reshape(1, -1), ssm_norm_w.reshape(1, -1))

    x1 = _out_proj(x2d, attn.reshape(-1, ATTN_WIDTH), ssm.reshape(-1, SSM_WIDTH), w_out.astype(BF16))
    return _conv_glu(x1, norm2_w.reshape(1, -1), w_up.astype(BF16), ffn_conv_w, ffn_conv_b.reshape(1, -1),
                     w_down.astype(BF16), seq)


def kernel(x, norm1_w, w_in, q_norm_w, k_norm_w, ssm_conv_w, ssm_conv_b, dt_bias, a_log, d_skip, ssm_norm_w,
           w_out, norm2_w, w_up, ffn_conv_w, ffn_conv_b, w_down):
    batch, seq, d = x.shape
    x2d = x.reshape(batch * seq, d)
    for i in range(norm1_w.shape[0]):
        x2d = _layer(x2d, batch, seq, norm1_w[i], w_in[i], q_norm_w[i], k_norm_w[i], ssm_conv_w[i],
                     ssm_conv_b[i], dt_bias[i], a_log[i], d_skip[i], ssm_norm_w[i], w_out[i], norm2_w[i],
                     w_up[i], ffn_conv_w[i], ffn_conv_b[i], w_down[i])
    return x2d.reshape(batch, seq, d)
```

```python
import functools

import jax
import jax.numpy as jnp
from jax import lax
from jax.experimental import pallas as pl
from jax.experimental.pallas import tpu as pltpu

F32 = jnp.float32
BF16 = jnp.bfloat16

D_MODEL = 2048
ATTN_WIDTH = 1024
ATTN_HEAD_DIM = 128
ATTN_HEADS = ATTN_WIDTH // ATTN_HEAD_DIM
MOBA_BLOCK = 256
MOBA_TOPK = 3
SSM_WIDTH = 1024
SSM_HEAD_DIM = 64
SSM_HEADS = SSM_WIDTH // SSM_HEAD_DIM
SSM_GROUPS = 2
SSM_HEADS_PER_GROUP = SSM_HEADS // SSM_GROUPS
SSM_GROUP_WIDTH = SSM_WIDTH // SSM_GROUPS
SSM_STATE = 128
SSM_CONV = 4
XBC_WIDTH = SSM_WIDTH + 2 * SSM_GROUPS * SSM_STATE
PROJ_MAIN = 3 * ATTN_WIDTH + SSM_WIDTH + XBC_WIDTH
D_FF = 5632
FFN_CONV = 3
EPS = 1e-6
LOG2_E = 1.4426950408889634

LANES = 128
MXU_COLS = 256
SUBLANES = 8
VMEM_LIMIT = 56 * 1024 * 1024

SSD_CHUNK = 256
ROW_TILE = 512
IN_ROW_TILE = 256
COL_TILE = 512
FFN_HALO = 16
NORM_ROWS = 64
CONV_ROWS = 64

_NT = (((1,), (1,)), ((), ()))


def _rms(x, w):
    return x * lax.rsqrt(jnp.mean(x * x, axis=-1, keepdims=True) + EPS) * w


def _norm_rows(x_ref, nw_ref, h_ref, dst_off, n_rows):
    def body(c, carry):
        r = pl.multiple_of(c * NORM_ROWS, NORM_ROWS)
        h_ref[pl.ds(dst_off + r, NORM_ROWS), :] = _rms(x_ref[pl.ds(r, NORM_ROWS), :], nw_ref[...]).astype(BF16)
        return carry
    lax.fori_loop(0, n_rows // NORM_ROWS, body, 0)


def _in_proj_kernel(x_ref, nw_ref, w_ref, wdt_ref, qw_ref, kw_ref, qkv_ref, rest_ref, km_ref, dt_ref, h_ref):
    _norm_rows(x_ref, nw_ref, h_ref, 0, x_ref.shape[0])
    dt_ref[...] = jnp.dot(h_ref[...], wdt_ref[...], preferred_element_type=F32)
    q_scale = ATTN_HEAD_DIM ** -0.5 * LOG2_E
    for c0 in range(0, PROJ_MAIN, COL_TILE):
        acc = jnp.dot(h_ref[...], w_ref[:, c0:c0 + COL_TILE], preferred_element_type=F32)
        if c0 >= 3 * ATTN_WIDTH:
            rest_ref[:, c0 - 3 * ATTN_WIDTH:c0 - 3 * ATTN_WIDTH + COL_TILE] = acc
        elif c0 >= 2 * ATTN_WIDTH:
            qkv_ref[:, c0:c0 + COL_TILE] = acc.astype(BF16)
        else:
            for h0 in range(0, COL_TILE, ATTN_HEAD_DIM):
                head = acc[:, h0:h0 + ATTN_HEAD_DIM]
                cols = slice(c0 + h0, c0 + h0 + ATTN_HEAD_DIM)
                if c0 < ATTN_WIDTH:
                    qkv_ref[:, cols] = (_rms(head, qw_ref[...]) * q_scale).astype(BF16)
                else:
                    kn = _rms(head, kw_ref[...])
                    qkv_ref[:, cols] = kn.astype(BF16)
                    kcols = slice(c0 + h0 - ATTN_WIDTH, c0 + h0 - ATTN_WIDTH + ATTN_HEAD_DIM)
                    km_ref[:, kcols] = jnp.mean(kn, axis=0, keepdims=True)


def _in_proj(x2d, nw, w_all, w_dt, qw, kw):
    t, d = x2d.shape
    tm = MOBA_BLOCK
    rest = PROJ_MAIN - 3 * ATTN_WIDTH
    return pl.pallas_call(
        _in_proj_kernel,
        grid=(t // tm,),
        in_specs=[
            pl.BlockSpec((tm, d), lambda i: (i, 0)),
            pl.BlockSpec((1, d), lambda i: (0, 0)),
            pl.BlockSpec(w_all.shape, lambda i: (0, 0), pipeline_mode=pl.Buffered(1)),
            pl.BlockSpec((d, LANES), lambda i: (0, 0)),
            pl.BlockSpec((1, ATTN_HEAD_DIM), lambda i: (0, 0)),
            pl.BlockSpec((1, ATTN_HEAD_DIM), lambda i: (0, 0)),
        ],
        out_specs=[
            pl.BlockSpec((tm, 3 * ATTN_WIDTH), lambda i: (i, 0)),
            pl.BlockSpec((tm, rest), lambda i: (i, 0)),
            pl.BlockSpec((None, 1, ATTN_WIDTH), lambda i: (i, 0, 0)),
            pl.BlockSpec((tm, LANES), lambda i: (i, 0)),
        ],
        out_shape=[jax.ShapeDtypeStruct((t, 3 * ATTN_WIDTH), BF16), jax.ShapeDtypeStruct((t, rest), F32),
                   jax.ShapeDtypeStruct((t // tm, 1, ATTN_WIDTH), F32), jax.ShapeDtypeStruct((t, LANES), F32)],
        scratch_shapes=[pltpu.VMEM((tm, d), BF16)],
        compiler_params=pltpu.CompilerParams(
            dimension_semantics=("parallel",), vmem_limit_bytes=VMEM_LIMIT),
        name="in_proj",
    )(x2d, nw, w_all, w_dt, qw, kw)


def _pair_schedule(nb, width):
    remaining = {i: list(range(i)) for i in range(1, nb)}
    qi, kj = [], []
    while any(remaining.values()):
        live = sorted((i for i in remaining if remaining[i]), key=lambda i: -len(remaining[i]))
        if len(live) < width:
            return None
        for i in live[:width]:
            qi.append(i)
            kj.append(remaining[i].pop())
    return qi, kj


def _moba_schedule(nb):
    for width in (4, 2, 1):
        sched = _pair_schedule(nb, width)
        if sched is not None and width <= nb - 1:
            qi, kj = sched
            if (len(qi) // width) % 2:
                qi = qi + list(range(1, width + 1))
                kj = kj + [nb - 1] * width
            return width, (qi, kj)
    raise ValueError(f"no MoBA pair schedule for {nb} blocks")


def _moba_kernel(qi_ref, kj_ref, q_ref, k_ref, v_ref, km_ref, o_ref,
                 vt_s, sel_s, m_s, l_s, acc_s,
                 sc_a, sc_b, p_a, p_b, al_a, al_b, *, nb, width, n_groups):
    blk = MOBA_BLOCK
    for j in range(nb):
        vt_s[j] = v_ref[j * blk:(j + 1) * blk, :].astype(F32).T.astype(BF16)

    def block(ref, i):
        return ref[pl.ds(pl.multiple_of(i * blk, blk), blk), :]

    key_i = lax.broadcasted_iota(jnp.int32, (blk, blk), 0)
    qry_i = lax.broadcasted_iota(jnp.int32, (blk, blk), 1)
    bid = lax.broadcasted_iota(jnp.int32, (nb, blk), 0)

    def init_group(t, carry):
        ids = [t * width + u for u in range(width)]
        gates = [lax.dot_general(km_ref[...], block(q_ref, i).astype(F32), _NT, precision=lax.Precision.HIGHEST,
                                 preferred_element_type=F32) for i in ids]
        owns = [lax.dot_general(block(k_ref, i), block(q_ref, i), _NT, preferred_element_type=F32) for i in ids]
        for i, gate in zip(ids, gates):
            past = bid < i
            g = jnp.where(past, gate, -jnp.inf)
            rank = jnp.zeros((nb, blk), jnp.int32)
            for jp in range(nb):
                row = g[jp:jp + 1, :]
                rank = rank + jnp.where(row > g, 1, jnp.where(row == g, jnp.where(bid > jp, 1, 0), 0))
            sel = jnp.where(past, jnp.where(rank < MOBA_TOPK, 1.0, 0.0), 0.0)
            for jp in range(nb):
                sel_s[i * nb + jp] = sel[jp:jp + 1, :]
        for i, s in zip(ids, owns):
            s = jnp.where(key_i <= qry_i, s, -jnp.inf)
            m = jnp.max(s, axis=0, keepdims=True)
            p = jnp.exp2(s - m)
            m_s[i] = m
            l_s[i] = jnp.sum(p, axis=0, keepdims=True)
            acc_s[i] = jnp.dot(vt_s[i], p.astype(BF16), preferred_element_type=F32)
        return carry

    lax.fori_loop(0, nb // width, init_group, 0)

    def group_pairs(g):
        return [(qi_ref[g * width + u], kj_ref[g * width + u]) for u in range(width)]

    def score_group(g, sc):
        for u, (i, j) in enumerate(group_pairs(g)):
            sc[u] = lax.dot_general(block(k_ref, j), block(q_ref, i), _NT, preferred_element_type=F32)

    def softmax_group(g, sc, p_buf, al_buf):
        pairs = group_pairs(g)
        m_old = [m_s[i] for i, _ in pairs]
        upd = []
        for u, ((i, j), m0) in enumerate(zip(pairs, m_old)):
            s = jnp.where(sel_s[i * nb + j] > 0.0, sc[u], -jnp.inf)
            m1 = jnp.maximum(m0, jnp.max(s, axis=0, keepdims=True))
            p = jnp.exp2(s - m1)
            p_buf[u] = p.astype(BF16)
            alpha = jnp.exp2(m0 - m1)
            al_buf[u] = alpha
            upd.append((i, m1, alpha, jnp.sum(p, axis=0, keepdims=True)))
        for i, m1, alpha, psum in upd:
            m_s[i] = m1
            l_s[i] = alpha * l_s[i] + psum

    def pv_group(g, p_buf, al_buf):
        pairs = group_pairs(g)
        pvs = [jnp.dot(vt_s[j], p_buf[u], preferred_element_type=F32) for u, (_, j) in enumerate(pairs)]
        for u, ((i, _), pv) in enumerate(zip(pairs, pvs)):
            acc_s[i] = al_buf[u] * acc_s[i] + pv

    def step(g, sc_cur, sc_next, p_cur, al_cur, p_prev, al_prev):
        pv_group(jnp.maximum(g - 1, 0), p_prev, al_prev)
        score_group(jnp.minimum(g + 1, n_groups - 1), sc_next)
        softmax_group(g, sc_cur, p_cur, al_cur)

    p_b[...] = jnp.zeros(p_b.shape, BF16)
    al_b[...] = jnp.ones(al_b.shape, F32)
    score_group(0, sc_a)

    def two_steps(t, carry):
        step(2 * t, sc_a, sc_b, p_a, al_a, p_b, al_b)
        step(2 * t + 1, sc_b, sc_a, p_b, al_b, p_a, al_a)
        return carry

    lax.fori_loop(0, n_groups // 2, two_steps, 0)
    pv_group(n_groups - 1, p_b, al_b)

    def finish_two(t, carry):
        for i in (2 * t, 2 * t + 1):
            o_ref[pl.ds(pl.multiple_of(i * blk, blk), blk), :] = (acc_s[i] / l_s[i]).T.astype(o_ref.dtype)
        return carry

    lax.fori_loop(0, nb // 2, finish_two, 0)


def _moba(qkv3, km3):
    b, s, _ = qkv3.shape
    nb = s // MOBA_BLOCK
    assert nb % 2 == 0
    dh = ATTN_HEAD_DIM
    blk = MOBA_BLOCK
    width, (qi, kj) = _moba_schedule(nb)
    assert nb % width == 0 and (len(qi) // width) % 2 == 0
    smem = pl.BlockSpec(memory_space=pltpu.SMEM)
    return pl.pallas_call(
        functools.partial(_moba_kernel, nb=nb, width=width, n_groups=len(qi) // width),
        grid=(b, ATTN_HEADS),
        in_specs=[
            smem,
            smem,
            pl.BlockSpec((None, s, dh), lambda bi, h: (bi, 0, h)),
            pl.BlockSpec((None, s, dh), lambda bi, h: (bi, 0, ATTN_HEADS + h)),
            pl.BlockSpec((None, s, dh), lambda bi, h: (bi, 0, 2 * ATTN_HEADS + h)),
            pl.BlockSpec((None, nb, dh), lambda bi, h: (bi, 0, h)),
        ],
        out_specs=pl.BlockSpec((None, s, dh), lambda bi, h: (bi, 0, h)),
        out_shape=jax.ShapeDtypeStruct((b, s, ATTN_WIDTH), BF16),
        scratch_shapes=[
            pltpu.VMEM((nb, dh, blk), BF16),
            pltpu.VMEM((nb * nb, 1, blk), F32),
            pltpu.VMEM((nb, 1, blk), F32),
            pltpu.VMEM((nb, 1, blk), F32),
            pltpu.VMEM((nb, dh, blk), F32),
            pltpu.VMEM((width, blk, blk), F32),
            pltpu.VMEM((width, blk, blk), F32),
            pltpu.VMEM((width, blk, blk), BF16),
            pltpu.VMEM((width, blk, blk), BF16),
            pltpu.VMEM((width, 1, blk), F32),
            pltpu.VMEM((width, 1, blk), F32),
        ],
        compiler_params=pltpu.CompilerParams(
            dimension_semantics=("parallel", "parallel"), vmem_limit_bytes=VMEM_LIMIT),
        name="moba",
    )(jnp.asarray(qi, jnp.int32), jnp.asarray(kj, jnp.int32), qkv3, qkv3, qkv3, km3)


def _pair_cols(arr, i0):
    rows = arr.shape[0]
    lo = lax.broadcasted_iota(jnp.int32, (rows, LANES), 1) < SSM_HEAD_DIM
    a0 = jnp.broadcast_to(arr[:, i0:i0 + 1], (rows, LANES))
    a1 = jnp.broadcast_to(arr[:, i0 + 1:i0 + 2], (rows, LANES))
    return jnp.where(lo, a0, a1)


def _ssd_kernel(xs_ref, bc_ref, z_ref, dt_ref, cw_ref, cb_ref, dtb_ref, alog_ref, dsk_ref, nw_ref, o_ref,
                ext_s, xbc_s, ht_s, y_s, wx_s, dec_s):
    L = xs_ref.shape[0]
    pad = SUBLANES

    @pl.when(pl.program_id(1) == 0)
    def _():
        ext_s[0:pad, :] = jnp.zeros((pad, XBC_WIDTH), F32)
        ht_s[...] = jnp.zeros(ht_s.shape, F32)

    ext_s[pad:pad + L, 0:SSM_WIDTH] = xs_ref[...]
    ext_s[pad:pad + L, SSM_WIDTH:XBC_WIDTH] = bc_ref[...]
    for cblk in range(XBC_WIDTH // LANES):
        cols = slice(cblk * LANES, (cblk + 1) * LANES)
        conv = cb_ref[:, cols] + cw_ref[0:1, cols] * ext_s[pad - 3:pad - 3 + L, cols]
        for j in range(1, SSM_CONV):
            conv = conv + cw_ref[j:j + 1, cols] * ext_s[pad - 3 + j:pad - 3 + j + L, cols]
        xbc_s[:, cols] = conv * jax.nn.sigmoid(conv)
    ext_s[0:pad, :] = ext_s[L:L + pad, :]

    dtv = dt_ref[...] + dtb_ref[...]
    dt = jnp.maximum(dtv, 0.0) + jnp.log1p(jnp.exp(-jnp.abs(dtv)))
    la = dt * (-jnp.exp(alog_ref[...]))
    row_i = lax.broadcasted_iota(jnp.int32, (L, L), 0)
    col_i = lax.broadcasted_iota(jnp.int32, (L, L), 1)
    tril = row_i >= col_i
    acol = jnp.dot(jnp.where(tril, 1.0, 0.0), la, precision=lax.Precision.HIGHEST,
                   preferred_element_type=F32)
    arow = acol.T
    ecol = jnp.exp(acol)
    aend = acol[L - 1:L, :]
    wcol = jnp.exp(aend - acol)
    eend = jnp.exp(aend)
    lo = lax.broadcasted_iota(jnp.int32, (L, LANES), 1) < SSM_HEAD_DIM

    for g in range(SSM_GROUPS):
        b_off = SSM_WIDTH + g * SSM_STATE
        c_off = SSM_WIDTH + SSM_GROUPS * SSM_STATE + g * SSM_STATE
        bg = xbc_s[:, b_off:b_off + SSM_STATE]
        cg = xbc_s[:, c_off:c_off + SSM_STATE].astype(BF16)
        cb = lax.dot_general(cg, bg.astype(BF16), _NT, preferred_element_type=F32)
        cbm = jnp.where(tril, cb, 0.0)
        ht = ht_s[g]
        ch = jnp.dot(cg, ht.astype(BF16), preferred_element_type=F32)
        for pr in range(SSM_HEADS_PER_GROUP // 2):
            i0 = g * SSM_HEADS_PER_GROUP + 2 * pr
            lanes = slice(i0 * SSM_HEAD_DIM, i0 * SSM_HEAD_DIM + LANES)
            gl = slice(pr * LANES, (pr + 1) * LANES)
            xs_p = xbc_s[:, lanes]
            xdt = xs_p * _pair_cols(dt, i0)
            yd = None
            for hh in range(2):
                idx = i0 + hh
                seg = acol[:, idx:idx + 1] - arow[idx:idx + 1, :]
                mm = (cbm * jnp.exp(jnp.minimum(seg, 0.0))).astype(BF16)
                xh = jnp.where(lo, xdt, 0.0) if hh == 0 else jnp.where(lo, 0.0, xdt)
                part = jnp.dot(mm, xh.astype(BF16), preferred_element_type=F32)
                yd = part if yd is None else yd + part
            y_off = ch[:, gl] * _pair_cols(ecol, i0)
            y_s[:, lanes] = yd + y_off + dsk_ref[:, lanes] * xs_p
            wx_s[:, gl] = (xdt * _pair_cols(wcol, i0)).astype(BF16)
            dec_s[:, gl] = _pair_cols(eend, i0)
        ht_s[g] = ht * dec_s[...] + jnp.dot(bg.T.astype(BF16), wx_s[...], preferred_element_type=F32)

    for g in range(SSM_GROUPS):
        cols = slice(g * SSM_GROUP_WIDTH, (g + 1) * SSM_GROUP_WIDTH)
        zz = z_ref[:, cols]
        yg = y_s[:, cols] * (zz * jax.nn.sigmoid(zz))
        o_ref[:, cols] = _rms(yg, nw_ref[:, cols]).astype(o_ref.dtype)


def _ssd(proj3, dt3, conv_w, conv_b, dt_bias, a_log, d_skip, norm_w):
    b, s, _ = proj3.shape
    L = SSD_CHUNK
    full = lambda shape: pl.BlockSpec(shape, lambda bi, c: (0,) * len(shape))
    return pl.pallas_call(
        _ssd_kernel,
        grid=(b, s // L),
        in_specs=[
            pl.BlockSpec((None, L, SSM_WIDTH), lambda bi, c: (bi, c, 1)),
            pl.BlockSpec((None, L, XBC_WIDTH - SSM_WIDTH),
                         lambda bi, c: (bi, c, 2 * SSM_WIDTH // (XBC_WIDTH - SSM_WIDTH))),
            pl.BlockSpec((None, L, SSM_WIDTH), lambda bi, c: (bi, c, 0)),
            pl.BlockSpec((None, L, LANES), lambda bi, c: (bi, c, 0)),
            full((SSM_CONV, XBC_WIDTH)),
            full((1, XBC_WIDTH)),
            full((1, LANES)),
            full((1, LANES)),
            full((1, SSM_WIDTH)),
            full((1, SSM_WIDTH)),
        ],
        out_specs=pl.BlockSpec((None, L, SSM_WIDTH), lambda bi, c: (bi, c, 0)),
        out_shape=jax.ShapeDtypeStruct((b, s, SSM_WIDTH), BF16),
        scratch_shapes=[
            pltpu.VMEM((L + 2 * SUBLANES, XBC_WIDTH), F32),
            pltpu.VMEM((L, XBC_WIDTH), F32),
            pltpu.VMEM((SSM_GROUPS, SSM_STATE, SSM_GROUP_WIDTH), F32),
            pltpu.VMEM((L, SSM_WIDTH), F32),
            pltpu.VMEM((L, SSM_GROUP_WIDTH), BF16),
            pltpu.VMEM((1, SSM_GROUP_WIDTH), F32),
        ],
        compiler_params=pltpu.CompilerParams(
            dimension_semantics=("parallel", "arbitrary"), vmem_limit_bytes=VMEM_LIMIT),
        name="ssd",
    )(proj3, proj3, proj3, dt3, conv_w, conv_b, dt_bias, a_log, d_skip, norm_w)


def _out_proj_kernel(x_ref, a_ref, s_ref, wa_ref, ws_ref, o_ref):
    o_ref[...] = (x_ref[...]
                  + jnp.dot(a_ref[...], wa_ref[...], preferred_element_type=F32)
                  + jnp.dot(s_ref[...], ws_ref[...], preferred_element_type=F32))


def _out_proj(x2d, attn2d, ssm2d, w_out):
    t, d = x2d.shape
    return pl.pallas_call(
        _out_proj_kernel,
        grid=(t // ROW_TILE,),
        in_specs=[
            pl.BlockSpec((ROW_TILE, d), lambda i: (i, 0)),
            pl.BlockSpec((ROW_TILE, ATTN_WIDTH), lambda i: (i, 0)),
            pl.BlockSpec((ROW_TILE, SSM_WIDTH), lambda i: (i, 0)),
            pl.BlockSpec((ATTN_WIDTH, d), lambda i: (0, 0)),
            pl.BlockSpec((SSM_WIDTH, d), lambda i: (ATTN_WIDTH // SSM_WIDTH, 0)),
        ],
        out_specs=pl.BlockSpec((ROW_TILE, d), lambda i: (i, 0)),
        out_shape=jax.ShapeDtypeStruct((t, d), F32),
        compiler_params=pltpu.CompilerParams(
            dimension_semantics=("parallel",), vmem_limit_bytes=VMEM_LIMIT),
        name="out_proj",
    )(x2d, attn2d, ssm2d, w_out, w_out)


def _conv_glu_kernel(x_ref, halo_ref, xres_ref, nw_ref, wg_ref, wv_ref, cwg_ref, cwv_ref, cbg_ref, cbv_ref,
                     wd_ref, o_ref, h_s, hp_s, ug0_s, uv0_s, ug1_s, uv1_s, act_s, op_s, *, tiles_per_seq, n_up):
    tm = x_ref.shape[0]
    tf = wg_ref.shape[1]
    span = tm // SUBLANES
    halo = FFN_HALO
    i = pl.program_id(0)
    f = pl.program_id(1)

    u_bufs = ((ug0_s, uv0_s), (ug1_s, uv1_s))

    n_parts = 2 * (tf // MXU_COLS)

    def up_dot_part(slot, part):
        u_s, w_ref = ((u_bufs[slot][0], wg_ref), (u_bufs[slot][1], wv_ref))[part // (tf // MXU_COLS)]
        cols = slice((part % (tf // MXU_COLS)) * MXU_COLS, (part % (tf // MXU_COLS) + 1) * MXU_COLS)
        u_s[:, cols] = jnp.dot(hp_s[...], w_ref[:, cols], preferred_element_type=F32)

    def up_fix(slot):
        for u_s in u_bufs[slot]:
            before = [jnp.concatenate([u_s[halo - d:halo - d + 1, :],
                                       u_s[halo + tm - d * SUBLANES:halo + tm - d * SUBLANES + SUBLANES - 1, :]],
                                      axis=0) for d in (1, 2)]
            u_s[halo - SUBLANES:halo, :] = before[0]
            u_s[halo - 2 * SUBLANES:halo - SUBLANES, :] = before[1]

    def conv(u_s, cw_ref, cb_ref, r0, cols):
        out = cb_ref[:, cols]
        for j in range(FFN_CONV):
            first = halo - (FFN_CONV - 1 - j) * SUBLANES + r0
            out = out + cw_ref[j:j + 1, cols] * u_s[first:first + CONV_ROWS, cols]
        return out

    def conv_act(slot, chunk, part=0, parts=1):
        ug_s, uv_s = u_bufs[slot]
        for c0 in range(part * (tf // parts), (part + 1) * (tf // parts), LANES):
            cols = slice(c0, c0 + LANES)
            for r0 in range(0, tm, CONV_ROWS):
                gate = conv(ug_s, cwg_ref, cbg_ref, r0, cols)
                val = conv(uv_s, cwv_ref, cbv_ref, r0, cols)
                act_s[chunk, r0:r0 + CONV_ROWS, cols] = (gate * jax.nn.sigmoid(gate) * val).astype(BF16)

    @pl.when(f == 0)
    def _():
        _norm_rows(x_ref, nw_ref, h_s, 0, tm)
        keep = jnp.where(i % tiles_per_seq == 0, 0.0, 1.0)
        hp_s[0:halo, :] = (_rms(halo_ref[...], nw_ref[...]) * keep).astype(BF16)
        pos = lax.broadcasted_iota(jnp.int32, (tm, tm), 0)
        row = lax.broadcasted_iota(jnp.int32, (tm, tm), 1)
        perm = jnp.where(row == (pos % SUBLANES) * span + pos // SUBLANES, 1.0, 0.0).astype(BF16)
        for c0 in range(0, h_s.shape[1], tf):
            hp_s[halo:, c0:c0 + tf] = jnp.dot(perm, h_s[:, c0:c0 + tf],
                                              preferred_element_type=F32).astype(BF16)
        for part in range(n_parts):
            up_dot_part(0, part)
        up_fix(0)

    for parity in (0, 1):
        @pl.when((f >= 1) & (f < n_up) & (f % 2 == parity))
        def _():
            for part in range(n_parts):
                up_dot_part(parity, part)
                conv_act(1 - parity, f - 1, part, n_parts)
            up_fix(parity)

    @pl.when(f == n_up)
    def _():
        conv_act((n_up - 1) % 2, n_up - 1)

    @pl.when(f >= n_up)
    def _():
        acc = jnp.dot(act_s[0], wd_ref[0:tf, :], preferred_element_type=F32)
        for c in range(1, n_up):
            acc = acc + jnp.dot(act_s[c], wd_ref[c * tf:(c + 1) * tf, :], preferred_element_type=F32)
        for c in range(op_s.shape[0]):
            op_s[c] = acc[:, c * LANES:(c + 1) * LANES]
        for a in range(SUBLANES):
            rows = slice(a * span, (a + 1) * span)
            for c in range(op_s.shape[0]):
                cols = slice(c * LANES, (c + 1) * LANES)
                o_ref[rows, cols] = xres_ref[rows, cols] + op_s[c, pl.ds(a, span, stride=SUBLANES), :]


def _conv_glu(x2d, nw, w_up, conv_w, conv_b, w_down, seq_len):
    t, d = x2d.shape
    tm, tf, tn = ROW_TILE, COL_TILE, COL_TILE
    n_up = D_FF // tf
    n_down = d // tn
    halo_blocks = tm // FFN_HALO
    up = lambda f: jnp.minimum(f, n_up - 1)
    cv = lambda f: jnp.clip(f - 1, 0, n_up - 1)
    down = lambda f: jnp.maximum(f - n_up, 0)
    return pl.pallas_call(
        functools.partial(_conv_glu_kernel, tiles_per_seq=seq_len // tm, n_up=n_up),
        grid=(t // tm, n_up + n_down),
        in_specs=[
            pl.BlockSpec((tm, d), lambda i, f: (i, 0)),
            pl.BlockSpec((FFN_HALO, d), lambda i, f: (jnp.maximum(i * halo_blocks - 1, 0), 0)),
            pl.BlockSpec((tm, tn), lambda i, f: (i, down(f))),
            pl.BlockSpec((1, d), lambda i, f: (0, 0)),
            pl.BlockSpec((d, tf), lambda i, f: (0, up(f))),
            pl.BlockSpec((d, tf), lambda i, f: (0, n_up + up(f))),
            pl.BlockSpec((FFN_CONV, tf), lambda i, f: (0, cv(f))),
            pl.BlockSpec((FFN_CONV, tf), lambda i, f: (0, n_up + cv(f))),
            pl.BlockSpec((1, tf), lambda i, f: (0, cv(f))),
            pl.BlockSpec((1, tf), lambda i, f: (0, n_up + cv(f))),
            pl.BlockSpec((D_FF, tn), lambda i, f: (0, down(f))),
        ],
        out_specs=pl.BlockSpec((tm, tn), lambda i, f: (i, down(f))),
        out_shape=jax.ShapeDtypeStruct((t, d), F32),
        scratch_shapes=[
            pltpu.VMEM((tm, d), BF16),
            pltpu.VMEM((FFN_HALO + tm, d), BF16),
            pltpu.VMEM((FFN_HALO + tm, tf), F32),
            pltpu.VMEM((FFN_HALO + tm, tf), F32),
            pltpu.VMEM((FFN_HALO + tm, tf), F32),
            pltpu.VMEM((FFN_HALO + tm, tf), F32),
            pltpu.VMEM((n_up, tm, tf), BF16),
            pltpu.VMEM((tn // LANES, tm, LANES), F32),
        ],
        compiler_params=pltpu.CompilerParams(
            dimension_semantics=("parallel", "arbitrary"), vmem_limit_bytes=VMEM_LIMIT),
        name="conv_glu",
    )(x2d, x2d, x2d, nw, w_up, w_up, conv_w, conv_w, conv_b, conv_b, w_down)


def _pad_lanes(v):
    return jnp.pad(v.reshape(1, -1), ((0, 0), (0, LANES - v.shape[-1])))


def _layer(x2d, batch, seq, norm1_w, w_in, q_norm_w, k_norm_w, ssm_conv_w, ssm_conv_b, dt_bias, a_log,
           d_skip, ssm_norm_w, w_out, norm2_w, w_up, ffn_conv_w, ffn_conv_b, w_down):
    w_dt = jnp.pad(w_in[:, PROJ_MAIN:], ((0, 0), (0, LANES - SSM_HEADS))).astype(BF16)
    qkv, zxbc, k_mean, dt_raw = _in_proj(x2d, norm1_w.reshape(1, -1), w_in.astype(BF16), w_dt,
                                         q_norm_w.reshape(1, -1), k_norm_w.reshape(1, -1))

    attn = _moba(qkv.reshape(batch, seq, 3 * ATTN_WIDTH), k_mean.reshape(batch, seq // MOBA_BLOCK, ATTN_WIDTH))
    ssm = _ssd(zxbc.reshape(batch, seq, -1), dt_raw.reshape(batch, seq, LANES), ssm_conv_w, ssm_conv_b.reshape(1, -1),
               _pad_lanes(dt_bias), _pad_lanes(a_log),
               jnp.repeat(d_skip, SSM_HEAD_DIM).reshape(1, -1), ssm_norm_w.reshape(1, -1))

    x1 = _out_proj(x2d, attn.reshape(-1, ATTN_WIDTH), ssm.reshape(-1, SSM_WIDTH), w_out.astype(BF16))
    return _conv_glu(x1, norm2_w.reshape(1, -1), w_up.astype(BF16), ffn_conv_w, ffn_conv_b.reshape(1, -1),
                     w_down.astype(BF16), seq)


def kernel(x, norm1_w, w_in, q_norm_w, k_norm_w, ssm_conv_w, ssm_conv_b, dt_bias, a_log, d_skip, ssm_norm_w,
           w_out, norm2_w, w_up, ffn_conv_w, ffn_conv_b, w_down):
    batch, seq, d = x.shape
    x2d = x.reshape(batch * seq, d)
    for i in range(norm1_w.shape[0]):
        x2d = _layer(x2d, batch, seq, norm1_w[i], w_in[i], q_norm_w[i], k_norm_w[i], ssm_conv_w[i],
                     ssm_conv_b[i], dt_bias[i], a_log[i], d_skip[i], ssm_norm_w[i], w_out[i], norm2_w[i],
                     w_up[i], ffn_conv_w[i], ffn_conv_b[i], w_down[i])
    return x2d.reshape(batch, seq, d)
```

```python
import functools

import jax
import jax.numpy as jnp
from jax import lax
from jax.experimental import pallas as pl
from jax.experimental.pallas import tpu as pltpu

F32 = jnp.float32
BF16 = jnp.bfloat16

D_MODEL = 2048
ATTN_WIDTH = 1024
ATTN_HEAD_DIM = 128
ATTN_HEADS = ATTN_WIDTH // ATTN_HEAD_DIM
MOBA_BLOCK = 256
MOBA_TOPK = 3
SSM_WIDTH = 1024
SSM_HEAD_DIM = 64
SSM_HEADS = SSM_WIDTH // SSM_HEAD_DIM
SSM_GROUPS = 2
SSM_HEADS_PER_GROUP = SSM_HEADS // SSM_GROUPS
SSM_GROUP_WIDTH = SSM_WIDTH // SSM_GROUPS
SSM_STATE = 128
SSM_CONV = 4
XBC_WIDTH = SSM_WIDTH + 2 * SSM_GROUPS * SSM_STATE
PROJ_MAIN = 3 * ATTN_WIDTH + SSM_WIDTH + XBC_WIDTH
D_FF = 5632
FFN_CONV = 3
EPS = 1e-6
LOG2_E = 1.4426950408889634

LANES = 128
MXU_COLS = 256
SUBLANES = 8
VMEM_LIMIT = 56 * 1024 * 1024

SSD_CHUNK = 256
ROW_TILE = 512
IN_ROW_TILE = 256
COL_TILE = 512
FFN_HALO = 16
NORM_ROWS = 64
CONV_ROWS = 64
CAST_ROWS = 256

_NT = (((1,), (1,)), ((), ()))


def _rms(x, w):
    return x * lax.rsqrt(jnp.mean(x * x, axis=-1, keepdims=True) + EPS) * w


def _norm_rows(x_ref, nw_ref, h_ref, dst_off, n_rows):
    def body(c, carry):
        r = pl.multiple_of(c * NORM_ROWS, NORM_ROWS)
        h_ref[pl.ds(dst_off + r, NORM_ROWS), :] = _rms(x_ref[pl.ds(r, NORM_ROWS), :], nw_ref[...]).astype(BF16)
        return carry
    lax.fori_loop(0, n_rows // NORM_ROWS, body, 0)


def _in_proj_kernel(x_ref, nw_ref, w_ref, wdt_ref, qw_ref, kw_ref, qkv_ref, rest_ref, km_ref, dt_ref, h_ref):
    _norm_rows(x_ref, nw_ref, h_ref, 0, x_ref.shape[0])
    dt_ref[...] = jnp.dot(h_ref[...], wdt_ref[...], preferred_element_type=F32)
    q_scale = ATTN_HEAD_DIM ** -0.5 * LOG2_E
    for c0 in range(0, PROJ_MAIN, COL_TILE):
        acc = jnp.dot(h_ref[...], w_ref[:, c0:c0 + COL_TILE], preferred_element_type=F32)
        if c0 >= 3 * ATTN_WIDTH:
            rest_ref[:, c0 - 3 * ATTN_WIDTH:c0 - 3 * ATTN_WIDTH + COL_TILE] = acc
        else:
            for h0 in range(0, COL_TILE, ATTN_HEAD_DIM):
                head = acc[:, h0:h0 + ATTN_HEAD_DIM]
                slot = (c0 + h0) // ATTN_HEAD_DIM
                if c0 >= 2 * ATTN_WIDTH:
                    qkv_ref[slot] = head.astype(BF16)
                elif c0 < ATTN_WIDTH:
                    qkv_ref[slot] = (_rms(head, qw_ref[...]) * q_scale).astype(BF16)
                else:
                    kn = _rms(head, kw_ref[...])
                    qkv_ref[slot] = kn.astype(BF16)
                    kcols = slice(c0 + h0 - ATTN_WIDTH, c0 + h0 - ATTN_WIDTH + ATTN_HEAD_DIM)
                    km_ref[:, kcols] = jnp.mean(kn, axis=0, keepdims=True)


def _in_proj(x2d, nw, w_all, layer, w_dt, qw, kw):
    t, d = x2d.shape
    tm = MOBA_BLOCK
    rest = PROJ_MAIN - 3 * ATTN_WIDTH
    n_slots = 3 * ATTN_HEADS
    return pl.pallas_call(
        _in_proj_kernel,
        grid=(t // tm,),
        in_specs=[
            pl.BlockSpec((tm, d), lambda i: (i, 0)),
            pl.BlockSpec((1, d), lambda i: (0, 0)),
            pl.BlockSpec((None,) + w_all.shape[1:], lambda i: (layer, 0, 0), pipeline_mode=pl.Buffered(1)),
            pl.BlockSpec((d, LANES), lambda i: (0, 0)),
            pl.BlockSpec((1, ATTN_HEAD_DIM), lambda i: (0, 0)),
            pl.BlockSpec((1, ATTN_HEAD_DIM), lambda i: (0, 0)),
        ],
        out_specs=[
            pl.BlockSpec((n_slots, tm, ATTN_HEAD_DIM), lambda i: (0, i, 0)),
            pl.BlockSpec((tm, rest), lambda i: (i, 0)),
            pl.BlockSpec((None, 1, ATTN_WIDTH), lambda i: (i, 0, 0)),
            pl.BlockSpec((tm, LANES), lambda i: (i, 0)),
        ],
        out_shape=[jax.ShapeDtypeStruct((n_slots, t, ATTN_HEAD_DIM), BF16), jax.ShapeDtypeStruct((t, rest), F32),
                   jax.ShapeDtypeStruct((t // tm, 1, ATTN_WIDTH), F32), jax.ShapeDtypeStruct((t, LANES), F32)],
        scratch_shapes=[pltpu.VMEM((tm, d), BF16)],
        compiler_params=pltpu.CompilerParams(
            dimension_semantics=("parallel",), vmem_limit_bytes=VMEM_LIMIT),
        name="in_proj",
    )(x2d, nw, w_all, w_dt, qw, kw)


def _pair_schedule(nb, width):
    remaining = {i: list(range(i)) for i in range(1, nb)}
    qi, kj = [], []
    while any(remaining.values()):
        live = sorted((i for i in remaining if remaining[i]), key=lambda i: -len(remaining[i]))
        if len(live) < width:
            return None
        for i in live[:width]:
            qi.append(i)
            kj.append(remaining[i].pop())
    return qi, kj


def _moba_schedule(nb):
    for width in (4, 2, 1):
        sched = _pair_schedule(nb, width)
        if sched is not None and width <= nb - 1:
            qi, kj = sched
            if (len(qi) // width) % 2:
                qi = qi + list(range(1, width + 1))
                kj = kj + [nb - 1] * width
            return width, (qi, kj)
    raise ValueError(f"no MoBA pair schedule for {nb} blocks")


def _moba_kernel(qi_ref, kj_ref, q_ref, k_ref, v_ref, km_ref, o_ref,
                 vt_s, sel_s, m_s, l_s, acc_s,
                 sc_a, sc_b, p_a, p_b, al_a, al_b, *, nb, width, n_groups):
    blk = MOBA_BLOCK
    for j in range(nb):
        vt_s[j] = v_ref[j * blk:(j + 1) * blk, :].astype(F32).T.astype(BF16)

    def block(ref, i):
        return ref[pl.ds(pl.multiple_of(i * blk, blk), blk), :]

    key_i = lax.broadcasted_iota(jnp.int32, (blk, blk), 0)
    qry_i = lax.broadcasted_iota(jnp.int32, (blk, blk), 1)
    bid = lax.broadcasted_iota(jnp.int32, (nb, blk), 0)

    def init_group(t, carry):
        ids = [t * width + u for u in range(width)]
        gates = [lax.dot_general(km_ref[...], block(q_ref, i).astype(F32), _NT, precision=lax.Precision.HIGHEST,
                                 preferred_element_type=F32) for i in ids]
        owns = [lax.dot_general(block(k_ref, i), block(q_ref, i), _NT, preferred_element_type=F32) for i in ids]
        for i, gate in zip(ids, gates):
            past = bid < i
            g = jnp.where(past, gate, -jnp.inf)
            rank = jnp.zeros((nb, blk), jnp.int32)
            for jp in range(nb):
                row = g[jp:jp + 1, :]
                rank = rank + jnp.where(row > g, 1, jnp.where(row == g, jnp.where(bid > jp, 1, 0), 0))
            sel = jnp.where(past, jnp.where(rank < MOBA_TOPK, 1.0, 0.0), 0.0)
            for jp in range(nb):
                sel_s[i * nb + jp] = sel[jp:jp + 1, :]
        for i, s in zip(ids, owns):
            s = jnp.where(key_i <= qry_i, s, -jnp.inf)
            m = jnp.max(s, axis=0, keepdims=True)
            p = jnp.exp2(s - m)
            m_s[i] = m
            l_s[i] = jnp.sum(p, axis=0, keepdims=True)
            acc_s[i] = jnp.dot(vt_s[i], p.astype(BF16), preferred_element_type=F32)
        return carry

    lax.fori_loop(0, nb // width, init_group, 0)

    def group_pairs(g):
        return [(qi_ref[g * width + u], kj_ref[g * width + u]) for u in range(width)]

    def score_group(g, sc):
        for u, (i, j) in enumerate(group_pairs(g)):
            sc[u] = lax.dot_general(block(k_ref, j), block(q_ref, i), _NT, preferred_element_type=F32)

    def softmax_group(g, sc, p_buf, al_buf):
        pairs = group_pairs(g)
        m_old = [m_s[i] for i, _ in pairs]
        upd = []
        for u, ((i, j), m0) in enumerate(zip(pairs, m_old)):
            s = jnp.where(sel_s[i * nb + j] > 0.0, sc[u], -jnp.inf)
            m1 = jnp.maximum(m0, jnp.max(s, axis=0, keepdims=True))
            p = jnp.exp2(s - m1)
            p_buf[u] = p.astype(BF16)
            alpha = jnp.exp2(m0 - m1)
            al_buf[u] = alpha
            upd.append((i, m1, alpha, jnp.sum(p, axis=0, keepdims=True)))
        for i, m1, alpha, psum in upd:
            m_s[i] = m1
            l_s[i] = alpha * l_s[i] + psum

    def pv_group(g, p_buf, al_buf):
        pairs = group_pairs(g)
        pvs = [jnp.dot(vt_s[j], p_buf[u], preferred_element_type=F32) for u, (_, j) in enumerate(pairs)]
        for u, ((i, _), pv) in enumerate(zip(pairs, pvs)):
            acc_s[i] = al_buf[u] * acc_s[i] + pv

    def step(g, sc_cur, sc_next, p_cur, al_cur, p_prev, al_prev):
        pv_group(jnp.maximum(g - 1, 0), p_prev, al_prev)
        score_group(jnp.minimum(g + 1, n_groups - 1), sc_next)
        softmax_group(g, sc_cur, p_cur, al_cur)

    p_b[...] = jnp.zeros(p_b.shape, BF16)
    al_b[...] = jnp.ones(al_b.shape, F32)
    score_group(0, sc_a)

    def two_steps(t, carry):
        step(2 * t, sc_a, sc_b, p_a, al_a, p_b, al_b)
        step(2 * t + 1, sc_b, sc_a, p_b, al_b, p_a, al_a)
        return carry

    lax.fori_loop(0, n_groups // 2, two_steps, 0)
    pv_group(n_groups - 1, p_b, al_b)

    def finish_two(t, carry):
        for i in (2 * t, 2 * t + 1):
            o_ref[pl.ds(pl.multiple_of(i * blk, blk), blk), :] = (acc_s[i] / l_s[i]).T.astype(o_ref.dtype)
        return carry

    lax.fori_loop(0, nb // 2, finish_two, 0)


def _moba(qkv4, km3):
    _, b, s, _ = qkv4.shape
    nb = s // MOBA_BLOCK
    assert nb % 2 == 0
    dh = ATTN_HEAD_DIM
    blk = MOBA_BLOCK
    width, (qi, kj) = _moba_schedule(nb)
    assert nb % width == 0 and (len(qi) // width) % 2 == 0
    smem = pl.BlockSpec(memory_space=pltpu.SMEM)
    return pl.pallas_call(
        functools.partial(_moba_kernel, nb=nb, width=width, n_groups=len(qi) // width),
        grid=(b, ATTN_HEADS),
        in_specs=[
            smem,
            smem,
            pl.BlockSpec((None, None, s, dh), lambda bi, h: (h, bi, 0, 0)),
            pl.BlockSpec((None, None, s, dh), lambda bi, h: (ATTN_HEADS + h, bi, 0, 0)),
            pl.BlockSpec((None, None, s, dh), lambda bi, h: (2 * ATTN_HEADS + h, bi, 0, 0)),
            pl.BlockSpec((None, nb, dh), lambda bi, h: (bi, 0, h)),
        ],
        out_specs=pl.BlockSpec((None, None, s, dh), lambda bi, h: (h, bi, 0, 0)),
        out_shape=jax.ShapeDtypeStruct((ATTN_HEADS, b, s, dh), BF16),
        scratch_shapes=[
            pltpu.VMEM((nb, dh, blk), BF16),
            pltpu.VMEM((nb * nb, 1, blk), F32),
            pltpu.VMEM((nb, 1, blk), F32),
            pltpu.VMEM((nb, 1, blk), F32),
            pltpu.VMEM((nb, dh, blk), F32),
            pltpu.VMEM((width, blk, blk), F32),
            pltpu.VMEM((width, blk, blk), F32),
            pltpu.VMEM((width, blk, blk), BF16),
            pltpu.VMEM((width, blk, blk), BF16),
            pltpu.VMEM((width, 1, blk), F32),
            pltpu.VMEM((width, 1, blk), F32),
        ],
        compiler_params=pltpu.CompilerParams(
            dimension_semantics=("parallel", "parallel"), vmem_limit_bytes=VMEM_LIMIT),
        name="moba",
    )(jnp.asarray(qi, jnp.int32), jnp.asarray(kj, jnp.int32), qkv4, qkv4, qkv4, km3)


def _pair_cols(arr, i0):
    rows = arr.shape[0]
    lo = lax.broadcasted_iota(jnp.int32, (rows, LANES), 1) < SSM_HEAD_DIM
    a0 = jnp.broadcast_to(arr[:, i0:i0 + 1], (rows, LANES))
    a1 = jnp.broadcast_to(arr[:, i0 + 1:i0 + 2], (rows, LANES))
    return jnp.where(lo, a0, a1)


def _ssd_kernel(xs_ref, bc_ref, z_ref, dt_ref, cw_ref, cb_ref, dtb_ref, alog_ref, dsk_ref, nw_ref, o_ref,
                ext_s, xbc_s, ht_s, y_s, wx_s, dec_s):
    L = xs_ref.shape[0]
    pad = SUBLANES

    @pl.when(pl.program_id(1) == 0)
    def _():
        ext_s[0:pad, :] = jnp.zeros((pad, XBC_WIDTH), F32)
        ht_s[...] = jnp.zeros(ht_s.shape, F32)

    ext_s[pad:pad + L, 0:SSM_WIDTH] = xs_ref[...]
    ext_s[pad:pad + L, SSM_WIDTH:XBC_WIDTH] = bc_ref[...]
    for cblk in range(XBC_WIDTH // LANES):
        cols = slice(cblk * LANES, (cblk + 1) * LANES)
        conv = cb_ref[:, cols] + cw_ref[0:1, cols] * ext_s[pad - 3:pad - 3 + L, cols]
        for j in range(1, SSM_CONV):
            conv = conv + cw_ref[j:j + 1, cols] * ext_s[pad - 3 + j:pad - 3 + j + L, cols]
        xbc_s[:, cols] = conv * jax.nn.sigmoid(conv)
    ext_s[0:pad, :] = ext_s[L:L + pad, :]

    dtv = dt_ref[...] + dtb_ref[...]
    dt = jnp.maximum(dtv, 0.0) + jnp.log1p(jnp.exp(-jnp.abs(dtv)))
    la = dt * (-jnp.exp(alog_ref[...]))
    row_i = lax.broadcasted_iota(jnp.int32, (L, L), 0)
    col_i = lax.broadcasted_iota(jnp.int32, (L, L), 1)
    tril = row_i >= col_i
    acol = jnp.dot(jnp.where(tril, 1.0, 0.0), la, precision=lax.Precision.HIGHEST,
                   preferred_element_type=F32)
    arow = acol.T
    ecol = jnp.exp(acol)
    aend = acol[L - 1:L, :]
    wcol = jnp.exp(aend - acol)
    eend = jnp.exp(aend)
    lo = lax.broadcasted_iota(jnp.int32, (L, LANES), 1) < SSM_HEAD_DIM

    for g in range(SSM_GROUPS):
        b_off = SSM_WIDTH + g * SSM_STATE
        c_off = SSM_WIDTH + SSM_GROUPS * SSM_STATE + g * SSM_STATE
        bg = xbc_s[:, b_off:b_off + SSM_STATE]
        cg = xbc_s[:, c_off:c_off + SSM_STATE].astype(BF16)
        cb = lax.dot_general(cg, bg.astype(BF16), _NT, preferred_element_type=F32)
        cbm = jnp.where(tril, cb, 0.0)
        ht = ht_s[g]
        ch = jnp.dot(cg, ht.astype(BF16), preferred_element_type=F32)
        for pr in range(SSM_HEADS_PER_GROUP // 2):
            i0 = g * SSM_HEADS_PER_GROUP + 2 * pr
            lanes = slice(i0 * SSM_HEAD_DIM, i0 * SSM_HEAD_DIM + LANES)
            gl = slice(pr * LANES, (pr + 1) * LANES)
            xs_p = xbc_s[:, lanes]
            xdt = xs_p * _pair_cols(dt, i0)
            yd = None
            for hh in range(2):
                idx = i0 + hh
                seg = acol[:, idx:idx + 1] - arow[idx:idx + 1, :]
                mm = (cbm * jnp.exp(jnp.minimum(seg, 0.0))).astype(BF16)
                xh = jnp.where(lo, xdt, 0.0) if hh == 0 else jnp.where(lo, 0.0, xdt)
                part = jnp.dot(mm, xh.astype(BF16), preferred_element_type=F32)
                yd = part if yd is None else yd + part
            y_off = ch[:, gl] * _pair_cols(ecol, i0)
            y_s[:, lanes] = yd + y_off + dsk_ref[:, lanes] * xs_p
            wx_s[:, gl] = (xdt * _pair_cols(wcol, i0)).astype(BF16)
            dec_s[:, gl] = _pair_cols(eend, i0)
        ht_s[g] = ht * dec_s[...] + jnp.dot(bg.T.astype(BF16), wx_s[...], preferred_element_type=F32)

    for g in range(SSM_GROUPS):
        cols = slice(g * SSM_GROUP_WIDTH, (g + 1) * SSM_GROUP_WIDTH)
        zz = z_ref[:, cols]
        yg = y_s[:, cols] * (zz * jax.nn.sigmoid(zz))
        o_ref[:, cols] = _rms(yg, nw_ref[:, cols]).astype(o_ref.dtype)


def _ssd(proj3, dt3, conv_w, conv_b, dt_bias, a_log, d_skip, norm_w):
    b, s, _ = proj3.shape
    L = SSD_CHUNK
    full = lambda shape: pl.BlockSpec(shape, lambda bi, c: (0,) * len(shape))
    return pl.pallas_call(
        _ssd_kernel,
        grid=(b, s // L),
        in_specs=[
            pl.BlockSpec((None, L, SSM_WIDTH), lambda bi, c: (bi, c, 1)),
            pl.BlockSpec((None, L, XBC_WIDTH - SSM_WIDTH),
                         lambda bi, c: (bi, c, 2 * SSM_WIDTH // (XBC_WIDTH - SSM_WIDTH))),
            pl.BlockSpec((None, L, SSM_WIDTH), lambda bi, c: (bi, c, 0)),
            pl.BlockSpec((None, L, LANES), lambda bi, c: (bi, c, 0)),
            full((SSM_CONV, XBC_WIDTH)),
            full((1, XBC_WIDTH)),
            full((1, LANES)),
            full((1, LANES)),
            full((1, SSM_WIDTH)),
            full((1, SSM_WIDTH)),
        ],
        out_specs=pl.BlockSpec((None, L, SSM_WIDTH), lambda bi, c: (bi, c, 0)),
        out_shape=jax.ShapeDtypeStruct((b, s, SSM_WIDTH), BF16),
        scratch_shapes=[
            pltpu.VMEM((L + 2 * SUBLANES, XBC_WIDTH), F32),
            pltpu.VMEM((L, XBC_WIDTH), F32),
            pltpu.VMEM((SSM_GROUPS, SSM_STATE, SSM_GROUP_WIDTH), F32),
            pltpu.VMEM((L, SSM_WIDTH), F32),
            pltpu.VMEM((L, SSM_GROUP_WIDTH), BF16),
            pltpu.VMEM((1, SSM_GROUP_WIDTH), F32),
        ],
        compiler_params=pltpu.CompilerParams(
            dimension_semantics=("parallel", "arbitrary"), vmem_limit_bytes=VMEM_LIMIT),
        name="ssd",
    )(proj3, proj3, proj3, dt3, conv_w, conv_b, dt_bias, a_log, d_skip, norm_w)


def _out_proj_kernel(x_ref, a_ref, s_ref, wa_ref, ws_ref, o_ref):
    attn = jnp.concatenate([a_ref[h] for h in range(a_ref.shape[0])], axis=1)
    o_ref[...] = (x_ref[...]
                  + jnp.dot(attn, wa_ref[...], preferred_element_type=F32)
                  + jnp.dot(s_ref[...], ws_ref[...], preferred_element_type=F32))


def _out_proj(x2d, attn2d, ssm2d, w_out, layer):
    t, d = x2d.shape
    return pl.pallas_call(
        _out_proj_kernel,
        grid=(t // ROW_TILE,),
        in_specs=[
            pl.BlockSpec((ROW_TILE, d), lambda i: (i, 0)),
            pl.BlockSpec((ATTN_HEADS, ROW_TILE, ATTN_HEAD_DIM), lambda i: (0, i, 0)),
            pl.BlockSpec((ROW_TILE, SSM_WIDTH), lambda i: (i, 0)),
            pl.BlockSpec((None, ATTN_WIDTH, d), lambda i: (layer, 0, 0)),
            pl.BlockSpec((None, SSM_WIDTH, d), lambda i: (layer, ATTN_WIDTH // SSM_WIDTH, 0)),
        ],
        out_specs=pl.BlockSpec((ROW_TILE, d), lambda i: (i, 0)),
        out_shape=jax.ShapeDtypeStruct((t, d), F32),
        compiler_params=pltpu.CompilerParams(
            dimension_semantics=("parallel",), vmem_limit_bytes=VMEM_LIMIT),
        name="out_proj",
    )(x2d, attn2d, ssm2d, w_out, w_out)


def _conv_glu_kernel(x_ref, halo_ref, xres_ref, nw_ref, wg_ref, wv_ref, cwg_ref, cwv_ref, cbg_ref, cbv_ref,
                     wd_ref, o_ref, h_s, hp_s, ug0_s, uv0_s, ug1_s, uv1_s, act_s, op_s, *, tiles_per_seq, n_up):
    tm = x_ref.shape[0]
    tf = wg_ref.shape[1]
    span = tm // SUBLANES
    halo = FFN_HALO
    i = pl.program_id(0)
    f = pl.program_id(1)

    u_bufs = ((ug0_s, uv0_s), (ug1_s, uv1_s))

    n_parts = 2 * (tf // MXU_COLS)

    def up_dot_part(slot, part):
        u_s, w_ref = ((u_bufs[slot][0], wg_ref), (u_bufs[slot][1], wv_ref))[part // (tf // MXU_COLS)]
        cols = slice((part % (tf // MXU_COLS)) * MXU_COLS, (part % (tf // MXU_COLS) + 1) * MXU_COLS)
        u_s[:, cols] = jnp.dot(hp_s[...], w_ref[:, cols], preferred_element_type=F32)

    def up_fix(slot):
        for u_s in u_bufs[slot]:
            before = [jnp.concatenate([u_s[halo - d:halo - d + 1, :],
                                       u_s[halo + tm - d * SUBLANES:halo + tm - d * SUBLANES + SUBLANES - 1, :]],
                                      axis=0) for d in (1, 2)]
            u_s[halo - SUBLANES:halo, :] = before[0]
            u_s[halo - 2 * SUBLANES:halo - SUBLANES, :] = before[1]

    def conv(u_s, cw_ref, cb_ref, r0, cols):
        out = cb_ref[:, cols]
        for j in range(FFN_CONV):
            first = halo - (FFN_CONV - 1 - j) * SUBLANES + r0
            out = out + cw_ref[j:j + 1, cols] * u_s[first:first + CONV_ROWS, cols]
        return out

    def conv_act(slot, chunk, part=0, parts=1):
        ug_s, uv_s = u_bufs[slot]
        for c0 in range(part * (tf // parts), (part + 1) * (tf // parts), LANES):
            cols = slice(c0, c0 + LANES)
            for r0 in range(0, tm, CONV_ROWS):
                gate = conv(ug_s, cwg_ref, cbg_ref, r0, cols)
                val = conv(uv_s, cwv_ref, cbv_ref, r0, cols)
                act_s[chunk, r0:r0 + CONV_ROWS, cols] = (gate * jax.nn.sigmoid(gate) * val).astype(BF16)

    @pl.when(f == 0)
    def _():
        _norm_rows(x_ref, nw_ref, h_s, 0, tm)
        keep = jnp.where(i % tiles_per_seq == 0, 0.0, 1.0)
        hp_s[0:halo, :] = (_rms(halo_ref[...], nw_ref[...]) * keep).astype(BF16)
        pos = lax.broadcasted_iota(jnp.int32, (tm, tm), 0)
        row = lax.broadcasted_iota(jnp.int32, (tm, tm), 1)
        perm = jnp.where(row == (pos % SUBLANES) * span + pos // SUBLANES, 1.0, 0.0).astype(BF16)
        for c0 in range(0, h_s.shape[1], tf):
            hp_s[halo:, c0:c0 + tf] = jnp.dot(perm, h_s[:, c0:c0 + tf],
                                              preferred_element_type=F32).astype(BF16)
        for part in range(n_parts):
            up_dot_part(0, part)
        up_fix(0)

    for parity in (0, 1):
        @pl.when((f >= 1) & (f < n_up) & (f % 2 == parity))
        def _():
            for part in range(n_parts):
                up_dot_part(parity, part)
                conv_act(1 - parity, f - 1, part, n_parts)
            up_fix(parity)

    @pl.when(f == n_up)
    def _():
        conv_act((n_up - 1) % 2, n_up - 1)

    @pl.when(f >= n_up)
    def _():
        acc = jnp.dot(act_s[0], wd_ref[0:tf, :], preferred_element_type=F32)
        for c in range(1, n_up):
            acc = acc + jnp.dot(act_s[c], wd_ref[c * tf:(c + 1) * tf, :], preferred_element_type=F32)
        for c in range(op_s.shape[0]):
            op_s[c] = acc[:, c * LANES:(c + 1) * LANES]
        for a in range(SUBLANES):
            rows = slice(a * span, (a + 1) * span)
            for c in range(op_s.shape[0]):
                cols = slice(c * LANES, (c + 1) * LANES)
                o_ref[rows, cols] = xres_ref[rows, cols] + op_s[c, pl.ds(a, span, stride=SUBLANES), :]


def _conv_glu(x2d, nw, w_up, conv_w, conv_b, w_down, layer, seq_len):
    t, d = x2d.shape
    tm, tf, tn = ROW_TILE, COL_TILE, COL_TILE
    n_up = D_FF // tf
    n_down = d // tn
    halo_blocks = tm // FFN_HALO
    up = lambda f: jnp.minimum(f, n_up - 1)
    cv = lambda f: jnp.clip(f - 1, 0, n_up - 1)
    down = lambda f: jnp.maximum(f - n_up, 0)
    return pl.pallas_call(
        functools.partial(_conv_glu_kernel, tiles_per_seq=seq_len // tm, n_up=n_up),
        grid=(t // tm, n_up + n_down),
        in_specs=[
            pl.BlockSpec((tm, d), lambda i, f: (i, 0)),
            pl.BlockSpec((FFN_HALO, d), lambda i, f: (jnp.maximum(i * halo_blocks - 1, 0), 0)),
            pl.BlockSpec((tm, tn), lambda i, f: (i, down(f))),
            pl.BlockSpec((1, d), lambda i, f: (0, 0)),
            pl.BlockSpec((None, d, tf), lambda i, f: (layer, 0, up(f))),
            pl.BlockSpec((None, d, tf), lambda i, f: (layer, 0, n_up + up(f))),
            pl.BlockSpec((FFN_CONV, tf), lambda i, f: (0, cv(f))),
            pl.BlockSpec((FFN_CONV, tf), lambda i, f: (0, n_up + cv(f))),
            pl.BlockSpec((1, tf), lambda i, f: (0, cv(f))),
            pl.BlockSpec((1, tf), lambda i, f: (0, n_up + cv(f))),
            pl.BlockSpec((None, D_FF, tn), lambda i, f: (layer, 0, down(f))),
        ],
        out_specs=pl.BlockSpec((tm, tn), lambda i, f: (i, down(f))),
        out_shape=jax.ShapeDtypeStruct((t, d), F32),
        scratch_shapes=[
            pltpu.VMEM((tm, d), BF16),
            pltpu.VMEM((FFN_HALO + tm, d), BF16),
            pltpu.VMEM((FFN_HALO + tm, tf), F32),
            pltpu.VMEM((FFN_HALO + tm, tf), F32),
            pltpu.VMEM((FFN_HALO + tm, tf), F32),
            pltpu.VMEM((FFN_HALO + tm, tf), F32),
            pltpu.VMEM((n_up, tm, tf), BF16),
            pltpu.VMEM((tn // LANES, tm, LANES), F32),
        ],
        compiler_params=pltpu.CompilerParams(
            dimension_semantics=("parallel", "arbitrary"), vmem_limit_bytes=VMEM_LIMIT),
        name="conv_glu",
    )(x2d, x2d, x2d, nw, w_up, w_up, conv_w, conv_w, conv_b, conv_b, w_down)


def _pad_lanes(v):
    return jnp.pad(v.reshape(1, -1), ((0, 0), (0, LANES - v.shape[-1])))


def _cast_kernel(w_ref, o_ref):
    o_ref[...] = w_ref[...].astype(BF16)


def _to_bf16(w):
    depth, k, n = w.shape
    return pl.pallas_call(
        _cast_kernel,
        grid=(depth, k // CAST_ROWS),
        in_specs=[pl.BlockSpec((None, CAST_ROWS, n), lambda l, r: (l, r, 0))],
        out_specs=pl.BlockSpec((None, CAST_ROWS, n), lambda l, r: (l, r, 0)),
        out_shape=jax.ShapeDtypeStruct(w.shape, BF16),
        compiler_params=pltpu.CompilerParams(
            dimension_semantics=("parallel", "parallel"), vmem_limit_bytes=VMEM_LIMIT),
        name="cast_bf16",
    )(w)


def _layer(x2d, batch, seq, layer, w_in_b, w_out_b, w_up_b, w_down_b, norm1_w, w_in, q_norm_w, k_norm_w,
           ssm_conv_w, ssm_conv_b, dt_bias, a_log, d_skip, ssm_norm_w, norm2_w, ffn_conv_w, ffn_conv_b):
    w_dt = jnp.pad(w_in[:, PROJ_MAIN:], ((0, 0), (0, LANES - SSM_HEADS))).astype(BF16)
    qkv, zxbc, k_mean, dt_raw = _in_proj(x2d, norm1_w.reshape(1, -1), w_in_b, layer, w_dt,
                                         q_norm_w.reshape(1, -1), k_norm_w.reshape(1, -1))

    attn = _moba(qkv.reshape(3 * ATTN_HEADS, batch, seq, ATTN_HEAD_DIM),
                 k_mean.reshape(batch, seq // MOBA_BLOCK, ATTN_WIDTH))
    ssm = _ssd(zxbc.reshape(batch, seq, -1), dt_raw.reshape(batch, seq, LANES), ssm_conv_w,
               ssm_conv_b.reshape(1, -1), _pad_lanes(dt_bias), _pad_lanes(a_log),
               jnp.repeat(d_skip, SSM_HEAD_DIM).reshape(1, -1), ssm_norm_w.reshape(1, -1))

    x1 = _out_proj(x2d, attn.reshape(ATTN_HEADS, -1, ATTN_HEAD_DIM), ssm.reshape(-1, SSM_WIDTH), w_out_b, layer)
    return _conv_glu(x1, norm2_w.reshape(1, -1), w_up_b, ffn_conv_w, ffn_conv_b.reshape(1, -1), w_down_b,
                     layer, seq)


def kernel(x, norm1_w, w_in, q_norm_w, k_norm_w, ssm_conv_w, ssm_conv_b, dt_bias, a_log, d_skip, ssm_norm_w,
           w_out, norm2_w, w_up, ffn_conv_w, ffn_conv_b, w_down):
    batch, seq, d = x.shape
    x2d = x.reshape(batch * seq, d)
    w_in_b, w_out_b, w_up_b, w_down_b = _to_bf16(w_in), _to_bf16(w_out), _to_bf16(w_up), _to_bf16(w_down)
    for i in range(norm1_w.shape[0]):
        x2d = _layer(x2d, batch, seq, i, w_in_b, w_out_b, w_up_b, w_down_b, norm1_w[i], w_in[i], q_norm_w[i],
                     k_norm_w[i], ssm_conv_w[i], ssm_conv_b[i], dt_bias[i], a_log[i], d_skip[i], ssm_norm_w[i],
                     norm2_w[i], ffn_conv_w[i], ffn_conv_b[i])
    return x2d.reshape(batch, seq, d)
```

```python
import functools

import jax
import jax.numpy as jnp
from jax import lax
from jax.experimental import pallas as pl
from jax.experimental.pallas import tpu as pltpu

F32 = jnp.float32
BF16 = jnp.bfloat16

D_MODEL = 2048
ATTN_WIDTH = 1024
ATTN_HEAD_DIM = 128
ATTN_HEADS = ATTN_WIDTH // ATTN_HEAD_DIM
MOBA_BLOCK = 256
MOBA_TOPK = 3
SSM_WIDTH = 1024
SSM_HEAD_DIM = 64
SSM_HEADS = SSM_WIDTH // SSM_HEAD_DIM
SSM_GROUPS = 2
SSM_HEADS_PER_GROUP = SSM_HEADS // SSM_GROUPS
SSM_GROUP_WIDTH = SSM_WIDTH // SSM_GROUPS
SSM_STATE = 128
SSM_CONV = 4
XBC_WIDTH = SSM_WIDTH + 2 * SSM_GROUPS * SSM_STATE
PROJ_MAIN = 3 * ATTN_WIDTH + SSM_WIDTH + XBC_WIDTH
D_FF = 5632
FFN_CONV = 3
EPS = 1e-6
LOG2_E = 1.4426950408889634

LANES = 128
MXU_COLS = 256
SUBLANES = 8
VMEM_LIMIT = 56 * 1024 * 1024

SSD_CHUNK = 256
ROW_TILE = 512
IN_ROW_TILE = 256
COL_TILE = 512
FFN_HALO = 16
NORM_ROWS = 64
CONV_ROWS = 64
CAST_ROWS = 256

_NT = (((1,), (1,)), ((), ()))


def _rms(x, w):
    return x * lax.rsqrt(jnp.mean(x * x, axis=-1, keepdims=True) + EPS) * w


def _norm_rows(x_ref, nw_ref, h_ref, dst_off, n_rows):
    def body(c, carry):
        r = pl.multiple_of(c * NORM_ROWS, NORM_ROWS)
        h_ref[pl.ds(dst_off + r, NORM_ROWS), :] = _rms(x_ref[pl.ds(r, NORM_ROWS), :], nw_ref[...]).astype(BF16)
        return carry
    lax.fori_loop(0, n_rows // NORM_ROWS, body, 0)


def _in_proj_kernel(x_ref, nw_ref, w_ref, wdt_ref, qw_ref, kw_ref, qkv_ref, rest_ref, km_ref, dt_ref, h_ref):
    _norm_rows(x_ref, nw_ref, h_ref, 0, x_ref.shape[0])
    dt_ref[...] = jnp.dot(h_ref[...], wdt_ref[...], preferred_element_type=F32)
    q_scale = ATTN_HEAD_DIM ** -0.5 * LOG2_E
    for c0 in range(0, PROJ_MAIN, COL_TILE):
        acc = jnp.dot(h_ref[...], w_ref[:, c0:c0 + COL_TILE], preferred_element_type=F32)
        if c0 >= 3 * ATTN_WIDTH:
            rest_ref[:, c0 - 3 * ATTN_WIDTH:c0 - 3 * ATTN_WIDTH + COL_TILE] = acc
        else:
            for h0 in range(0, COL_TILE, ATTN_HEAD_DIM):
                head = acc[:, h0:h0 + ATTN_HEAD_DIM]
                slot = (c0 + h0) // ATTN_HEAD_DIM
                if c0 >= 2 * ATTN_WIDTH:
                    qkv_ref[slot] = head.astype(BF16)
                elif c0 < ATTN_WIDTH:
                    qkv_ref[slot] = (_rms(head, qw_ref[...]) * q_scale).astype(BF16)
                else:
                    kn = _rms(head, kw_ref[...])
                    qkv_ref[slot] = kn.astype(BF16)
                    kcols = slice(c0 + h0 - ATTN_WIDTH, c0 + h0 - ATTN_WIDTH + ATTN_HEAD_DIM)
                    km_ref[:, kcols] = jnp.mean(kn, axis=0, keepdims=True)


def _in_proj(x2d, nw, w_all, layer, w_dt, qw, kw):
    t, d = x2d.shape
    tm = MOBA_BLOCK
    rest = PROJ_MAIN - 3 * ATTN_WIDTH
    n_slots = 3 * ATTN_HEADS
    return pl.pallas_call(
        _in_proj_kernel,
        grid=(t // tm,),
        in_specs=[
            pl.BlockSpec((tm, d), lambda i: (i, 0)),
            pl.BlockSpec((1, d), lambda i: (0, 0)),
            pl.BlockSpec((None,) + w_all.shape[1:], lambda i: (layer, 0, 0), pipeline_mode=pl.Buffered(1)),
            pl.BlockSpec((None, d, LANES), lambda i: (layer, 0, 0)),
            pl.BlockSpec((1, ATTN_HEAD_DIM), lambda i: (0, 0)),
            pl.BlockSpec((1, ATTN_HEAD_DIM), lambda i: (0, 0)),
        ],
        out_specs=[
            pl.BlockSpec((n_slots, tm, ATTN_HEAD_DIM), lambda i: (0, i, 0)),
            pl.BlockSpec((tm, rest), lambda i: (i, 0)),
            pl.BlockSpec((None, 1, ATTN_WIDTH), lambda i: (i, 0, 0)),
            pl.BlockSpec((tm, LANES), lambda i: (i, 0)),
        ],
        out_shape=[jax.ShapeDtypeStruct((n_slots, t, ATTN_HEAD_DIM), BF16), jax.ShapeDtypeStruct((t, rest), F32),
                   jax.ShapeDtypeStruct((t // tm, 1, ATTN_WIDTH), F32), jax.ShapeDtypeStruct((t, LANES), F32)],
        scratch_shapes=[pltpu.VMEM((tm, d), BF16)],
        compiler_params=pltpu.CompilerParams(
            dimension_semantics=("parallel",), vmem_limit_bytes=VMEM_LIMIT),
        name="in_proj",
    )(x2d, nw, w_all, w_dt, qw, kw)


def _pair_schedule(nb, width):
    remaining = {i: list(range(i)) for i in range(1, nb)}
    qi, kj = [], []
    while any(remaining.values()):
        live = sorted((i for i in remaining if remaining[i]), key=lambda i: -len(remaining[i]))
        if len(live) < width:
            return None
        for i in live[:width]:
            qi.append(i)
            kj.append(remaining[i].pop())
    return qi, kj


def _moba_schedule(nb):
    for width in (4, 2, 1):
        sched = _pair_schedule(nb, width)
        if sched is not None and width <= nb - 1:
            qi, kj = sched
            if (len(qi) // width) % 2:
                qi = qi + list(range(1, width + 1))
                kj = kj + [nb - 1] * width
            return width, (qi, kj)
    raise ValueError(f"no MoBA pair schedule for {nb} blocks")


def _moba_kernel(qi_ref, kj_ref, q_ref, k_ref, v_ref, km_ref, o_ref,
                 vt_s, km3_s, sel_s, m_s, l_s, acc_s,
                 sc_a, sc_b, p_a, p_b, al_a, al_b, *, nb, width, n_groups):
    blk = MOBA_BLOCK
    for j in range(nb):
        vt_s[j] = v_ref[j * blk:(j + 1) * blk, :].astype(F32).T.astype(BF16)

    def block(ref, i):
        return ref[pl.ds(pl.multiple_of(i * blk, blk), blk), :]

    key_i = lax.broadcasted_iota(jnp.int32, (blk, blk), 0)
    qry_i = lax.broadcasted_iota(jnp.int32, (blk, blk), 1)
    bid = lax.broadcasted_iota(jnp.int32, (nb, blk), 0)

    km = km_ref[...]
    km_hi = km.astype(BF16)
    km_r = km - km_hi.astype(F32)
    km_mid = km_r.astype(BF16)
    km3_s[...] = jnp.concatenate([km_hi, km_mid, (km_r - km_mid.astype(F32)).astype(BF16)], axis=0)

    def init_group(t, carry):
        ids = [t * width + u for u in range(width)]
        gate3 = [lax.dot_general(km3_s[...], block(q_ref, i), _NT, preferred_element_type=F32) for i in ids]
        owns = [lax.dot_general(block(k_ref, i), block(q_ref, i), _NT, preferred_element_type=F32) for i in ids]
        for i, g3 in zip(ids, gate3):
            gate = (g3[0:nb] + g3[nb:2 * nb]) + g3[2 * nb:3 * nb]
            past = bid < i
            g = jnp.where(past, gate, -jnp.inf)
            rank = jnp.zeros((nb, blk), jnp.int32)
            for jp in range(nb):
                row = g[jp:jp + 1, :]
                rank = rank + jnp.where(row > g, 1, jnp.where(row == g, jnp.where(bid > jp, 1, 0), 0))
            sel = jnp.where(past, jnp.where(rank < MOBA_TOPK, 1.0, 0.0), 0.0)
            for jp in range(nb):
                sel_s[i * nb + jp] = sel[jp:jp + 1, :]
        for i, s in zip(ids, owns):
            s = jnp.where(key_i <= qry_i, s, -jnp.inf)
            m = jnp.max(s, axis=0, keepdims=True)
            p = jnp.exp2(s - m)
            m_s[i] = m
            l_s[i] = jnp.sum(p, axis=0, keepdims=True)
            acc_s[i] = jnp.dot(vt_s[i], p.astype(BF16), preferred_element_type=F32)
        return carry

    lax.fori_loop(0, nb // width, init_group, 0)

    def group_pairs(g):
        return [(qi_ref[g * width + u], kj_ref[g * width + u]) for u in range(width)]

    def score_group(g, sc):
        for u, (i, j) in enumerate(group_pairs(g)):
            sc[u] = lax.dot_general(block(k_ref, j), block(q_ref, i), _NT, preferred_element_type=F32)

    def softmax_group(g, sc, p_buf, al_buf):
        pairs = group_pairs(g)
        m_old = [m_s[i] for i, _ in pairs]
        upd = []
        for u, ((i, j), m0) in enumerate(zip(pairs, m_old)):
            s = jnp.where(sel_s[i * nb + j] > 0.0, sc[u], -jnp.inf)
            m1 = jnp.maximum(m0, jnp.max(s, axis=0, keepdims=True))
            p = jnp.exp2(s - m1)
            p_buf[u] = p.astype(BF16)
            alpha = jnp.exp2(m0 - m1)
            al_buf[u] = alpha
            upd.append((i, m1, alpha, jnp.sum(p, axis=0, keepdims=True)))
        for i, m1, alpha, psum in upd:
            m_s[i] = m1
            l_s[i] = alpha * l_s[i] + psum

    def pv_group(g, p_buf, al_buf):
        pairs = group_pairs(g)
        pvs = [jnp.dot(vt_s[j], p_buf[u], preferred_element_type=F32) for u, (_, j) in enumerate(pairs)]
        for u, ((i, _), pv) in enumerate(zip(pairs, pvs)):
            acc_s[i] = al_buf[u] * acc_s[i] + pv

    def step(g, sc_cur, sc_next, p_cur, al_cur, p_prev, al_prev):
        pv_group(jnp.maximum(g - 1, 0), p_prev, al_prev)
        score_group(jnp.minimum(g + 1, n_groups - 1), sc_next)
        softmax_group(g, sc_cur, p_cur, al_cur)

    p_b[...] = jnp.zeros(p_b.shape, BF16)
    al_b[...] = jnp.ones(al_b.shape, F32)
    score_group(0, sc_a)

    def two_steps(t, carry):
        step(2 * t, sc_a, sc_b, p_a, al_a, p_b, al_b)
        step(2 * t + 1, sc_b, sc_a, p_b, al_b, p_a, al_a)
        return carry

    lax.fori_loop(0, n_groups // 2, two_steps, 0)
    pv_group(n_groups - 1, p_b, al_b)

    def finish_two(t, carry):
        for i in (2 * t, 2 * t + 1):
            o_ref[pl.ds(pl.multiple_of(i * blk, blk), blk), :] = (acc_s[i] / l_s[i]).T.astype(o_ref.dtype)
        return carry

    lax.fori_loop(0, nb // 2, finish_two, 0)


def _moba(qkv4, km3):
    _, b, s, _ = qkv4.shape
    nb = s // MOBA_BLOCK
    assert nb % 2 == 0
    dh = ATTN_HEAD_DIM
    blk = MOBA_BLOCK
    width, (qi, kj) = _moba_schedule(nb)
    assert nb % width == 0 and (len(qi) // width) % 2 == 0
    smem = pl.BlockSpec(memory_space=pltpu.SMEM)
    return pl.pallas_call(
        functools.partial(_moba_kernel, nb=nb, width=width, n_groups=len(qi) // width),
        grid=(b, ATTN_HEADS),
        in_specs=[
            smem,
            smem,
            pl.BlockSpec((None, None, s, dh), lambda bi, h: (h, bi, 0, 0)),
            pl.BlockSpec((None, None, s, dh), lambda bi, h: (ATTN_HEADS + h, bi, 0, 0)),
            pl.BlockSpec((None, None, s, dh), lambda bi, h: (2 * ATTN_HEADS + h, bi, 0, 0)),
            pl.BlockSpec((None, nb, dh), lambda bi, h: (bi, 0, h)),
        ],
        out_specs=pl.BlockSpec((None, None, s, dh), lambda bi, h: (h, bi, 0, 0)),
        out_shape=jax.ShapeDtypeStruct((ATTN_HEADS, b, s, dh), BF16),
        scratch_shapes=[
            pltpu.VMEM((nb, dh, blk), BF16),
            pltpu.VMEM((3 * nb, dh), BF16),
            pltpu.VMEM((nb * nb, 1, blk), F32),
            pltpu.VMEM((nb, 1, blk), F32),
            pltpu.VMEM((nb, 1, blk), F32),
            pltpu.VMEM((nb, dh, blk), F32),
            pltpu.VMEM((width, blk, blk), F32),
            pltpu.VMEM((width, blk, blk), F32),
            pltpu.VMEM((width, blk, blk), BF16),
            pltpu.VMEM((width, blk, blk), BF16),
            pltpu.VMEM((width, 1, blk), F32),
            pltpu.VMEM((width, 1, blk), F32),
        ],
        compiler_params=pltpu.CompilerParams(
            dimension_semantics=("parallel", "parallel"), vmem_limit_bytes=VMEM_LIMIT),
        name="moba",
    )(jnp.asarray(qi, jnp.int32), jnp.asarray(kj, jnp.int32), qkv4, qkv4, qkv4, km3)


def _pair_cols(arr, i0):
    rows = arr.shape[0]
    lo = lax.broadcasted_iota(jnp.int32, (rows, LANES), 1) < SSM_HEAD_DIM
    a0 = jnp.broadcast_to(arr[:, i0:i0 + 1], (rows, LANES))
    a1 = jnp.broadcast_to(arr[:, i0 + 1:i0 + 2], (rows, LANES))
    return jnp.where(lo, a0, a1)


def _ssd_kernel(xs_ref, bc_ref, z_ref, dt_ref, cw_ref, cb_ref, dtb_ref, alog_ref, dsk_ref, nw_ref, o_ref,
                ext_s, xbc_s, ht_s, y_s, wx_s, dec_s):
    L = xs_ref.shape[0]
    pad = SUBLANES

    @pl.when(pl.program_id(1) == 0)
    def _():
        ext_s[0:pad, :] = jnp.zeros((pad, XBC_WIDTH), F32)
        ht_s[...] = jnp.zeros(ht_s.shape, F32)

    ext_s[pad:pad + L, 0:SSM_WIDTH] = xs_ref[...]
    ext_s[pad:pad + L, SSM_WIDTH:XBC_WIDTH] = bc_ref[...]
    for cblk in range(XBC_WIDTH // LANES):
        cols = slice(cblk * LANES, (cblk + 1) * LANES)
        conv = cb_ref[:, cols] + cw_ref[0:1, cols] * ext_s[pad - 3:pad - 3 + L, cols]
        for j in range(1, SSM_CONV):
            conv = conv + cw_ref[j:j + 1, cols] * ext_s[pad - 3 + j:pad - 3 + j + L, cols]
        xbc_s[:, cols] = conv * jax.nn.sigmoid(conv)
    ext_s[0:pad, :] = ext_s[L:L + pad, :]

    dtv = dt_ref[...] + dtb_ref[...]
    dt = jnp.maximum(dtv, 0.0) + jnp.log1p(jnp.exp(-jnp.abs(dtv)))
    la = dt * (-jnp.exp(alog_ref[...]))
    row_i = lax.broadcasted_iota(jnp.int32, (L, L), 0)
    col_i = lax.broadcasted_iota(jnp.int32, (L, L), 1)
    tril = row_i >= col_i
    acol = jnp.dot(jnp.where(tril, 1.0, 0.0), la, precision=lax.Precision.HIGHEST,
                   preferred_element_type=F32)
    arow = acol.T
    ecol = jnp.exp(acol)
    aend = acol[L - 1:L, :]
    wcol = jnp.exp(aend - acol)
    eend = jnp.exp(aend)
    lo = lax.broadcasted_iota(jnp.int32, (L, LANES), 1) < SSM_HEAD_DIM

    for g in range(SSM_GROUPS):
        b_off = SSM_WIDTH + g * SSM_STATE
        c_off = SSM_WIDTH + SSM_GROUPS * SSM_STATE + g * SSM_STATE
        bg = xbc_s[:, b_off:b_off + SSM_STATE]
        cg = xbc_s[:, c_off:c_off + SSM_STATE].astype(BF16)
        cb = lax.dot_general(cg, bg.astype(BF16), _NT, preferred_element_type=F32)
        cbm = jnp.where(tril, cb, 0.0)
        ht = ht_s[g]
        ch = jnp.dot(cg, ht.astype(BF16), preferred_element_type=F32)
        for pr in range(SSM_HEADS_PER_GROUP // 2):
            i0 = g * SSM_HEADS_PER_GROUP + 2 * pr
            lanes = slice(i0 * SSM_HEAD_DIM, i0 * SSM_HEAD_DIM + LANES)
            gl = slice(pr * LANES, (pr + 1) * LANES)
            xs_p = xbc_s[:, lanes]
            xdt = xs_p * _pair_cols(dt, i0)
            yd = None
            for hh in range(2):
                idx = i0 + hh
                seg = acol[:, idx:idx + 1] - arow[idx:idx + 1, :]
                mm = (cbm * jnp.exp(jnp.minimum(seg, 0.0))).astype(BF16)
                xh = jnp.where(lo, xdt, 0.0) if hh == 0 else jnp.where(lo, 0.0, xdt)
                part = jnp.dot(mm, xh.astype(BF16), preferred_element_type=F32)
                yd = part if yd is None else yd + part
            y_off = ch[:, gl] * _pair_cols(ecol, i0)
            y_s[:, lanes] = yd + y_off + dsk_ref[:, lanes] * xs_p
            wx_s[:, gl] = (xdt * _pair_cols(wcol, i0)).astype(BF16)
            dec_s[:, gl] = _pair_cols(eend, i0)
        ht_s[g] = ht * dec_s[...] + jnp.dot(bg.T.astype(BF16), wx_s[...], preferred_element_type=F32)

    for g in range(SSM_GROUPS):
        cols = slice(g * SSM_GROUP_WIDTH, (g + 1) * SSM_GROUP_WIDTH)
        zz = z_ref[:, cols]
        yg = y_s[:, cols] * (zz * jax.nn.sigmoid(zz))
        o_ref[:, cols] = _rms(yg, nw_ref[:, cols]).astype(o_ref.dtype)


def _ssd(proj3, dt3, conv_w, conv_b, dt_bias, a_log, d_skip, norm_w):
    b, s, _ = proj3.shape
    L = SSD_CHUNK
    full = lambda shape: pl.BlockSpec(shape, lambda bi, c: (0,) * len(shape))
    return pl.pallas_call(
        _ssd_kernel,
        grid=(b, s // L),
        in_specs=[
            pl.BlockSpec((None, L, SSM_WIDTH), lambda bi, c: (bi, c, 1)),
            pl.BlockSpec((None, L, XBC_WIDTH - SSM_WIDTH),
                         lambda bi, c: (bi, c, 2 * SSM_WIDTH // (XBC_WIDTH - SSM_WIDTH))),
            pl.BlockSpec((None, L, SSM_WIDTH), lambda bi, c: (bi, c, 0)),
            pl.BlockSpec((None, L, LANES), lambda bi, c: (bi, c, 0)),
            full((SSM_CONV, XBC_WIDTH)),
            full((1, XBC_WIDTH)),
            full((1, LANES)),
            full((1, LANES)),
            full((1, SSM_WIDTH)),
            full((1, SSM_WIDTH)),
        ],
        out_specs=pl.BlockSpec((None, L, SSM_WIDTH), lambda bi, c: (bi, c, 0)),
        out_shape=jax.ShapeDtypeStruct((b, s, SSM_WIDTH), BF16),
        scratch_shapes=[
            pltpu.VMEM((L + 2 * SUBLANES, XBC_WIDTH), F32),
            pltpu.VMEM((L, XBC_WIDTH), F32),
            pltpu.VMEM((SSM_GROUPS, SSM_STATE, SSM_GROUP_WIDTH), F32),
            pltpu.VMEM((L, SSM_WIDTH), F32),
            pltpu.VMEM((L, SSM_GROUP_WIDTH), BF16),
            pltpu.VMEM((1, SSM_GROUP_WIDTH), F32),
        ],
        compiler_params=pltpu.CompilerParams(
            dimension_semantics=("parallel", "arbitrary"), vmem_limit_bytes=VMEM_LIMIT),
        name="ssd",
    )(proj3, proj3, proj3, dt3, conv_w, conv_b, dt_bias, a_log, d_skip, norm_w)


def _out_proj_kernel(x_ref, a_ref, s_ref, wa_ref, ws_ref, o_ref):
    attn = jnp.concatenate([a_ref[h] for h in range(a_ref.shape[0])], axis=1)
    o_ref[...] = (x_ref[...]
                  + jnp.dot(attn, wa_ref[...], preferred_element_type=F32)
                  + jnp.dot(s_ref[...], ws_ref[...], preferred_element_type=F32))


def _out_proj(x2d, attn2d, ssm2d, w_out, layer):
    t, d = x2d.shape
    return pl.pallas_call(
        _out_proj_kernel,
        grid=(t // ROW_TILE,),
        in_specs=[
            pl.BlockSpec((ROW_TILE, d), lambda i: (i, 0)),
            pl.BlockSpec((ATTN_HEADS, ROW_TILE, ATTN_HEAD_DIM), lambda i: (0, i, 0)),
            pl.BlockSpec((ROW_TILE, SSM_WIDTH), lambda i: (i, 0)),
            pl.BlockSpec((None, ATTN_WIDTH, d), lambda i: (layer, 0, 0)),
            pl.BlockSpec((None, SSM_WIDTH, d), lambda i: (layer, ATTN_WIDTH // SSM_WIDTH, 0)),
        ],
        out_specs=pl.BlockSpec((ROW_TILE, d), lambda i: (i, 0)),
        out_shape=jax.ShapeDtypeStruct((t, d), F32),
        compiler_params=pltpu.CompilerParams(
            dimension_semantics=("parallel",), vmem_limit_bytes=VMEM_LIMIT),
        name="out_proj",
    )(x2d, attn2d, ssm2d, w_out, w_out)


def _conv_glu_kernel(x_ref, halo_ref, xres_ref, nw_ref, wg_ref, wv_ref, cwg_ref, cwv_ref, cbg_ref, cbv_ref,
                     wd_ref, o_ref, h_s, hp_s, ug0_s, uv0_s, ug1_s, uv1_s, act_s, op_s, *, tiles_per_seq, n_up):
    tm = x_ref.shape[0]
    tf = wg_ref.shape[1]
    span = tm // SUBLANES
    halo = FFN_HALO
    i = pl.program_id(0)
    f = pl.program_id(1)

    u_bufs = ((ug0_s, uv0_s), (ug1_s, uv1_s))

    n_parts = 2 * (tf // MXU_COLS)

    def up_dot_part(slot, part):
        u_s, w_ref = ((u_bufs[slot][0], wg_ref), (u_bufs[slot][1], wv_ref))[part // (tf // MXU_COLS)]
        cols = slice((part % (tf // MXU_COLS)) * MXU_COLS, (part % (tf // MXU_COLS) + 1) * MXU_COLS)
        u_s[:, cols] = jnp.dot(hp_s[...], w_ref[:, cols], preferred_element_type=F32)

    def up_fix(slot):
        for u_s in u_bufs[slot]:
            before = [jnp.concatenate([u_s[halo - d:halo - d + 1, :],
                                       u_s[halo + tm - d * SUBLANES:halo + tm - d * SUBLANES + SUBLANES - 1, :]],
                                      axis=0) for d in (1, 2)]
            u_s[halo - SUBLANES:halo, :] = before[0]
            u_s[halo - 2 * SUBLANES:halo - SUBLANES, :] = before[1]

    def conv(u_s, cw_ref, cb_ref, r0, cols):
        out = cb_ref[:, cols]
        for j in range(FFN_CONV):
            first = halo - (FFN_CONV - 1 - j) * SUBLANES + r0
            out = out + cw_ref[j:j + 1, cols] * u_s[first:first + CONV_ROWS, cols]
        return out

    def conv_act(slot, chunk, part=0, parts=1):
        ug_s, uv_s = u_bufs[slot]
        for c0 in range(part * (tf // parts), (part + 1) * (tf // parts), LANES):
            cols = slice(c0, c0 + LANES)
            for r0 in range(0, tm, CONV_ROWS):
                gate = conv(ug_s, cwg_ref, cbg_ref, r0, cols)
                val = conv(uv_s, cwv_ref, cbv_ref, r0, cols)
                act_s[chunk, r0:r0 + CONV_ROWS, cols] = (gate * jax.nn.sigmoid(gate) * val).astype(BF16)

    @pl.when(f == 0)
    def _():
        _norm_rows(x_ref, nw_ref, h_s, 0, tm)
        keep = jnp.where(i % tiles_per_seq == 0, 0.0, 1.0)
        hp_s[0:halo, :] = (_rms(halo_ref[...], nw_ref[...]) * keep).astype(BF16)
        pos = lax.broadcasted_iota(jnp.int32, (tm, tm), 0)
        row = lax.broadcasted_iota(jnp.int32, (tm, tm), 1)
        perm = jnp.where(row == (pos % SUBLANES) * span + pos // SUBLANES, 1.0, 0.0).astype(BF16)
        for c0 in range(0, h_s.shape[1], tf):
            hp_s[halo:, c0:c0 + tf] = jnp.dot(perm, h_s[:, c0:c0 + tf],
                                              preferred_element_type=F32).astype(BF16)
        for part in range(n_parts):
            up_dot_part(0, part)
        up_fix(0)

    for parity in (0, 1):
        @pl.when((f >= 1) & (f < n_up) & (f % 2 == parity))
        def _():
            for part in range(n_parts):
                up_dot_part(parity, part)
                conv_act(1 - parity, f - 1, part, n_parts)
            up_fix(parity)

    @pl.when(f == n_up)
    def _():
        conv_act((n_up - 1) % 2, n_up - 1)

    @pl.when(f >= n_up)
    def _():
        acc = jnp.dot(act_s[0], wd_ref[0:tf, :], preferred_element_type=F32)
        for c in range(1, n_up):
            acc = acc + jnp.dot(act_s[c], wd_ref[c * tf:(c + 1) * tf, :], preferred_element_type=F32)
        for c in range(op_s.shape[0]):
            op_s[c] = acc[:, c * LANES:(c + 1) * LANES]
        for a in range(SUBLANES):
            rows = slice(a * span, (a + 1) * span)
            for c in range(op_s.shape[0]):
                cols = slice(c * LANES, (c + 1) * LANES)
                o_ref[rows, cols] = xres_ref[rows, cols] + op_s[c, pl.ds(a, span, stride=SUBLANES), :]


def _conv_glu(x2d, nw, w_up, conv_w, conv_b, w_down, layer, seq_len):
    t, d = x2d.shape
    tm, tf, tn = ROW_TILE, COL_TILE, COL_TILE
    n_up = D_FF // tf
    n_down = d // tn
    halo_blocks = tm // FFN_HALO
    up = lambda f: jnp.minimum(f, n_up - 1)
    cv = lambda f: jnp.clip(f - 1, 0, n_up - 1)
    down = lambda f: jnp.maximum(f - n_up, 0)
    return pl.pallas_call(
        functools.partial(_conv_glu_kernel, tiles_per_seq=seq_len // tm, n_up=n_up),
        grid=(t // tm, n_up + n_down),
        in_specs=[
            pl.BlockSpec((tm, d), lambda i, f: (i, 0)),
            pl.BlockSpec((FFN_HALO, d), lambda i, f: (jnp.maximum(i * halo_blocks - 1, 0), 0)),
            pl.BlockSpec((tm, tn), lambda i, f: (i, down(f))),
            pl.BlockSpec((1, d), lambda i, f: (0, 0)),
            pl.BlockSpec((None, d, tf), lambda i, f: (layer, 0, up(f))),
            pl.BlockSpec((None, d, tf), lambda i, f: (layer, 0, n_up + up(f))),
            pl.BlockSpec((FFN_CONV, tf), lambda i, f: (0, cv(f))),
            pl.BlockSpec((FFN_CONV, tf), lambda i, f: (0, n_up + cv(f))),
            pl.BlockSpec((1, tf), lambda i, f: (0, cv(f))),
            pl.BlockSpec((1, tf), lambda i, f: (0, n_up + cv(f))),
            pl.BlockSpec((None, D_FF, tn), lambda i, f: (layer, 0, down(f))),
        ],
        out_specs=pl.BlockSpec((tm, tn), lambda i, f: (i, down(f))),
        out_shape=jax.ShapeDtypeStruct((t, d), F32),
        scratch_shapes=[
            pltpu.VMEM((tm, d), BF16),
            pltpu.VMEM((FFN_HALO + tm, d), BF16),
            pltpu.VMEM((FFN_HALO + tm, tf), F32),
            pltpu.VMEM((FFN_HALO + tm, tf), F32),
            pltpu.VMEM((FFN_HALO + tm, tf), F32),
            pltpu.VMEM((FFN_HALO + tm, tf), F32),
            pltpu.VMEM((n_up, tm, tf), BF16),
            pltpu.VMEM((tn // LANES, tm, LANES), F32),
        ],
        compiler_params=pltpu.CompilerParams(
            dimension_semantics=("parallel", "arbitrary"), vmem_limit_bytes=VMEM_LIMIT),
        name="conv_glu",
    )(x2d, x2d, x2d, nw, w_up, w_up, conv_w, conv_w, conv_b, conv_b, w_down)


def _pad_lanes(v):
    return jnp.pad(v.reshape(1, -1), ((0, 0), (0, LANES - v.shape[-1])))


def _cast_kernel(w_ref, o_ref):
    o_ref[...] = w_ref[...].astype(BF16)


def _to_bf16(w):
    depth, k, n = w.shape
    return pl.pallas_call(
        _cast_kernel,
        grid=(depth, k // CAST_ROWS),
        in_specs=[pl.BlockSpec((None, CAST_ROWS, n), lambda l, r: (l, r, 0))],
        out_specs=pl.BlockSpec((None, CAST_ROWS, n), lambda l, r: (l, r, 0)),
        out_shape=jax.ShapeDtypeStruct(w.shape, BF16),
        compiler_params=pltpu.CompilerParams(
            dimension_semantics=("parallel", "parallel"), vmem_limit_bytes=VMEM_LIMIT),
        name="cast_bf16",
    )(w)


def _cast_w_in_kernel(w_ref, o_ref, dt_ref):
    o_ref[...] = w_ref[:, :PROJ_MAIN].astype(BF16)
    dt_ref[...] = jnp.zeros(dt_ref.shape, BF16)
    dt_ref[:, 0:SSM_HEADS] = w_ref[:, PROJ_MAIN:].astype(BF16)


def _cast_w_in(w):
    depth, k, n = w.shape
    return pl.pallas_call(
        _cast_w_in_kernel,
        grid=(depth, k // CAST_ROWS),
        in_specs=[pl.BlockSpec((None, CAST_ROWS, n), lambda l, r: (l, r, 0))],
        out_specs=[pl.BlockSpec((None, CAST_ROWS, PROJ_MAIN), lambda l, r: (l, r, 0)),
                   pl.BlockSpec((None, CAST_ROWS, LANES), lambda l, r: (l, r, 0))],
        out_shape=[jax.ShapeDtypeStruct((depth, k, PROJ_MAIN), BF16), jax.ShapeDtypeStruct((depth, k, LANES), BF16)],
        compiler_params=pltpu.CompilerParams(
            dimension_semantics=("parallel", "parallel"), vmem_limit_bytes=VMEM_LIMIT),
        name="cast_w_in",
    )(w)


def _layer(x2d, batch, seq, layer, w_in_b, w_dt_b, w_out_b, w_up_b, w_down_b, norm1_w, q_norm_w, k_norm_w,
           ssm_conv_w, ssm_conv_b, dt_bias, a_log, d_skip, ssm_norm_w, norm2_w, ffn_conv_w, ffn_conv_b):
    qkv, zxbc, k_mean, dt_raw = _in_proj(x2d, norm1_w.reshape(1, -1), w_in_b, layer, w_dt_b,
                                         q_norm_w.reshape(1, -1), k_norm_w.reshape(1, -1))

    attn = _moba(qkv.reshape(3 * ATTN_HEADS, batch, seq, ATTN_HEAD_DIM),
                 k_mean.reshape(batch, seq // MOBA_BLOCK, ATTN_WIDTH))
    ssm = _ssd(zxbc.reshape(batch, seq, -1), dt_raw.reshape(batch, seq, LANES), ssm_conv_w,
               ssm_conv_b.reshape(1, -1), _pad_lanes(dt_bias), _pad_lanes(a_log),
               jnp.repeat(d_skip, SSM_HEAD_DIM).reshape(1, -1), ssm_norm_w.reshape(1, -1))

    x1 = _out_proj(x2d, attn.reshape(ATTN_HEADS, -1, ATTN_HEAD_DIM), ssm.reshape(-1, SSM_WIDTH), w_out_b, layer)
    return _conv_glu(x1, norm2_w.reshape(1, -1), w_up_b, ffn_conv_w, ffn_conv_b.reshape(1, -1), w_down_b,
                     layer, seq)


def kernel(x, norm1_w, w_in, q_norm_w, k_norm_w, ssm_conv_w, ssm_conv_b, dt_bias, a_log, d_skip, ssm_norm_w,
           w_out, norm2_w, w_up, ffn_conv_w, ffn_conv_b, w_down):
    batch, seq, d = x.shape
    x2d = x.reshape(batch * seq, d)
    (w_in_b, w_dt_b), w_out_b, w_up_b, w_down_b = _cast_w_in(w_in), _to_bf16(w_out), _to_bf16(w_up), _to_bf16(w_down)
    for i in range(norm1_w.shape[0]):
        x2d = _layer(x2d, batch, seq, i, w_in_b, w_dt_b, w_out_b, w_up_b, w_down_b, norm1_w[i], q_norm_w[i],
                     k_norm_w[i], ssm_conv_w[i], ssm_conv_b[i], dt_bias[i], a_log[i], d_skip[i], ssm_norm_w[i],
                     norm2_w[i], ffn_conv_w[i], ffn_conv_b[i])
    return x2d.reshape(batch, seq, d)
```

```python
import functools

import jax
import jax.numpy as jnp
from jax import lax
from jax.experimental import pallas as pl
from jax.experimental.pallas import tpu as pltpu

F32 = jnp.float32
BF16 = jnp.bfloat16

D_MODEL = 2048
ATTN_WIDTH = 1024
ATTN_HEAD_DIM = 128
ATTN_HEADS = ATTN_WIDTH // ATTN_HEAD_DIM
MOBA_BLOCK = 256
MOBA_TOPK = 3
SSM_WIDTH = 1024
SSM_HEAD_DIM = 64
SSM_HEADS = SSM_WIDTH // SSM_HEAD_DIM
SSM_GROUPS = 2
SSM_HEADS_PER_GROUP = SSM_HEADS // SSM_GROUPS
SSM_GROUP_WIDTH = SSM_WIDTH // SSM_GROUPS
SSM_STATE = 128
SSM_CONV = 4
XBC_WIDTH = SSM_WIDTH + 2 * SSM_GROUPS * SSM_STATE
PROJ_MAIN = 3 * ATTN_WIDTH + SSM_WIDTH + XBC_WIDTH
D_FF = 5632
FFN_CONV = 3
EPS = 1e-6
LOG2_E = 1.4426950408889634

LANES = 128
MXU_COLS = 256
SUBLANES = 8
VMEM_LIMIT = 56 * 1024 * 1024

SSD_CHUNK = 256
ROW_TILE = 512
IN_ROW_TILE = 256
COL_TILE = 512
FFN_HALO = 16
NORM_ROWS = 64
CONV_ROWS = 64
CAST_ROWS = 256

_NT = (((1,), (1,)), ((), ()))


def _rms(x, w):
    return x * lax.rsqrt(jnp.mean(x * x, axis=-1, keepdims=True) + EPS) * w


def _norm_rows(x_ref, nw_ref, h_ref, dst_off, n_rows):
    def body(c, carry):
        r = pl.multiple_of(c * NORM_ROWS, NORM_ROWS)
        h_ref[pl.ds(dst_off + r, NORM_ROWS), :] = _rms(x_ref[pl.ds(r, NORM_ROWS), :], nw_ref[...]).astype(BF16)
        return carry
    lax.fori_loop(0, n_rows // NORM_ROWS, body, 0)


def _in_proj_kernel(x_ref, nw_ref, w_ref, wdt_ref, qw_ref, kw_ref, qkv_ref, rest_ref, km_ref, dt_ref, h_ref):
    _norm_rows(x_ref, nw_ref, h_ref, 0, x_ref.shape[0])
    dt_ref[...] = jnp.dot(h_ref[...], wdt_ref[...], preferred_element_type=F32)
    q_scale = ATTN_HEAD_DIM ** -0.5 * LOG2_E
    for c0 in range(0, PROJ_MAIN, COL_TILE):
        acc = jnp.dot(h_ref[...], w_ref[:, c0:c0 + COL_TILE], preferred_element_type=F32)
        if c0 >= 3 * ATTN_WIDTH:
            rest_ref[:, c0 - 3 * ATTN_WIDTH:c0 - 3 * ATTN_WIDTH + COL_TILE] = acc
        else:
            for h0 in range(0, COL_TILE, ATTN_HEAD_DIM):
                head = acc[:, h0:h0 + ATTN_HEAD_DIM]
                slot = (c0 + h0) // ATTN_HEAD_DIM
                if c0 >= 2 * ATTN_WIDTH:
                    qkv_ref[slot] = head.astype(BF16)
                elif c0 < ATTN_WIDTH:
                    qkv_ref[slot] = (_rms(head, qw_ref[...]) * q_scale).astype(BF16)
                else:
                    kn = _rms(head, kw_ref[...])
                    qkv_ref[slot] = kn.astype(BF16)
                    kcols = slice(c0 + h0 - ATTN_WIDTH, c0 + h0 - ATTN_WIDTH + ATTN_HEAD_DIM)
                    km_ref[:, kcols] = jnp.mean(kn, axis=0, keepdims=True)


def _in_proj(x2d, nw, w_all, layer, w_dt, qw, kw):
    t, d = x2d.shape
    tm = MOBA_BLOCK
    rest = PROJ_MAIN - 3 * ATTN_WIDTH
    n_slots = 3 * ATTN_HEADS
    return pl.pallas_call(
        _in_proj_kernel,
        grid=(t // tm,),
        in_specs=[
            pl.BlockSpec((tm, d), lambda i: (i, 0)),
            pl.BlockSpec((1, d), lambda i: (0, 0)),
            pl.BlockSpec((None,) + w_all.shape[1:], lambda i: (layer, 0, 0), pipeline_mode=pl.Buffered(1)),
            pl.BlockSpec((None, d, LANES), lambda i: (layer, 0, 0)),
            pl.BlockSpec((1, ATTN_HEAD_DIM), lambda i: (0, 0)),
            pl.BlockSpec((1, ATTN_HEAD_DIM), lambda i: (0, 0)),
        ],
        out_specs=[
            pl.BlockSpec((n_slots, tm, ATTN_HEAD_DIM), lambda i: (0, i, 0)),
            pl.BlockSpec((tm, rest), lambda i: (i, 0)),
            pl.BlockSpec((None, 1, ATTN_WIDTH), lambda i: (i, 0, 0)),
            pl.BlockSpec((tm, LANES), lambda i: (i, 0)),
        ],
        out_shape=[jax.ShapeDtypeStruct((n_slots, t, ATTN_HEAD_DIM), BF16), jax.ShapeDtypeStruct((t, rest), F32),
                   jax.ShapeDtypeStruct((t // tm, 1, ATTN_WIDTH), F32), jax.ShapeDtypeStruct((t, LANES), F32)],
        scratch_shapes=[pltpu.VMEM((tm, d), BF16)],
        compiler_params=pltpu.CompilerParams(
            dimension_semantics=("parallel",), vmem_limit_bytes=VMEM_LIMIT),
        name="in_proj",
    )(x2d, nw, w_all, w_dt, qw, kw)


def _pair_schedule(nb, width):
    remaining = {i: list(range(i)) for i in range(1, nb)}
    qi, kj = [], []
    while any(remaining.values()):
        live = sorted((i for i in remaining if remaining[i]), key=lambda i: -len(remaining[i]))
        if len(live) < width:
            return None
        for i in live[:width]:
            qi.append(i)
            kj.append(remaining[i].pop())
    return qi, kj


def _moba_schedule(nb):
    for width in (4, 2, 1):
        sched = _pair_schedule(nb, width)
        if sched is not None and width <= nb - 1:
            qi, kj = sched
            if (len(qi) // width) % 2:
                qi = qi + list(range(1, width + 1))
                kj = kj + [nb - 1] * width
            return width, (qi, kj)
    raise ValueError(f"no MoBA pair schedule for {nb} blocks")


def _moba_kernel(qi_ref, kj_ref, q_ref, k_ref, v_ref, km_ref, o_ref,
                 vt_s, km3_s, sel_s, m_s, l_s, acc_s,
                 sc_a, sc_b, p_a, p_b, al_a, al_b, *, nb, width, n_groups):
    blk = MOBA_BLOCK
    for j in range(nb):
        vt_s[j] = v_ref[j * blk:(j + 1) * blk, :].astype(F32).T.astype(BF16)

    def block(ref, i):
        return ref[pl.ds(pl.multiple_of(i * blk, blk), blk), :]

    key_i = lax.broadcasted_iota(jnp.int32, (blk, blk), 0)
    qry_i = lax.broadcasted_iota(jnp.int32, (blk, blk), 1)
    bid = lax.broadcasted_iota(jnp.int32, (nb, blk), 0)

    km = km_ref[...]
    km_hi = km.astype(BF16)
    km_r = km - km_hi.astype(F32)
    km_mid = km_r.astype(BF16)
    km3_s[...] = jnp.concatenate([km_hi, km_mid, (km_r - km_mid.astype(F32)).astype(BF16)], axis=0)

    def init_group(t, carry):
        ids = [t * width + u for u in range(width)]
        gate3 = [lax.dot_general(km3_s[...], block(q_ref, i), _NT, preferred_element_type=F32) for i in ids]
        owns = [lax.dot_general(block(k_ref, i), block(q_ref, i), _NT, preferred_element_type=F32) for i in ids]
        for i, g3 in zip(ids, gate3):
            gate = (g3[0:nb] + g3[nb:2 * nb]) + g3[2 * nb:3 * nb]
            past = bid < i
            g = jnp.where(past, gate, -jnp.inf)
            rank = jnp.zeros((nb, blk), jnp.int32)
            for jp in range(nb):
                row = g[jp:jp + 1, :]
                rank = rank + jnp.where(row > g, 1, jnp.where(row == g, jnp.where(bid > jp, 1, 0), 0))
            sel = jnp.where(past, jnp.where(rank < MOBA_TOPK, 1.0, 0.0), 0.0)
            for jp in range(nb):
                sel_s[i * nb + jp] = sel[jp:jp + 1, :]
        for i, s in zip(ids, owns):
            s = jnp.where(key_i <= qry_i, s, -jnp.inf)
            m = jnp.max(s, axis=0, keepdims=True)
            p = jnp.exp2(s - m)
            m_s[i] = m
            l_s[i] = jnp.sum(p, axis=0, keepdims=True)
            acc_s[i] = jnp.dot(vt_s[i], p.astype(BF16), preferred_element_type=F32)
        return carry

    lax.fori_loop(0, nb // width, init_group, 0)

    def group_pairs(g):
        return [(qi_ref[g * width + u], kj_ref[g * width + u]) for u in range(width)]

    def score_group(g, sc):
        for u, (i, j) in enumerate(group_pairs(g)):
            sc[u] = lax.dot_general(block(k_ref, j), block(q_ref, i), _NT, preferred_element_type=F32)

    def softmax_group(g, sc, p_buf, al_buf):
        pairs = group_pairs(g)
        m_old = [m_s[i] for i, _ in pairs]
        upd = []
        for u, ((i, j), m0) in enumerate(zip(pairs, m_old)):
            s = jnp.where(sel_s[i * nb + j] > 0.0, sc[u], -jnp.inf)
            m1 = jnp.maximum(m0, jnp.max(s, axis=0, keepdims=True))
            p = jnp.exp2(s - m1)
            p_buf[u] = p.astype(BF16)
            alpha = jnp.exp2(m0 - m1)
            al_buf[u] = alpha
            upd.append((i, m1, alpha, jnp.sum(p, axis=0, keepdims=True)))
        for i, m1, alpha, psum in upd:
            m_s[i] = m1
            l_s[i] = alpha * l_s[i] + psum

    def pv_group(g, p_buf, al_buf):
        pairs = group_pairs(g)
        pvs = [jnp.dot(vt_s[j], p_buf[u], preferred_element_type=F32) for u, (_, j) in enumerate(pairs)]
        for u, ((i, _), pv) in enumerate(zip(pairs, pvs)):
            acc_s[i] = al_buf[u] * acc_s[i] + pv

    def step(g, sc_cur, sc_next, p_cur, al_cur, p_prev, al_prev):
        pv_group(jnp.maximum(g - 1, 0), p_prev, al_prev)
        score_group(jnp.minimum(g + 1, n_groups - 1), sc_next)
        softmax_group(g, sc_cur, p_cur, al_cur)

    p_b[...] = jnp.zeros(p_b.shape, BF16)
    al_b[...] = jnp.ones(al_b.shape, F32)
    score_group(0, sc_a)

    def two_steps(t, carry):
        step(2 * t, sc_a, sc_b, p_a, al_a, p_b, al_b)
        step(2 * t + 1, sc_b, sc_a, p_b, al_b, p_a, al_a)
        return carry

    lax.fori_loop(0, n_groups // 2, two_steps, 0)
    pv_group(n_groups - 1, p_b, al_b)

    def finish_two(t, carry):
        for i in (2 * t, 2 * t + 1):
            o_ref[pl.ds(pl.multiple_of(i * blk, blk), blk), :] = (acc_s[i] / l_s[i]).T.astype(o_ref.dtype)
        return carry

    lax.fori_loop(0, nb // 2, finish_two, 0)


def _moba(qkv4, km3):
    _, b, s, _ = qkv4.shape
    nb = s // MOBA_BLOCK
    assert nb % 2 == 0
    dh = ATTN_HEAD_DIM
    blk = MOBA_BLOCK
    width, (qi, kj) = _moba_schedule(nb)
    assert nb % width == 0 and (len(qi) // width) % 2 == 0
    smem = pl.BlockSpec(memory_space=pltpu.SMEM)
    return pl.pallas_call(
        functools.partial(_moba_kernel, nb=nb, width=width, n_groups=len(qi) // width),
        grid=(b, ATTN_HEADS),
        in_specs=[
            smem,
            smem,
            pl.BlockSpec((None, None, s, dh), lambda bi, h: (h, bi, 0, 0)),
            pl.BlockSpec((None, None, s, dh), lambda bi, h: (ATTN_HEADS + h, bi, 0, 0)),
            pl.BlockSpec((None, None, s, dh), lambda bi, h: (2 * ATTN_HEADS + h, bi, 0, 0)),
            pl.BlockSpec((None, nb, dh), lambda bi, h: (bi, 0, h)),
        ],
        out_specs=pl.BlockSpec((None, None, s, dh), lambda bi, h: (h, bi, 0, 0)),
        out_shape=jax.ShapeDtypeStruct((ATTN_HEADS, b, s, dh), BF16),
        scratch_shapes=[
            pltpu.VMEM((nb, dh, blk), BF16),
            pltpu.VMEM((3 * nb, dh), BF16),
            pltpu.VMEM((nb * nb, 1, blk), F32),
            pltpu.VMEM((nb, 1, blk), F32),
            pltpu.VMEM((nb, 1, blk), F32),
            pltpu.VMEM((nb, dh, blk), F32),
            pltpu.VMEM((width, blk, blk), F32),
            pltpu.VMEM((width, blk, blk), F32),
            pltpu.VMEM((width, blk, blk), BF16),
            pltpu.VMEM((width, blk, blk), BF16),
            pltpu.VMEM((width, 1, blk), F32),
            pltpu.VMEM((width, 1, blk), F32),
        ],
        compiler_params=pltpu.CompilerParams(
            dimension_semantics=("parallel", "parallel"), vmem_limit_bytes=VMEM_LIMIT),
        name="moba",
    )(jnp.asarray(qi, jnp.int32), jnp.asarray(kj, jnp.int32), qkv4, qkv4, qkv4, km3)


def _pair_cols(arr, i0):
    rows = arr.shape[0]
    lo = lax.broadcasted_iota(jnp.int32, (rows, LANES), 1) < SSM_HEAD_DIM
    a0 = jnp.broadcast_to(arr[:, i0:i0 + 1], (rows, LANES))
    a1 = jnp.broadcast_to(arr[:, i0 + 1:i0 + 2], (rows, LANES))
    return jnp.where(lo, a0, a1)


def _ssd_kernel(xs_ref, bc_ref, z_ref, dt_ref, cw_ref, cb_ref, dtb_ref, alog_ref, dsk_ref, nw_ref, o_ref,
                ext_s, xbc_s, ht_s, y_s, wx_s, dec_s):
    L = xs_ref.shape[0]
    pad = SUBLANES

    @pl.when(pl.program_id(1) == 0)
    def _():
        ext_s[0:pad, :] = jnp.zeros((pad, XBC_WIDTH), F32)
        ht_s[...] = jnp.zeros(ht_s.shape, F32)

    ext_s[pad:pad + L, 0:SSM_WIDTH] = xs_ref[...]
    ext_s[pad:pad + L, SSM_WIDTH:XBC_WIDTH] = bc_ref[...]
    for cblk in range(XBC_WIDTH // LANES):
        cols = slice(cblk * LANES, (cblk + 1) * LANES)
        conv = cb_ref[:, cols] + cw_ref[0:1, cols] * ext_s[pad - 3:pad - 3 + L, cols]
        for j in range(1, SSM_CONV):
            conv = conv + cw_ref[j:j + 1, cols] * ext_s[pad - 3 + j:pad - 3 + j + L, cols]
        xbc_s[:, cols] = conv * jax.nn.sigmoid(conv)
    ext_s[0:pad, :] = ext_s[L:L + pad, :]

    dtv = dt_ref[...] + dtb_ref[...]
    dt = jnp.maximum(dtv, 0.0) + jnp.log1p(jnp.exp(-jnp.abs(dtv)))
    la = dt * (-jnp.exp(alog_ref[...]))
    row_i = lax.broadcasted_iota(jnp.int32, (L, L), 0)
    col_i = lax.broadcasted_iota(jnp.int32, (L, L), 1)
    tril = row_i >= col_i
    acol = jnp.dot(jnp.where(tril, 1.0, 0.0), la, precision=lax.Precision.HIGHEST,
                   preferred_element_type=F32)
    arow = acol.T
    ecol = jnp.exp(acol)
    aend = acol[L - 1:L, :]
    wcol = jnp.exp(aend - acol)
    eend = jnp.exp(aend)
    lo = lax.broadcasted_iota(jnp.int32, (L, LANES), 1) < SSM_HEAD_DIM

    for g in range(SSM_GROUPS):
        b_off = SSM_WIDTH + g * SSM_STATE
        c_off = SSM_WIDTH + SSM_GROUPS * SSM_STATE + g * SSM_STATE
        bg = xbc_s[:, b_off:b_off + SSM_STATE]
        cg = xbc_s[:, c_off:c_off + SSM_STATE].astype(BF16)
        cb = lax.dot_general(cg, bg.astype(BF16), _NT, preferred_element_type=F32)
        cbm = jnp.where(tril, cb, 0.0)
        ht = ht_s[g]
        ch = jnp.dot(cg, ht.astype(BF16), preferred_element_type=F32)
        for pr in range(SSM_HEADS_PER_GROUP // 2):
            i0 = g * SSM_HEADS_PER_GROUP + 2 * pr
            lanes = slice(i0 * SSM_HEAD_DIM, i0 * SSM_HEAD_DIM + LANES)
            gl = slice(pr * LANES, (pr + 1) * LANES)
            xs_p = xbc_s[:, lanes]
            xdt = xs_p * _pair_cols(dt, i0)
            yd = None
            for hh in range(2):
                idx = i0 + hh
                seg = acol[:, idx:idx + 1] - arow[idx:idx + 1, :]
                mm = (cbm * jnp.exp(jnp.minimum(seg, 0.0))).astype(BF16)
                xh = jnp.where(lo, xdt, 0.0) if hh == 0 else jnp.where(lo, 0.0, xdt)
                part = jnp.dot(mm, xh.astype(BF16), preferred_element_type=F32)
                yd = part if yd is None else yd + part
            y_off = ch[:, gl] * _pair_cols(ecol, i0)
            y_s[:, lanes] = yd + y_off + dsk_ref[:, lanes] * xs_p
            wx_s[:, gl] = (xdt * _pair_cols(wcol, i0)).astype(BF16)
            dec_s[:, gl] = _pair_cols(eend, i0)
        ht_s[g] = ht * dec_s[...] + jnp.dot(bg.T.astype(BF16), wx_s[...], preferred_element_type=F32)

    for g in range(SSM_GROUPS):
        cols = slice(g * SSM_GROUP_WIDTH, (g + 1) * SSM_GROUP_WIDTH)
        zz = z_ref[:, cols]
        yg = y_s[:, cols] * (zz * jax.nn.sigmoid(zz))
        o_ref[:, cols] = _rms(yg, nw_ref[:, cols]).astype(o_ref.dtype)


def _ssd(proj3, dt3, conv_w, conv_b, dt_bias, a_log, d_skip, norm_w):
    b, s, _ = proj3.shape
    L = SSD_CHUNK
    full = lambda shape: pl.BlockSpec(shape, lambda bi, c: (0,) * len(shape))
    return pl.pallas_call(
        _ssd_kernel,
        grid=(b, s // L),
        in_specs=[
            pl.BlockSpec((None, L, SSM_WIDTH), lambda bi, c: (bi, c, 1)),
            pl.BlockSpec((None, L, XBC_WIDTH - SSM_WIDTH),
                         lambda bi, c: (bi, c, 2 * SSM_WIDTH // (XBC_WIDTH - SSM_WIDTH))),
            pl.BlockSpec((None, L, SSM_WIDTH), lambda bi, c: (bi, c, 0)),
            pl.BlockSpec((None, L, LANES), lambda bi, c: (bi, c, 0)),
            full((SSM_CONV, XBC_WIDTH)),
            full((1, XBC_WIDTH)),
            full((1, LANES)),
            full((1, LANES)),
            full((1, SSM_WIDTH)),
            full((1, SSM_WIDTH)),
        ],
        out_specs=pl.BlockSpec((None, L, SSM_WIDTH), lambda bi, c: (bi, c, 0)),
        out_shape=jax.ShapeDtypeStruct((b, s, SSM_WIDTH), BF16),
        scratch_shapes=[
            pltpu.VMEM((L + 2 * SUBLANES, XBC_WIDTH), F32),
            pltpu.VMEM((L, XBC_WIDTH), F32),
            pltpu.VMEM((SSM_GROUPS, SSM_STATE, SSM_GROUP_WIDTH), F32),
            pltpu.VMEM((L, SSM_WIDTH), F32),
            pltpu.VMEM((L, SSM_GROUP_WIDTH), BF16),
            pltpu.VMEM((1, SSM_GROUP_WIDTH), F32),
        ],
        compiler_params=pltpu.CompilerParams(
            dimension_semantics=("parallel", "arbitrary"), vmem_limit_bytes=VMEM_LIMIT),
        name="ssd",
    )(proj3, proj3, proj3, dt3, conv_w, conv_b, dt_bias, a_log, d_skip, norm_w)


def _out_proj_kernel(x_ref, a_ref, s_ref, wa_ref, ws_ref, o_ref):
    attn = jnp.concatenate([a_ref[h] for h in range(a_ref.shape[0])], axis=1)
    o_ref[...] = (x_ref[...]
                  + jnp.dot(attn, wa_ref[...], preferred_element_type=F32)
                  + jnp.dot(s_ref[...], ws_ref[...], preferred_element_type=F32))


def _out_proj(x2d, attn2d, ssm2d, w_out, layer):
    t, d = x2d.shape
    return pl.pallas_call(
        _out_proj_kernel,
        grid=(t // ROW_TILE,),
        in_specs=[
            pl.BlockSpec((ROW_TILE, d), lambda i: (i, 0)),
            pl.BlockSpec((ATTN_HEADS, ROW_TILE, ATTN_HEAD_DIM), lambda i: (0, i, 0)),
            pl.BlockSpec((ROW_TILE, SSM_WIDTH), lambda i: (i, 0)),
            pl.BlockSpec((None, ATTN_WIDTH, d), lambda i: (layer, 0, 0)),
            pl.BlockSpec((None, SSM_WIDTH, d), lambda i: (layer, ATTN_WIDTH // SSM_WIDTH, 0)),
        ],
        out_specs=pl.BlockSpec((ROW_TILE, d), lambda i: (i, 0)),
        out_shape=jax.ShapeDtypeStruct((t, d), F32),
        compiler_params=pltpu.CompilerParams(
            dimension_semantics=("parallel",), vmem_limit_bytes=VMEM_LIMIT),
        name="out_proj",
    )(x2d, attn2d, ssm2d, w_out, w_out)


def _conv_glu_kernel(x_ref, halo_ref, xres_ref, nw_ref, wg_ref, wv_ref, cwg_ref, cwv_ref, cbg_ref, cbv_ref,
                     wd_ref, o_ref, h_s, hp_s, ug0_s, uv0_s, ug1_s, uv1_s, act_s, op_s, *, tiles_per_seq, n_up):
    tm = x_ref.shape[0]
    tf = wg_ref.shape[1]
    span = tm // SUBLANES
    halo = FFN_HALO
    i = pl.program_id(0)
    f = pl.program_id(1)

    u_bufs = ((ug0_s, uv0_s), (ug1_s, uv1_s))

    n_parts = 2 * (tf // MXU_COLS)

    def up_dot_part(slot, part):
        u_s, w_ref = ((u_bufs[slot][0], wg_ref), (u_bufs[slot][1], wv_ref))[part // (tf // MXU_COLS)]
        cols = slice((part % (tf // MXU_COLS)) * MXU_COLS, (part % (tf // MXU_COLS) + 1) * MXU_COLS)
        u_s[:, cols] = jnp.dot(hp_s[...], w_ref[:, cols], preferred_element_type=F32)

    def up_fix(slot):
        for u_s in u_bufs[slot]:
            before = [jnp.concatenate([u_s[halo - d:halo - d + 1, :],
                                       u_s[halo + tm - d * SUBLANES:halo + tm - d * SUBLANES + SUBLANES - 1, :]],
                                      axis=0) for d in (1, 2)]
            u_s[halo - SUBLANES:halo, :] = before[0]
            u_s[halo - 2 * SUBLANES:halo - SUBLANES, :] = before[1]

    def conv(u_s, cw_ref, cb_ref, r0, cols):
        out = cb_ref[:, cols]
        for j in range(FFN_CONV):
            first = halo - (FFN_CONV - 1 - j) * SUBLANES + r0
            out = out + cw_ref[j:j + 1, cols] * u_s[first:first + CONV_ROWS, cols]
        return out

    def conv_act(slot, chunk, part=0, parts=1):
        ug_s, uv_s = u_bufs[slot]
        for c0 in range(part * (tf // parts), (part + 1) * (tf // parts), LANES):
            cols = slice(c0, c0 + LANES)
            for r0 in range(0, tm, CONV_ROWS):
                gate = conv(ug_s, cwg_ref, cbg_ref, r0, cols)
                val = conv(uv_s, cwv_ref, cbv_ref, r0, cols)
                act_s[chunk, r0:r0 + CONV_ROWS, cols] = (gate * jax.nn.sigmoid(gate) * val).astype(BF16)

    @pl.when(f == 0)
    def _():
        _norm_rows(x_ref, nw_ref, h_s, 0, tm)
        keep = jnp.where(i % tiles_per_seq == 0, 0.0, 1.0)
        hp_s[0:halo, :] = (_rms(halo_ref[...], nw_ref[...]) * keep).astype(BF16)
        pos = lax.broadcasted_iota(jnp.int32, (tm, tm), 0)
        row = lax.broadcasted_iota(jnp.int32, (tm, tm), 1)
        perm = jnp.where(row == (pos % SUBLANES) * span + pos // SUBLANES, 1.0, 0.0).astype(BF16)
        for c0 in range(0, h_s.shape[1], tf):
            hp_s[halo:, c0:c0 + tf] = jnp.dot(perm, h_s[:, c0:c0 + tf],
                                              preferred_element_type=F32).astype(BF16)
        for part in range(n_parts):
            up_dot_part(0, part)
        up_fix(0)

    for parity in (0, 1):
        @pl.when((f >= 1) & (f < n_up) & (f % 2 == parity))
        def _():
            for part in range(n_parts):
                up_dot_part(parity, part)
                conv_act(1 - parity, f - 1, part, n_parts)
            up_fix(parity)

    @pl.when(f == n_up)
    def _():
        conv_act((n_up - 1) % 2, n_up - 1)

    @pl.when(f >= n_up)
    def _():
        acc = jnp.dot(act_s[0], wd_ref[0:tf, :], preferred_element_type=F32)
        for c in range(1, n_up):
            acc = acc + jnp.dot(act_s[c], wd_ref[c * tf:(c + 1) * tf, :], preferred_element_type=F32)
        for c in range(op_s.shape[0]):
            op_s[c] = acc[:, c * LANES:(c + 1) * LANES]
        for a in range(SUBLANES):
            rows = slice(a * span, (a + 1) * span)
            for c in range(op_s.shape[0]):
                cols = slice(c * LANES, (c + 1) * LANES)
                o_ref[rows, cols] = xres_ref[rows, cols] + op_s[c, pl.ds(a, span, stride=SUBLANES), :]


def _conv_glu(x2d, nw, w_up, conv_w, conv_b, w_down, layer, seq_len):
    t, d = x2d.shape
    tm, tf, tn = ROW_TILE, COL_TILE, COL_TILE
    n_up = D_FF // tf
    n_down = d // tn
    halo_blocks = tm // FFN_HALO
    up = lambda f: jnp.minimum(f, n_up - 1)
    cv = lambda f: jnp.clip(f - 1, 0, n_up - 1)
    down = lambda f: jnp.maximum(f - n_up, 0)
    return pl.pallas_call(
        functools.partial(_conv_glu_kernel, tiles_per_seq=seq_len // tm, n_up=n_up),
        grid=(t // tm, n_up + n_down),
        in_specs=[
            pl.BlockSpec((tm, d), lambda i, f: (i, 0)),
            pl.BlockSpec((FFN_HALO, d), lambda i, f: (jnp.maximum(i * halo_blocks - 1, 0), 0)),
            pl.BlockSpec((tm, tn), lambda i, f: (i, down(f))),
            pl.BlockSpec((1, d), lambda i, f: (0, 0)),
            pl.BlockSpec((None, d, tf), lambda i, f: (layer, 0, up(f))),
            pl.BlockSpec((None, d, tf), lambda i, f: (layer, 0, n_up + up(f))),
            pl.BlockSpec((FFN_CONV, tf), lambda i, f: (0, cv(f))),
            pl.BlockSpec((FFN_CONV, tf), lambda i, f: (0, n_up + cv(f))),
            pl.BlockSpec((1, tf), lambda i, f: (0, cv(f))),
            pl.BlockSpec((1, tf), lambda i, f: (0, n_up + cv(f))),
            pl.BlockSpec((None, D_FF, tn), lambda i, f: (layer, 0, down(f))),
        ],
        out_specs=pl.BlockSpec((tm, tn), lambda i, f: (i, down(f))),
        out_shape=jax.ShapeDtypeStruct((t, d), F32),
        scratch_shapes=[
            pltpu.VMEM((tm, d), BF16),
            pltpu.VMEM((FFN_HALO + tm, d), BF16),
            pltpu.VMEM((FFN_HALO + tm, tf), F32),
            pltpu.VMEM((FFN_HALO + tm, tf), F32),
            pltpu.VMEM((FFN_HALO + tm, tf), F32),
            pltpu.VMEM((FFN_HALO + tm, tf), F32),
            pltpu.VMEM((n_up, tm, tf), BF16),
            pltpu.VMEM((tn // LANES, tm, LANES), F32),
        ],
        compiler_params=pltpu.CompilerParams(
            dimension_semantics=("parallel", "arbitrary"), vmem_limit_bytes=VMEM_LIMIT),
        name="conv_glu",
    )(x2d, x2d, x2d, nw, w_up, w_up, conv_w, conv_w, conv_b, conv_b, w_down)


def _pad_lanes(v):
    return jnp.pad(v.reshape(1, -1), ((0, 0), (0, LANES - v.shape[-1])))


def _cast_kernel(w_ref, o_ref):
    o_ref[...] = w_ref[...].astype(BF16)


def _to_bf16(w):
    depth, k, n = w.shape
    return pl.pallas_call(
        _cast_kernel,
        grid=(depth, k // CAST_ROWS),
        in_specs=[pl.BlockSpec((None, CAST_ROWS, n), lambda l, r: (l, r, 0))],
        out_specs=pl.BlockSpec((None, CAST_ROWS, n), lambda l, r: (l, r, 0)),
        out_shape=jax.ShapeDtypeStruct(w.shape, BF16),
        compiler_params=pltpu.CompilerParams(
            dimension_semantics=("parallel", "parallel"), vmem_limit_bytes=VMEM_LIMIT),
        name="cast_bf16",
    )(w)


def _layer(x2d, batch, seq, layer, w_in_b, w_dt_b, w_out_b, w_up_b, w_down_b, norm1_w, q_norm_w, k_norm_w,
           ssm_conv_w, ssm_conv_b, dt_bias, a_log, d_skip, ssm_norm_w, norm2_w, ffn_conv_w, ffn_conv_b):
    qkv, zxbc, k_mean, dt_raw = _in_proj(x2d, norm1_w.reshape(1, -1), w_in_b, layer, w_dt_b,
                                         q_norm_w.reshape(1, -1), k_norm_w.reshape(1, -1))

    attn = _moba(qkv.reshape(3 * ATTN_HEADS, batch, seq, ATTN_HEAD_DIM),
                 k_mean.reshape(batch, seq // MOBA_BLOCK, ATTN_WIDTH))
    ssm = _ssd(zxbc.reshape(batch, seq, -1), dt_raw.reshape(batch, seq, LANES), ssm_conv_w,
               ssm_conv_b.reshape(1, -1), _pad_lanes(dt_bias), _pad_lanes(a_log),
               jnp.repeat(d_skip, SSM_HEAD_DIM).reshape(1, -1), ssm_norm_w.reshape(1, -1))

    x1 = _out_proj(x2d, attn.reshape(ATTN_HEADS, -1, ATTN_HEAD_DIM), ssm.reshape(-1, SSM_WIDTH), w_out_b, layer)
    return _conv_glu(x1, norm2_w.reshape(1, -1), w_up_b, ffn_conv_w, ffn_conv_b.reshape(1, -1), w_down_b,
                     layer, seq)


def kernel(x, norm1_w, w_in, q_norm_w, k_norm_w, ssm_conv_w, ssm_conv_b, dt_bias, a_log, d_skip, ssm_norm_w,
           w_out, norm2_w, w_up, ffn_conv_w, ffn_conv_b, w_down):
    batch, seq, d = x.shape
    x2d = x.reshape(batch * seq, d)
    w_in_b = w_in[:, :, :PROJ_MAIN].astype(BF16)
    w_dt_b = jnp.pad(w_in[:, :, PROJ_MAIN:], ((0, 0), (0, 0), (0, LANES - SSM_HEADS))).astype(BF16)
    w_out_b, w_up_b, w_down_b = _to_bf16(w_out), _to_bf16(w_up), _to_bf16(w_down)
    for i in range(norm1_w.shape[0]):
        x2d = _layer(x2d, batch, seq, i, w_in_b, w_dt_b, w_out_b, w_up_b, w_down_b, norm1_w[i], q_norm_w[i],
                     k_norm_w[i], ssm_conv_w[i], ssm_conv_b[i], dt_bias[i], a_log[i], d_skip[i], ssm_norm_w[i],
                     norm2_w[i], ffn_conv_w[i], ffn_conv_b[i])
    return x2d.reshape(batch, seq, d)
```

```python
import functools

import jax
import jax.numpy as jnp
from jax import lax
from jax.experimental import pallas as pl
from jax.experimental.pallas import tpu as pltpu

F32 = jnp.float32
BF16 = jnp.bfloat16

D_MODEL = 2048
ATTN_WIDTH = 1024
ATTN_HEAD_DIM = 128
ATTN_HEADS = ATTN_WIDTH // ATTN_HEAD_DIM
MOBA_BLOCK = 256
MOBA_TOPK = 3
SSM_WIDTH = 1024
SSM_HEAD_DIM = 64
SSM_HEADS = SSM_WIDTH // SSM_HEAD_DIM
SSM_GROUPS = 2
SSM_HEADS_PER_GROUP = SSM_HEADS // SSM_GROUPS
SSM_GROUP_WIDTH = SSM_WIDTH // SSM_GROUPS
SSM_STATE = 128
SSM_CONV = 4
XBC_WIDTH = SSM_WIDTH + 2 * SSM_GROUPS * SSM_STATE
PROJ_MAIN = 3 * ATTN_WIDTH + SSM_WIDTH + XBC_WIDTH
D_FF = 5632
FFN_CONV = 3
EPS = 1e-6
LOG2_E = 1.4426950408889634

LANES = 128
MXU_COLS = 256
SUBLANES = 8
VMEM_LIMIT = 56 * 1024 * 1024

SSD_CHUNK = 256
ROW_TILE = 512
IN_ROW_TILE = 256
COL_TILE = 512
FFN_HALO = 16
NORM_ROWS = 64
CONV_ROWS = 64
CAST_ROWS = 256

_NT = (((1,), (1,)), ((), ()))


def _rms(x, w):
    return x * lax.rsqrt(jnp.mean(x * x, axis=-1, keepdims=True) + EPS) * w


def _norm_rows(x_ref, nw_ref, h_ref, dst_off, n_rows):
    def body(c, carry):
        r = pl.multiple_of(c * NORM_ROWS, NORM_ROWS)
        h_ref[pl.ds(dst_off + r, NORM_ROWS), :] = _rms(x_ref[pl.ds(r, NORM_ROWS), :], nw_ref[...]).astype(BF16)
        return carry
    lax.fori_loop(0, n_rows // NORM_ROWS, body, 0)


def _in_proj_kernel(x_ref, nw_ref, w_ref, wdt_ref, qw_ref, kw_ref, qkv_ref, rest_ref, km_ref, dt_ref, h_ref):
    _norm_rows(x_ref, nw_ref, h_ref, 0, x_ref.shape[0])
    dt_ref[...] = jnp.dot(h_ref[...], wdt_ref[...], preferred_element_type=F32)
    q_scale = ATTN_HEAD_DIM ** -0.5 * LOG2_E
    for c0 in range(0, PROJ_MAIN, COL_TILE):
        acc = jnp.dot(h_ref[...], w_ref[:, c0:c0 + COL_TILE], preferred_element_type=F32)
        if c0 >= 3 * ATTN_WIDTH:
            rest_ref[:, c0 - 3 * ATTN_WIDTH:c0 - 3 * ATTN_WIDTH + COL_TILE] = acc
        else:
            for h0 in range(0, COL_TILE, ATTN_HEAD_DIM):
                head = acc[:, h0:h0 + ATTN_HEAD_DIM]
                slot = (c0 + h0) // ATTN_HEAD_DIM
                if c0 >= 2 * ATTN_WIDTH:
                    qkv_ref[slot] = head.astype(BF16)
                elif c0 < ATTN_WIDTH:
                    qkv_ref[slot] = (_rms(head, qw_ref[...]) * q_scale).astype(BF16)
                else:
                    kn = _rms(head, kw_ref[...])
                    qkv_ref[slot] = kn.astype(BF16)
                    kcols = slice(c0 + h0 - ATTN_WIDTH, c0 + h0 - ATTN_WIDTH + ATTN_HEAD_DIM)
                    km_ref[:, kcols] = jnp.mean(kn, axis=0, keepdims=True)


def _in_proj(x2d, nw, w_all, layer, w_dt, qw, kw):
    t, d = x2d.shape
    tm = MOBA_BLOCK
    rest = PROJ_MAIN - 3 * ATTN_WIDTH
    n_slots = 3 * ATTN_HEADS
    return pl.pallas_call(
        _in_proj_kernel,
        grid=(t // tm,),
        in_specs=[
            pl.BlockSpec((tm, d), lambda i: (i, 0)),
            pl.BlockSpec((1, d), lambda i: (0, 0)),
            pl.BlockSpec((None,) + w_all.shape[1:], lambda i: (layer, 0, 0), pipeline_mode=pl.Buffered(1)),
            pl.BlockSpec((None, d, LANES), lambda i: (layer, 0, 0)),
            pl.BlockSpec((1, ATTN_HEAD_DIM), lambda i: (0, 0)),
            pl.BlockSpec((1, ATTN_HEAD_DIM), lambda i: (0, 0)),
        ],
        out_specs=[
            pl.BlockSpec((n_slots, tm, ATTN_HEAD_DIM), lambda i: (0, i, 0)),
            pl.BlockSpec((tm, rest), lambda i: (i, 0)),
            pl.BlockSpec((None, 1, ATTN_WIDTH), lambda i: (i, 0, 0)),
            pl.BlockSpec((tm, LANES), lambda i: (i, 0)),
        ],
        out_shape=[jax.ShapeDtypeStruct((n_slots, t, ATTN_HEAD_DIM), BF16), jax.ShapeDtypeStruct((t, rest), F32),
                   jax.ShapeDtypeStruct((t // tm, 1, ATTN_WIDTH), F32), jax.ShapeDtypeStruct((t, LANES), F32)],
        scratch_shapes=[pltpu.VMEM((tm, d), BF16)],
        compiler_params=pltpu.CompilerParams(
            dimension_semantics=("parallel",), vmem_limit_bytes=VMEM_LIMIT),
        name="in_proj",
    )(x2d, nw, w_all, w_dt, qw, kw)


def _pair_schedule(nb, width):
    remaining = {i: list(range(i)) for i in range(1, nb)}
    qi, kj = [], []
    while any(remaining.values()):
        live = sorted((i for i in remaining if remaining[i]), key=lambda i: -len(remaining[i]))
        if len(live) < width:
            return None
        for i in live[:width]:
            qi.append(i)
            kj.append(remaining[i].pop())
    return qi, kj


def _moba_schedule(nb):
    for width in (4, 2, 1):
        sched = _pair_schedule(nb, width)
        if sched is not None and width <= nb - 1:
            qi, kj = sched
            if (len(qi) // width) % 2:
                qi = qi + list(range(1, width + 1))
                kj = kj + [nb - 1] * width
            return width, (qi, kj)
    raise ValueError(f"no MoBA pair schedule for {nb} blocks")


def _moba_kernel(qi_ref, kj_ref, q_ref, k_ref, v_ref, km_ref, o_ref,
                 vt_s, km3_s, sel_s, m_s, l_s, acc_s,
                 sc_a, sc_b, p_a, p_b, al_a, al_b, *, nb, width, n_groups):
    blk = MOBA_BLOCK
    for j in range(nb):
        vt_s[j] = v_ref[j * blk:(j + 1) * blk, :].astype(F32).T.astype(BF16)

    def block(ref, i):
        return ref[pl.ds(pl.multiple_of(i * blk, blk), blk), :]

    key_i = lax.broadcasted_iota(jnp.int32, (blk, blk), 0)
    qry_i = lax.broadcasted_iota(jnp.int32, (blk, blk), 1)
    bid = lax.broadcasted_iota(jnp.int32, (nb, blk), 0)

    km = km_ref[...]
    km_hi = km.astype(BF16)
    km_r = km - km_hi.astype(F32)
    km_mid = km_r.astype(BF16)
    km3_s[...] = jnp.concatenate([km_hi, km_mid, (km_r - km_mid.astype(F32)).astype(BF16)], axis=0)

    def init_group(t, carry):
        ids = [t * width + u for u in range(width)]
        gate3 = [lax.dot_general(km3_s[...], block(q_ref, i), _NT, preferred_element_type=F32) for i in ids]
        owns = [lax.dot_general(block(k_ref, i), block(q_ref, i), _NT, preferred_element_type=F32) for i in ids]
        for i, g3 in zip(ids, gate3):
            gate = (g3[0:nb] + g3[nb:2 * nb]) + g3[2 * nb:3 * nb]
            past = bid < i
            g = jnp.where(past, gate, -jnp.inf)
            rank = jnp.zeros((nb, blk), jnp.int32)
            for jp in range(nb):
                row = g[jp:jp + 1, :]
                rank = rank + jnp.where(row > g, 1, jnp.where(row == g, jnp.where(bid > jp, 1, 0), 0))
            sel = jnp.where(past, jnp.where(rank < MOBA_TOPK, 1.0, 0.0), 0.0)
            for jp in range(nb):
                sel_s[i * nb + jp] = sel[jp:jp + 1, :]
        for i, s in zip(ids, owns):
            s = jnp.where(key_i <= qry_i, s, -jnp.inf)
            m = jnp.max(s, axis=0, keepdims=True)
            p = jnp.exp2(s - m)
            m_s[i] = m
            l_s[i] = jnp.sum(p, axis=0, keepdims=True)
            acc_s[i] = jnp.dot(vt_s[i], p.astype(BF16), preferred_element_type=F32)
        return carry

    lax.fori_loop(0, nb // width, init_group, 0)

    def group_pairs(g):
        return [(qi_ref[g * width + u], kj_ref[g * width + u]) for u in range(width)]

    def score_group(g, sc):
        for u, (i, j) in enumerate(group_pairs(g)):
            sc[u] = lax.dot_general(block(k_ref, j), block(q_ref, i), _NT, preferred_element_type=F32)

    def softmax_group(g, sc, p_buf, al_buf):
        pairs = group_pairs(g)
        m_old = [m_s[i] for i, _ in pairs]
        upd = []
        for u, ((i, j), m0) in enumerate(zip(pairs, m_old)):
            s = jnp.where(sel_s[i * nb + j] > 0.0, sc[u], -jnp.inf)
            m1 = jnp.maximum(m0, jnp.max(s, axis=0, keepdims=True))
            p = jnp.exp2(s - m1)
            p_buf[u] = p.astype(BF16)
            alpha = jnp.exp2(m0 - m1)
            al_buf[u] = alpha
            upd.append((i, m1, alpha, jnp.sum(p, axis=0, keepdims=True)))
        for i, m1, alpha, psum in upd:
            m_s[i] = m1
            l_s[i] = alpha * l_s[i] + psum

    def pv_group(g, p_buf, al_buf):
        pairs = group_pairs(g)
        pvs = [jnp.dot(vt_s[j], p_buf[u], preferred_element_type=F32) for u, (_, j) in enumerate(pairs)]
        for u, ((i, _), pv) in enumerate(zip(pairs, pvs)):
            acc_s[i] = al_buf[u] * acc_s[i] + pv

    def step(g, sc_cur, sc_next, p_cur, al_cur, p_prev, al_prev):
        pv_group(jnp.maximum(g - 1, 0), p_prev, al_prev)
        score_group(jnp.minimum(g + 1, n_groups - 1), sc_next)
        softmax_group(g, sc_cur, p_cur, al_cur)

    p_b[...] = jnp.zeros(p_b.shape, BF16)
    al_b[...] = jnp.ones(al_b.shape, F32)
    score_group(0, sc_a)

    def two_steps(t, carry):
        step(2 * t, sc_a, sc_b, p_a, al_a, p_b, al_b)
        step(2 * t + 1, sc_b, sc_a, p_b, al_b, p_a, al_a)
        return carry

    lax.fori_loop(0, n_groups // 2, two_steps, 0)
    pv_group(n_groups - 1, p_b, al_b)

    def finish_two(t, carry):
        for i in (2 * t, 2 * t + 1):
            o_ref[pl.ds(pl.multiple_of(i * blk, blk), blk), :] = (acc_s[i] / l_s[i]).T.astype(o_ref.dtype)
        return carry

    lax.fori_loop(0, nb // 2, finish_two, 0)


def _moba(qkv4, km3):
    _, b, s, _ = qkv4.shape
    nb = s // MOBA_BLOCK
    assert nb % 2 == 0
    dh = ATTN_HEAD_DIM
    blk = MOBA_BLOCK
    width, (qi, kj) = _moba_schedule(nb)
    assert nb % width == 0 and (len(qi) // width) % 2 == 0
    smem = pl.BlockSpec(memory_space=pltpu.SMEM)
    return pl.pallas_call(
        functools.partial(_moba_kernel, nb=nb, width=width, n_groups=len(qi) // width),
        grid=(b, ATTN_HEADS),
        in_specs=[
            smem,
            smem,
            pl.BlockSpec((None, None, s, dh), lambda bi, h: (h, bi, 0, 0)),
            pl.BlockSpec((None, None, s, dh), lambda bi, h: (ATTN_HEADS + h, bi, 0, 0)),
            pl.BlockSpec((None, None, s, dh), lambda bi, h: (2 * ATTN_HEADS + h, bi, 0, 0)),
            pl.BlockSpec((None, nb, dh), lambda bi, h: (bi, 0, h)),
        ],
        out_specs=pl.BlockSpec((None, None, s, dh), lambda bi, h: (h, bi, 0, 0)),
        out_shape=jax.ShapeDtypeStruct((ATTN_HEADS, b, s, dh), BF16),
        scratch_shapes=[
            pltpu.VMEM((nb, dh, blk), BF16),
            pltpu.VMEM((3 * nb, dh), BF16),
            pltpu.VMEM((nb * nb, 1, blk), F32),
            pltpu.VMEM((nb, 1, blk), F32),
            pltpu.VMEM((nb, 1, blk), F32),
            pltpu.VMEM((nb, dh, blk), F32),
            pltpu.VMEM((width, blk, blk), F32),
            pltpu.VMEM((width, blk, blk), F32),
            pltpu.VMEM((width, blk, blk), BF16),
            pltpu.VMEM((width, blk, blk), BF16),
            pltpu.VMEM((width, 1, blk), F32),
            pltpu.VMEM((width, 1, blk), F32),
        ],
        compiler_params=pltpu.CompilerParams(
            dimension_semantics=("parallel", "parallel"), vmem_limit_bytes=VMEM_LIMIT),
        name="moba",
    )(jnp.asarray(qi, jnp.int32), jnp.asarray(kj, jnp.int32), qkv4, qkv4, qkv4, km3)


def _pair_cols(arr, i0):
    rows = arr.shape[0]
    lo = lax.broadcasted_iota(jnp.int32, (rows, LANES), 1) < SSM_HEAD_DIM
    a0 = jnp.broadcast_to(arr[:, i0:i0 + 1], (rows, LANES))
    a1 = jnp.broadcast_to(arr[:, i0 + 1:i0 + 2], (rows, LANES))
    return jnp.where(lo, a0, a1)


def _ssd_kernel(xs_ref, bc_ref, z_ref, dt_ref, cw_ref, cb_ref, dtb_ref, alog_ref, dsk_ref, nw_ref, o_ref,
                ext_s, xbc_s, ht_s, y_s, wx_s, dec_s):
    L = xs_ref.shape[0]
    pad = SUBLANES

    @pl.when(pl.program_id(1) == 0)
    def _():
        ext_s[0:pad, :] = jnp.zeros((pad, XBC_WIDTH), F32)
        ht_s[...] = jnp.zeros(ht_s.shape, F32)

    ext_s[pad:pad + L, 0:SSM_WIDTH] = xs_ref[...]
    ext_s[pad:pad + L, SSM_WIDTH:XBC_WIDTH] = bc_ref[...]
    for cblk in range(XBC_WIDTH // LANES):
        cols = slice(cblk * LANES, (cblk + 1) * LANES)
        conv = cb_ref[:, cols] + cw_ref[0:1, cols] * ext_s[pad - 3:pad - 3 + L, cols]
        for j in range(1, SSM_CONV):
            conv = conv + cw_ref[j:j + 1, cols] * ext_s[pad - 3 + j:pad - 3 + j + L, cols]
        xbc_s[:, cols] = conv * jax.nn.sigmoid(conv)
    ext_s[0:pad, :] = ext_s[L:L + pad, :]

    dtv = dt_ref[...] + dtb_ref[...]
    dt = jnp.maximum(dtv, 0.0) + jnp.log1p(jnp.exp(-jnp.abs(dtv)))
    la = dt * (-jnp.exp(alog_ref[...]))
    row_i = lax.broadcasted_iota(jnp.int32, (L, L), 0)
    col_i = lax.broadcasted_iota(jnp.int32, (L, L), 1)
    tril = row_i >= col_i
    acol = jnp.dot(jnp.where(tril, 1.0, 0.0), la, precision=lax.Precision.HIGHEST,
                   preferred_element_type=F32)
    arow = acol.T
    ecol = jnp.exp(acol)
    aend = acol[L - 1:L, :]
    wcol = jnp.exp(aend - acol)
    eend = jnp.exp(aend)
    lo = lax.broadcasted_iota(jnp.int32, (L, LANES), 1) < SSM_HEAD_DIM

    for g in range(SSM_GROUPS):
        b_off = SSM_WIDTH + g * SSM_STATE
        c_off = SSM_WIDTH + SSM_GROUPS * SSM_STATE + g * SSM_STATE
        bg = xbc_s[:, b_off:b_off + SSM_STATE]
        cg = xbc_s[:, c_off:c_off + SSM_STATE].astype(BF16)
        cb = lax.dot_general(cg, bg.astype(BF16), _NT, preferred_element_type=F32)
        cbm = jnp.where(tril, cb, 0.0)
        ht = ht_s[g]
        ch = jnp.dot(cg, ht.astype(BF16), preferred_element_type=F32)
        for pr in range(SSM_HEADS_PER_GROUP // 2):
            i0 = g * SSM_HEADS_PER_GROUP + 2 * pr
            lanes = slice(i0 * SSM_HEAD_DIM, i0 * SSM_HEAD_DIM + LANES)
            gl = slice(pr * LANES, (pr + 1) * LANES)
            xs_p = xbc_s[:, lanes]
            xdt = xs_p * _pair_cols(dt, i0)
            yd = None
            for hh in range(2):
                idx = i0 + hh
                seg = acol[:, idx:idx + 1] - arow[idx:idx + 1, :]
                mm = (cbm * jnp.exp(jnp.minimum(seg, 0.0))).astype(BF16)
                xh = jnp.where(lo, xdt, 0.0) if hh == 0 else jnp.where(lo, 0.0, xdt)
                part = jnp.dot(mm, xh.astype(BF16), preferred_element_type=F32)
                yd = part if yd is None else yd + part
            y_off = ch[:, gl] * _pair_cols(ecol, i0)
            y_s[:, lanes] = yd + y_off + dsk_ref[:, lanes] * xs_p
            wx_s[:, gl] = (xdt * _pair_cols(wcol, i0)).astype(BF16)
            dec_s[:, gl] = _pair_cols(eend, i0)
        ht_s[g] = ht * dec_s[...] + jnp.dot(bg.T.astype(BF16), wx_s[...], preferred_element_type=F32)

    for g in range(SSM_GROUPS):
        cols = slice(g * SSM_GROUP_WIDTH, (g + 1) * SSM_GROUP_WIDTH)
        zz = z_ref[:, cols]
        yg = y_s[:, cols] * (zz * jax.nn.sigmoid(zz))
        o_ref[:, cols] = _rms(yg, nw_ref[:, cols]).astype(o_ref.dtype)


def _ssd(proj3, dt3, conv_w, conv_b, dt_bias, a_log, d_skip, norm_w):
    b, s, _ = proj3.shape
    L = SSD_CHUNK
    full = lambda shape: pl.BlockSpec(shape, lambda bi, c: (0,) * len(shape))
    return pl.pallas_call(
        _ssd_kernel,
        grid=(b, s // L),
        in_specs=[
            pl.BlockSpec((None, L, SSM_WIDTH), lambda bi, c: (bi, c, 1)),
            pl.BlockSpec((None, L, XBC_WIDTH - SSM_WIDTH),
                         lambda bi, c: (bi, c, 2 * SSM_WIDTH // (XBC_WIDTH - SSM_WIDTH))),
            pl.BlockSpec((None, L, SSM_WIDTH), lambda bi, c: (bi, c, 0)),
            pl.BlockSpec((None, L, LANES), lambda bi, c: (bi, c, 0)),
            full((SSM_CONV, XBC_WIDTH)),
            full((1, XBC_WIDTH)),
            full((1, LANES)),
            full((1, LANES)),
            full((1, SSM_WIDTH)),
            full((1, SSM_WIDTH)),
        ],
        out_specs=pl.BlockSpec((None, L, SSM_WIDTH), lambda bi, c: (bi, c, 0)),
        out_shape=jax.ShapeDtypeStruct((b, s, SSM_WIDTH), BF16),
        scratch_shapes=[
            pltpu.VMEM((L + 2 * SUBLANES, XBC_WIDTH), F32),
            pltpu.VMEM((L, XBC_WIDTH), F32),
            pltpu.VMEM((SSM_GROUPS, SSM_STATE, SSM_GROUP_WIDTH), F32),
            pltpu.VMEM((L, SSM_WIDTH), F32),
            pltpu.VMEM((L, SSM_GROUP_WIDTH), BF16),
            pltpu.VMEM((1, SSM_GROUP_WIDTH), F32),
        ],
        compiler_params=pltpu.CompilerParams(
            dimension_semantics=("parallel", "arbitrary"), vmem_limit_bytes=VMEM_LIMIT),
        name="ssd",
    )(proj3, proj3, proj3, dt3, conv_w, conv_b, dt_bias, a_log, d_skip, norm_w)


def _out_proj_kernel(x_ref, a_ref, s_ref, wa_ref, ws_ref, o_ref):
    attn = jnp.concatenate([a_ref[h] for h in range(a_ref.shape[0])], axis=1)
    o_ref[...] = (x_ref[...]
                  + jnp.dot(attn, wa_ref[...], preferred_element_type=F32)
                  + jnp.dot(s_ref[...], ws_ref[...], preferred_element_type=F32))


def _out_proj(x2d, attn2d, ssm2d, w_out, layer):
    t, d = x2d.shape
    return pl.pallas_call(
        _out_proj_kernel,
        grid=(t // ROW_TILE,),
        in_specs=[
            pl.BlockSpec((ROW_TILE, d), lambda i: (i, 0)),
            pl.BlockSpec((ATTN_HEADS, ROW_TILE, ATTN_HEAD_DIM), lambda i: (0, i, 0)),
            pl.BlockSpec((ROW_TILE, SSM_WIDTH), lambda i: (i, 0)),
            pl.BlockSpec((None, ATTN_WIDTH, d), lambda i: (layer, 0, 0)),
            pl.BlockSpec((None, SSM_WIDTH, d), lambda i: (layer, ATTN_WIDTH // SSM_WIDTH, 0)),
        ],
        out_specs=pl.BlockSpec((ROW_TILE, d), lambda i: (i, 0)),
        out_shape=jax.ShapeDtypeStruct((t, d), F32),
        compiler_params=pltpu.CompilerParams(
            dimension_semantics=("parallel",), vmem_limit_bytes=VMEM_LIMIT),
        name="out_proj",
    )(x2d, attn2d, ssm2d, w_out, w_out)


def _conv_glu_up_kernel(x_ref, halo_ref, nw_ref, wg_ref, wv_ref, cwg_ref, cwv_ref, cbg_ref, cbv_ref,
                        a_ref, h_s, hp_s, ug0_s, uv0_s, ug1_s, uv1_s, *, tiles_per_seq, n_up):
    tm = x_ref.shape[0]
    tf = wg_ref.shape[1]
    span = tm // SUBLANES
    halo = FFN_HALO
    i = pl.program_id(0)
    f = pl.program_id(1)

    u_bufs = ((ug0_s, uv0_s), (ug1_s, uv1_s))

    n_parts = 2 * (tf // MXU_COLS)

    def up_dot_part(slot, part):
        u_s, w_ref = ((u_bufs[slot][0], wg_ref), (u_bufs[slot][1], wv_ref))[part // (tf // MXU_COLS)]
        cols = slice((part % (tf // MXU_COLS)) * MXU_COLS, (part % (tf // MXU_COLS) + 1) * MXU_COLS)
        u_s[:, cols] = jnp.dot(hp_s[...], w_ref[:, cols], preferred_element_type=F32)

    def up_fix(slot):
        for u_s in u_bufs[slot]:
            before = [jnp.concatenate([u_s[halo - d:halo - d + 1, :],
                                       u_s[halo + tm - d * SUBLANES:halo + tm - d * SUBLANES + SUBLANES - 1, :]],
                                      axis=0) for d in (1, 2)]
            u_s[halo - SUBLANES:halo, :] = before[0]
            u_s[halo - 2 * SUBLANES:halo - SUBLANES, :] = before[1]

    def conv(u_s, cw_ref, cb_ref, r0, cols):
        out = cb_ref[:, cols]
        for j in range(FFN_CONV):
            first = halo - (FFN_CONV - 1 - j) * SUBLANES + r0
            out = out + cw_ref[j:j + 1, cols] * u_s[first:first + CONV_ROWS, cols]
        return out

    def conv_act(slot, part=0, parts=1):
        ug_s, uv_s = u_bufs[slot]
        for c0 in range(part * (tf // parts), (part + 1) * (tf // parts), LANES):
            cols = slice(c0, c0 + LANES)
            for r0 in range(0, tm, CONV_ROWS):
                gate = conv(ug_s, cwg_ref, cbg_ref, r0, cols)
                val = conv(uv_s, cwv_ref, cbv_ref, r0, cols)
                a_ref[r0:r0 + CONV_ROWS, cols] = (gate * jax.nn.sigmoid(gate) * val).astype(BF16)

    @pl.when(f == 0)
    def _():
        _norm_rows(x_ref, nw_ref, h_s, 0, tm)
        keep = jnp.where(i % tiles_per_seq == 0, 0.0, 1.0)
        hp_s[0:halo, :] = (_rms(halo_ref[...], nw_ref[...]) * keep).astype(BF16)
        pos = lax.broadcasted_iota(jnp.int32, (tm, tm), 0)
        row = lax.broadcasted_iota(jnp.int32, (tm, tm), 1)
        perm = jnp.where(row == (pos % SUBLANES) * span + pos // SUBLANES, 1.0, 0.0).astype(BF16)
        for c0 in range(0, h_s.shape[1], tf):
            hp_s[halo:, c0:c0 + tf] = jnp.dot(perm, h_s[:, c0:c0 + tf],
                                              preferred_element_type=F32).astype(BF16)
        for part in range(n_parts):
            up_dot_part(0, part)
        up_fix(0)

    for parity in (0, 1):
        @pl.when((f >= 1) & (f < n_up) & (f % 2 == parity))
        def _():
            for part in range(n_parts):
                up_dot_part(parity, part)
                conv_act(1 - parity, part, n_parts)
            up_fix(parity)

    @pl.when(f == n_up)
    def _():
        conv_act((n_up - 1) % 2)


def _conv_glu_down_kernel(a_ref, x_ref, wd_ref, o_ref, op_s):
    tm = x_ref.shape[0]
    span = tm // SUBLANES
    for n0 in range(0, o_ref.shape[1], COL_TILE):
        acc = jnp.dot(a_ref[...], wd_ref[:, n0:n0 + COL_TILE], preferred_element_type=F32)
        for c in range(op_s.shape[0]):
            op_s[c] = acc[:, c * LANES:(c + 1) * LANES]
        for a in range(SUBLANES):
            rows = slice(a * span, (a + 1) * span)
            for c in range(op_s.shape[0]):
                cols = slice(n0 + c * LANES, n0 + (c + 1) * LANES)
                o_ref[rows, cols] = x_ref[rows, cols] + op_s[c, pl.ds(a, span, stride=SUBLANES), :]


def _conv_glu(x2d, nw, w_up, conv_w, conv_b, w_down, layer, seq_len):
    t, d = x2d.shape
    tm, tf = ROW_TILE, COL_TILE
    n_up = D_FF // tf
    halo_blocks = tm // FFN_HALO
    up = lambda f: jnp.minimum(f, n_up - 1)
    cv = lambda f: jnp.clip(f - 1, 0, n_up - 1)
    act = pl.pallas_call(
        functools.partial(_conv_glu_up_kernel, tiles_per_seq=seq_len // tm, n_up=n_up),
        grid=(t // tm, n_up + 1),
        in_specs=[
            pl.BlockSpec((tm, d), lambda i, f: (i, 0)),
            pl.BlockSpec((FFN_HALO, d), lambda i, f: (jnp.maximum(i * halo_blocks - 1, 0), 0)),
            pl.BlockSpec((1, d), lambda i, f: (0, 0)),
            pl.BlockSpec((None, d, tf), lambda i, f: (layer, 0, up(f))),
            pl.BlockSpec((None, d, tf), lambda i, f: (layer, 0, n_up + up(f))),
            pl.BlockSpec((FFN_CONV, tf), lambda i, f: (0, cv(f))),
            pl.BlockSpec((FFN_CONV, tf), lambda i, f: (0, n_up + cv(f))),
            pl.BlockSpec((1, tf), lambda i, f: (0, cv(f))),
            pl.BlockSpec((1, tf), lambda i, f: (0, n_up + cv(f))),
        ],
        out_specs=pl.BlockSpec((tm, tf), lambda i, f: (i, cv(f))),
        out_shape=jax.ShapeDtypeStruct((t, D_FF), BF16),
        scratch_shapes=[
            pltpu.VMEM((tm, d), BF16),
            pltpu.VMEM((FFN_HALO + tm, d), BF16),
            pltpu.VMEM((FFN_HALO + tm, tf), F32),
            pltpu.VMEM((FFN_HALO + tm, tf), F32),
            pltpu.VMEM((FFN_HALO + tm, tf), F32),
            pltpu.VMEM((FFN_HALO + tm, tf), F32),
        ],
        compiler_params=pltpu.CompilerParams(
            dimension_semantics=("parallel", "arbitrary"), vmem_limit_bytes=VMEM_LIMIT),
        name="conv_glu_up",
    )(x2d, x2d, nw, w_up, w_up, conv_w, conv_w, conv_b, conv_b)
    return pl.pallas_call(
        _conv_glu_down_kernel,
        grid=(t // tm,),
        in_specs=[
            pl.BlockSpec((tm, D_FF), lambda i: (i, 0)),
            pl.BlockSpec((tm, d), lambda i: (i, 0)),
            pl.BlockSpec((None, D_FF, d), lambda i: (layer, 0, 0), pipeline_mode=pl.Buffered(1)),
        ],
        out_specs=pl.BlockSpec((tm, d), lambda i: (i, 0)),
        out_shape=jax.ShapeDtypeStruct((t, d), F32),
        scratch_shapes=[pltpu.VMEM((COL_TILE // LANES, tm, LANES), F32)],
        compiler_params=pltpu.CompilerParams(
            dimension_semantics=("parallel",), vmem_limit_bytes=VMEM_LIMIT),
        name="conv_glu_down",
    )(act, x2d, w_down)


def _pad_lanes(v):
    return jnp.pad(v.reshape(1, -1), ((0, 0), (0, LANES - v.shape[-1])))


def _cast_kernel(w_ref, o_ref):
    o_ref[...] = w_ref[...].astype(BF16)


def _to_bf16(w):
    depth, k, n = w.shape
    return pl.pallas_call(
        _cast_kernel,
        grid=(depth, k // CAST_ROWS),
        in_specs=[pl.BlockSpec((None, CAST_ROWS, n), lambda l, r: (l, r, 0))],
        out_specs=pl.BlockSpec((None, CAST_ROWS, n), lambda l, r: (l, r, 0)),
        out_shape=jax.ShapeDtypeStruct(w.shape, BF16),
        compiler_params=pltpu.CompilerParams(
            dimension_semantics=("parallel", "parallel"), vmem_limit_bytes=VMEM_LIMIT),
        name="cast_bf16",
    )(w)


def _layer(x2d, batch, seq, layer, w_in_b, w_dt_b, w_out_b, w_up_b, w_down_b, norm1_w, q_norm_w, k_norm_w,
           ssm_conv_w, ssm_conv_b, dt_bias, a_log, d_skip, ssm_norm_w, norm2_w, ffn_conv_w, ffn_conv_b):
    qkv, zxbc, k_mean, dt_raw = _in_proj(x2d, norm1_w.reshape(1, -1), w_in_b, layer, w_dt_b,
                                         q_norm_w.reshape(1, -1), k_norm_w.reshape(1, -1))

    attn = _moba(qkv.reshape(3 * ATTN_HEADS, batch, seq, ATTN_HEAD_DIM),
                 k_mean.reshape(batch, seq // MOBA_BLOCK, ATTN_WIDTH))
    ssm = _ssd(zxbc.reshape(batch, seq, -1), dt_raw.reshape(batch, seq, LANES), ssm_conv_w,
               ssm_conv_b.reshape(1, -1), _pad_lanes(dt_bias), _pad_lanes(a_log),
               jnp.repeat(d_skip, SSM_HEAD_DIM).reshape(1, -1), ssm_norm_w.reshape(1, -1))

    x1 = _out_proj(x2d, attn.reshape(ATTN_HEADS, -1, ATTN_HEAD_DIM), ssm.reshape(-1, SSM_WIDTH), w_out_b, layer)
    return _conv_glu(x1, norm2_w.reshape(1, -1), w_up_b, ffn_conv_w, ffn_conv_b.reshape(1, -1), w_down_b,
                     layer, seq)


def kernel(x, norm1_w, w_in, q_norm_w, k_norm_w, ssm_conv_w, ssm_conv_b, dt_bias, a_log, d_skip, ssm_norm_w,
           w_out, norm2_w, w_up, ffn_conv_w, ffn_conv_b, w_down):
    batch, seq, d = x.shape
    x2d = x.reshape(batch * seq, d)
    w_in_b = w_in[:, :, :PROJ_MAIN].astype(BF16)
    w_dt_b = jnp.pad(w_in[:, :, PROJ_MAIN:], ((0, 0), (0, 0), (0, LANES - SSM_HEADS))).astype(BF16)
    w_out_b, w_up_b, w_down_b = _to_bf16(w_out), _to_bf16(w_up), _to_bf16(w_down)
    for i in range(norm1_w.shape[0]):
        x2d = _layer(x2d, batch, seq, i, w_in_b, w_dt_b, w_out_b, w_up_b, w_down_b, norm1_w[i], q_norm_w[i],
                     k_norm_w[i], ssm_conv_w[i], ssm_conv_b[i], dt_bias[i], a_log[i], d_skip[i], ssm_norm_w[i],
                     norm2_w[i], ffn_conv_w[i], ffn_conv_b[i])
    return x2d.reshape(batch, seq, d)
```

```python
import functools

import jax
import jax.numpy as jnp
from jax import lax
from jax.experimental import pallas as pl
from jax.experimental.pallas import tpu as pltpu

F32 = jnp.float32
BF16 = jnp.bfloat16

D_MODEL = 2048
ATTN_WIDTH = 1024
ATTN_HEAD_DIM = 128
ATTN_HEADS = ATTN_WIDTH // ATTN_HEAD_DIM
MOBA_BLOCK = 256
MOBA_TOPK = 3
SSM_WIDTH = 1024
SSM_HEAD_DIM = 64
SSM_HEADS = SSM_WIDTH // SSM_HEAD_DIM
SSM_GROUPS = 2
SSM_HEADS_PER_GROUP = SSM_HEADS // SSM_GROUPS
SSM_GROUP_WIDTH = SSM_WIDTH // SSM_GROUPS
SSM_STATE = 128
SSM_CONV = 4
XBC_WIDTH = SSM_WIDTH + 2 * SSM_GROUPS * SSM_STATE
PROJ_MAIN = 3 * ATTN_WIDTH + SSM_WIDTH + XBC_WIDTH
D_FF = 5632
FFN_CONV = 3
EPS = 1e-6
LOG2_E = 1.4426950408889634

LANES = 128
MXU_COLS = 256
SUBLANES = 8
VMEM_LIMIT = 56 * 1024 * 1024

SSD_CHUNK = 256
ROW_TILE = 512
UP_ROW_TILE = 1024
IN_ROW_TILE = 256
COL_TILE = 512
FFN_HALO = 16
NORM_ROWS = 64
CONV_ROWS = 64
CAST_ROWS = 256

_NT = (((1,), (1,)), ((), ()))


def _rms(x, w):
    return x * lax.rsqrt(jnp.mean(x * x, axis=-1, keepdims=True) + EPS) * w


def _norm_rows(x_ref, nw_ref, h_ref, dst_off, n_rows):
    def body(c, carry):
        r = pl.multiple_of(c * NORM_ROWS, NORM_ROWS)
        h_ref[pl.ds(dst_off + r, NORM_ROWS), :] = _rms(x_ref[pl.ds(r, NORM_ROWS), :], nw_ref[...]).astype(BF16)
        return carry
    lax.fori_loop(0, n_rows // NORM_ROWS, body, 0)


def _in_proj_kernel(x_ref, nw_ref, w_ref, wdt_ref, qw_ref, kw_ref, qkv_ref, rest_ref, km_ref, dt_ref, h_ref):
    _norm_rows(x_ref, nw_ref, h_ref, 0, x_ref.shape[0])
    dt_ref[...] = jnp.dot(h_ref[...], wdt_ref[...], preferred_element_type=F32)
    q_scale = ATTN_HEAD_DIM ** -0.5 * LOG2_E
    for c0 in range(0, PROJ_MAIN, COL_TILE):
        acc = jnp.dot(h_ref[...], w_ref[:, c0:c0 + COL_TILE], preferred_element_type=F32)
        if c0 >= 3 * ATTN_WIDTH:
            rest_ref[:, c0 - 3 * ATTN_WIDTH:c0 - 3 * ATTN_WIDTH + COL_TILE] = acc
        else:
            for h0 in range(0, COL_TILE, ATTN_HEAD_DIM):
                head = acc[:, h0:h0 + ATTN_HEAD_DIM]
                slot = (c0 + h0) // ATTN_HEAD_DIM
                if c0 >= 2 * ATTN_WIDTH:
                    qkv_ref[slot] = head.astype(BF16)
                elif c0 < ATTN_WIDTH:
                    qkv_ref[slot] = (_rms(head, qw_ref[...]) * q_scale).astype(BF16)
                else:
                    kn = _rms(head, kw_ref[...])
                    qkv_ref[slot] = kn.astype(BF16)
                    kcols = slice(c0 + h0 - ATTN_WIDTH, c0 + h0 - ATTN_WIDTH + ATTN_HEAD_DIM)
                    km_ref[:, kcols] = jnp.mean(kn, axis=0, keepdims=True)


def _in_proj(x2d, nw, w_all, layer, w_dt, qw, kw):
    t, d = x2d.shape
    tm = MOBA_BLOCK
    rest = PROJ_MAIN - 3 * ATTN_WIDTH
    n_slots = 3 * ATTN_HEADS
    return pl.pallas_call(
        _in_proj_kernel,
        grid=(t // tm,),
        in_specs=[
            pl.BlockSpec((tm, d), lambda i: (i, 0)),
            pl.BlockSpec((1, d), lambda i: (0, 0)),
            pl.BlockSpec((None,) + w_all.shape[1:], lambda i: (layer, 0, 0), pipeline_mode=pl.Buffered(1)),
            pl.BlockSpec((None, d, LANES), lambda i: (layer, 0, 0)),
            pl.BlockSpec((1, ATTN_HEAD_DIM), lambda i: (0, 0)),
            pl.BlockSpec((1, ATTN_HEAD_DIM), lambda i: (0, 0)),
        ],
        out_specs=[
            pl.BlockSpec((n_slots, tm, ATTN_HEAD_DIM), lambda i: (0, i, 0)),
            pl.BlockSpec((tm, rest), lambda i: (i, 0)),
            pl.BlockSpec((None, 1, ATTN_WIDTH), lambda i: (i, 0, 0)),
            pl.BlockSpec((tm, LANES), lambda i: (i, 0)),
        ],
        out_shape=[jax.ShapeDtypeStruct((n_slots, t, ATTN_HEAD_DIM), BF16), jax.ShapeDtypeStruct((t, rest), F32),
                   jax.ShapeDtypeStruct((t // tm, 1, ATTN_WIDTH), F32), jax.ShapeDtypeStruct((t, LANES), F32)],
        scratch_shapes=[pltpu.VMEM((tm, d), BF16)],
        compiler_params=pltpu.CompilerParams(
            dimension_semantics=("parallel",), vmem_limit_bytes=VMEM_LIMIT),
        name="in_proj",
    )(x2d, nw, w_all, w_dt, qw, kw)


def _pair_schedule(nb, width):
    remaining = {i: list(range(i)) for i in range(1, nb)}
    qi, kj = [], []
    while any(remaining.values()):
        live = sorted((i for i in remaining if remaining[i]), key=lambda i: -len(remaining[i]))
        if len(live) < width:
            return None
        for i in live[:width]:
            qi.append(i)
            kj.append(remaining[i].pop())
    return qi, kj


def _moba_schedule(nb):
    for width in (4, 2, 1):
        sched = _pair_schedule(nb, width)
        if sched is not None and width <= nb - 1:
            qi, kj = sched
            if (len(qi) // width) % 2:
                qi = qi + list(range(1, width + 1))
                kj = kj + [nb - 1] * width
            return width, (qi, kj)
    raise ValueError(f"no MoBA pair schedule for {nb} blocks")


def _moba_kernel(qi_ref, kj_ref, q_ref, k_ref, v_ref, km_ref, o_ref,
                 vt_s, km3_s, sel_s, m_s, l_s, acc_s,
                 sc_a, sc_b, p_a, p_b, al_a, al_b, *, nb, width, n_groups):
    blk = MOBA_BLOCK
    for j in range(nb):
        vt_s[j] = v_ref[j * blk:(j + 1) * blk, :].astype(F32).T.astype(BF16)

    def block(ref, i):
        return ref[pl.ds(pl.multiple_of(i * blk, blk), blk), :]

    key_i = lax.broadcasted_iota(jnp.int32, (blk, blk), 0)
    qry_i = lax.broadcasted_iota(jnp.int32, (blk, blk), 1)
    bid = lax.broadcasted_iota(jnp.int32, (nb, blk), 0)

    km = km_ref[...]
    km_hi = km.astype(BF16)
    km_r = km - km_hi.astype(F32)
    km_mid = km_r.astype(BF16)
    km3_s[...] = jnp.concatenate([km_hi, km_mid, (km_r - km_mid.astype(F32)).astype(BF16)], axis=0)

    def init_group(t, carry):
        ids = [t * width + u for u in range(width)]
        gate3 = [lax.dot_general(km3_s[...], block(q_ref, i), _NT, preferred_element_type=F32) for i in ids]
        owns = [lax.dot_general(block(k_ref, i), block(q_ref, i), _NT, preferred_element_type=F32) for i in ids]
        for i, g3 in zip(ids, gate3):
            gate = (g3[0:nb] + g3[nb:2 * nb]) + g3[2 * nb:3 * nb]
            past = bid < i
            g = jnp.where(past, gate, -jnp.inf)
            rank = jnp.zeros((nb, blk), jnp.int32)
            for jp in range(nb):
                row = g[jp:jp + 1, :]
                rank = rank + jnp.where(row > g, 1, jnp.where(row == g, jnp.where(bid > jp, 1, 0), 0))
            sel = jnp.where(past, jnp.where(rank < MOBA_TOPK, 1.0, 0.0), 0.0)
            for jp in range(nb):
                sel_s[i * nb + jp] = sel[jp:jp + 1, :]
        for i, s in zip(ids, owns):
            s = jnp.where(key_i <= qry_i, s, -jnp.inf)
            m = jnp.max(s, axis=0, keepdims=True)
            p = jnp.exp2(s - m)
            m_s[i] = m
            l_s[i] = jnp.sum(p, axis=0, keepdims=True)
            acc_s[i] = jnp.dot(vt_s[i], p.astype(BF16), preferred_element_type=F32)
        return carry

    lax.fori_loop(0, nb // width, init_group, 0)

    def group_pairs(g):
        return [(qi_ref[g * width + u], kj_ref[g * width + u]) for u in range(width)]

    def score_group(g, sc):
        for u, (i, j) in enumerate(group_pairs(g)):
            sc[u] = lax.dot_general(block(k_ref, j), block(q_ref, i), _NT, preferred_element_type=F32)

    def softmax_group(g, sc, p_buf, al_buf):
        pairs = group_pairs(g)
        m_old = [m_s[i] for i, _ in pairs]
        upd = []
        for u, ((i, j), m0) in enumerate(zip(pairs, m_old)):
            s = jnp.where(sel_s[i * nb + j] > 0.0, sc[u], -jnp.inf)
            m1 = jnp.maximum(m0, jnp.max(s, axis=0, keepdims=True))
            p = jnp.exp2(s - m1)
            p_buf[u] = p.astype(BF16)
            alpha = jnp.exp2(m0 - m1)
            al_buf[u] = alpha
            upd.append((i, m1, alpha, jnp.sum(p, axis=0, keepdims=True)))
        for i, m1, alpha, psum in upd:
            m_s[i] = m1
            l_s[i] = alpha * l_s[i] + psum

    def pv_group(g, p_buf, al_buf):
        pairs = group_pairs(g)
        pvs = [jnp.dot(vt_s[j], p_buf[u], preferred_element_type=F32) for u, (_, j) in enumerate(pairs)]
        for u, ((i, _), pv) in enumerate(zip(pairs, pvs)):
            acc_s[i] = al_buf[u] * acc_s[i] + pv

    def step(g, sc_cur, sc_next, p_cur, al_cur, p_prev, al_prev):
        pv_group(jnp.maximum(g - 1, 0), p_prev, al_prev)
        score_group(jnp.minimum(g + 1, n_groups - 1), sc_next)
        softmax_group(g, sc_cur, p_cur, al_cur)

    p_b[...] = jnp.zeros(p_b.shape, BF16)
    al_b[...] = jnp.ones(al_b.shape, F32)
    score_group(0, sc_a)

    def two_steps(t, carry):
        step(2 * t, sc_a, sc_b, p_a, al_a, p_b, al_b)
        step(2 * t + 1, sc_b, sc_a, p_b, al_b, p_a, al_a)
        return carry

    lax.fori_loop(0, n_groups // 2, two_steps, 0)
    pv_group(n_groups - 1, p_b, al_b)

    def finish_two(t, carry):
        for i in (2 * t, 2 * t + 1):
            o_ref[pl.ds(pl.multiple_of(i * blk, blk), blk), :] = (acc_s[i] / l_s[i]).T.astype(o_ref.dtype)
        return carry

    lax.fori_loop(0, nb // 2, finish_two, 0)


def _moba(qkv4, km3):
    _, b, s, _ = qkv4.shape
    nb = s // MOBA_BLOCK
    assert nb % 2 == 0
    dh = ATTN_HEAD_DIM
    blk = MOBA_BLOCK
    width, (qi, kj) = _moba_schedule(nb)
    assert nb % width == 0 and (len(qi) // width) % 2 == 0
    smem = pl.BlockSpec(memory_space=pltpu.SMEM)
    return pl.pallas_call(
        functools.partial(_moba_kernel, nb=nb, width=width, n_groups=len(qi) // width),
        grid=(b, ATTN_HEADS),
        in_specs=[
            smem,
            smem,
            pl.BlockSpec((None, None, s, dh), lambda bi, h: (h, bi, 0, 0)),
            pl.BlockSpec((None, None, s, dh), lambda bi, h: (ATTN_HEADS + h, bi, 0, 0)),
            pl.BlockSpec((None, None, s, dh), lambda bi, h: (2 * ATTN_HEADS + h, bi, 0, 0)),
            pl.BlockSpec((None, nb, dh), lambda bi, h: (bi, 0, h)),
        ],
        out_specs=pl.BlockSpec((None, None, s, dh), lambda bi, h: (h, bi, 0, 0)),
        out_shape=jax.ShapeDtypeStruct((ATTN_HEADS, b, s, dh), BF16),
        scratch_shapes=[
            pltpu.VMEM((nb, dh, blk), BF16),
            pltpu.VMEM((3 * nb, dh), BF16),
            pltpu.VMEM((nb * nb, 1, blk), F32),
            pltpu.VMEM((nb, 1, blk), F32),
            pltpu.VMEM((nb, 1, blk), F32),
            pltpu.VMEM((nb, dh, blk), F32),
            pltpu.VMEM((width, blk, blk), F32),
            pltpu.VMEM((width, blk, blk), F32),
            pltpu.VMEM((width, blk, blk), BF16),
            pltpu.VMEM((width, blk, blk), BF16),
            pltpu.VMEM((width, 1, blk), F32),
            pltpu.VMEM((width, 1, blk), F32),
        ],
        compiler_params=pltpu.CompilerParams(
            dimension_semantics=("parallel", "parallel"), vmem_limit_bytes=VMEM_LIMIT),
        name="moba",
    )(jnp.asarray(qi, jnp.int32), jnp.asarray(kj, jnp.int32), qkv4, qkv4, qkv4, km3)


def _pair_cols(arr, i0):
    rows = arr.shape[0]
    lo = lax.broadcasted_iota(jnp.int32, (rows, LANES), 1) < SSM_HEAD_DIM
    a0 = jnp.broadcast_to(arr[:, i0:i0 + 1], (rows, LANES))
    a1 = jnp.broadcast_to(arr[:, i0 + 1:i0 + 2], (rows, LANES))
    return jnp.where(lo, a0, a1)


def _ssd_kernel(xs_ref, bc_ref, z_ref, dt_ref, cw_ref, cb_ref, dtb_ref, alog_ref, dsk_ref, nw_ref, o_ref,
                ext_s, xbc_s, ht_s, y_s, wx_s, dec_s):
    L = xs_ref.shape[0]
    pad = SUBLANES

    @pl.when(pl.program_id(1) == 0)
    def _():
        ext_s[0:pad, :] = jnp.zeros((pad, XBC_WIDTH), F32)
        ht_s[...] = jnp.zeros(ht_s.shape, F32)

    ext_s[pad:pad + L, 0:SSM_WIDTH] = xs_ref[...]
    ext_s[pad:pad + L, SSM_WIDTH:XBC_WIDTH] = bc_ref[...]
    for cblk in range(XBC_WIDTH // LANES):
        cols = slice(cblk * LANES, (cblk + 1) * LANES)
        conv = cb_ref[:, cols] + cw_ref[0:1, cols] * ext_s[pad - 3:pad - 3 + L, cols]
        for j in range(1, SSM_CONV):
            conv = conv + cw_ref[j:j + 1, cols] * ext_s[pad - 3 + j:pad - 3 + j + L, cols]
        xbc_s[:, cols] = conv * jax.nn.sigmoid(conv)
    ext_s[0:pad, :] = ext_s[L:L + pad, :]

    dtv = dt_ref[...] + dtb_ref[...]
    dt = jnp.maximum(dtv, 0.0) + jnp.log1p(jnp.exp(-jnp.abs(dtv)))
    la = dt * (-jnp.exp(alog_ref[...]))
    row_i = lax.broadcasted_iota(jnp.int32, (L, L), 0)
    col_i = lax.broadcasted_iota(jnp.int32, (L, L), 1)
    tril = row_i >= col_i
    acol = jnp.dot(jnp.where(tril, 1.0, 0.0), la, precision=lax.Precision.HIGHEST,
                   preferred_element_type=F32)
    arow = acol.T
    ecol = jnp.exp(acol)
    aend = acol[L - 1:L, :]
    wcol = jnp.exp(aend - acol)
    eend = jnp.exp(aend)
    lo = lax.broadcasted_iota(jnp.int32, (L, LANES), 1) < SSM_HEAD_DIM

    for g in range(SSM_GROUPS):
        b_off = SSM_WIDTH + g * SSM_STATE
        c_off = SSM_WIDTH + SSM_GROUPS * SSM_STATE + g * SSM_STATE
        bg = xbc_s[:, b_off:b_off + SSM_STATE]
        cg = xbc_s[:, c_off:c_off + SSM_STATE].astype(BF16)
        cb = lax.dot_general(cg, bg.astype(BF16), _NT, preferred_element_type=F32)
        cbm = jnp.where(tril, cb, 0.0)
        ht = ht_s[g]
        ch = jnp.dot(cg, ht.astype(BF16), preferred_element_type=F32)
        for pr in range(SSM_HEADS_PER_GROUP // 2):
            i0 = g * SSM_HEADS_PER_GROUP + 2 * pr
            lanes = slice(i0 * SSM_HEAD_DIM, i0 * SSM_HEAD_DIM + LANES)
            gl = slice(pr * LANES, (pr + 1) * LANES)
            xs_p = xbc_s[:, lanes]
            xdt = xs_p * _pair_cols(dt, i0)
            yd = None
            for hh in range(2):
                idx = i0 + hh
                seg = acol[:, idx:idx + 1] - arow[idx:idx + 1, :]
                mm = (cbm * jnp.exp(jnp.minimum(seg, 0.0))).astype(BF16)
                xh = jnp.where(lo, xdt, 0.0) if hh == 0 else jnp.where(lo, 0.0, xdt)
                part = jnp.dot(mm, xh.astype(BF16), preferred_element_type=F32)
                yd = part if yd is None else yd + part
            y_off = ch[:, gl] * _pair_cols(ecol, i0)
            y_s[:, lanes] = yd + y_off + dsk_ref[:, lanes] * xs_p
            wx_s[:, gl] = (xdt * _pair_cols(wcol, i0)).astype(BF16)
            dec_s[:, gl] = _pair_cols(eend, i0)
        ht_s[g] = ht * dec_s[...] + jnp.dot(bg.T.astype(BF16), wx_s[...], preferred_element_type=F32)

    for g in range(SSM_GROUPS):
        cols = slice(g * SSM_GROUP_WIDTH, (g + 1) * SSM_GROUP_WIDTH)
        zz = z_ref[:, cols]
        yg = y_s[:, cols] * (zz * jax.nn.sigmoid(zz))
        o_ref[:, cols] = _rms(yg, nw_ref[:, cols]).astype(o_ref.dtype)


def _ssd(proj3, dt3, conv_w, conv_b, dt_bias, a_log, d_skip, norm_w):
    b, s, _ = proj3.shape
    L = SSD_CHUNK
    full = lambda shape: pl.BlockSpec(shape, lambda bi, c: (0,) * len(shape))
    return pl.pallas_call(
        _ssd_kernel,
        grid=(b, s // L),
        in_specs=[
            pl.BlockSpec((None, L, SSM_WIDTH), lambda bi, c: (bi, c, 1)),
            pl.BlockSpec((None, L, XBC_WIDTH - SSM_WIDTH),
                         lambda bi, c: (bi, c, 2 * SSM_WIDTH // (XBC_WIDTH - SSM_WIDTH))),
            pl.BlockSpec((None, L, SSM_WIDTH), lambda bi, c: (bi, c, 0)),
            pl.BlockSpec((None, L, LANES), lambda bi, c: (bi, c, 0)),
            full((SSM_CONV, XBC_WIDTH)),
            full((1, XBC_WIDTH)),
            full((1, LANES)),
            full((1, LANES)),
            full((1, SSM_WIDTH)),
            full((1, SSM_WIDTH)),
        ],
        out_specs=pl.BlockSpec((None, L, SSM_WIDTH), lambda bi, c: (bi, c, 0)),
        out_shape=jax.ShapeDtypeStruct((b, s, SSM_WIDTH), BF16),
        scratch_shapes=[
            pltpu.VMEM((L + 2 * SUBLANES, XBC_WIDTH), F32),
            pltpu.VMEM((L, XBC_WIDTH), F32),
            pltpu.VMEM((SSM_GROUPS, SSM_STATE, SSM_GROUP_WIDTH), F32),
            pltpu.VMEM((L, SSM_WIDTH), F32),
            pltpu.VMEM((L, SSM_GROUP_WIDTH), BF16),
            pltpu.VMEM((1, SSM_GROUP_WIDTH), F32),
        ],
        compiler_params=pltpu.CompilerParams(
            dimension_semantics=("parallel", "arbitrary"), vmem_limit_bytes=VMEM_LIMIT),
        name="ssd",
    )(proj3, proj3, proj3, dt3, conv_w, conv_b, dt_bias, a_log, d_skip, norm_w)


def _out_proj_kernel(x_ref, a_ref, s_ref, wa_ref, ws_ref, o_ref):
    attn = jnp.concatenate([a_ref[h] for h in range(a_ref.shape[0])], axis=1)
    o_ref[...] = (x_ref[...]
                  + jnp.dot(attn, wa_ref[...], preferred_element_type=F32)
                  + jnp.dot(s_ref[...], ws_ref[...], preferred_element_type=F32))


def _out_proj(x2d, attn2d, ssm2d, w_out, layer):
    t, d = x2d.shape
    return pl.pallas_call(
        _out_proj_kernel,
        grid=(t // ROW_TILE,),
        in_specs=[
            pl.BlockSpec((ROW_TILE, d), lambda i: (i, 0)),
            pl.BlockSpec((ATTN_HEADS, ROW_TILE, ATTN_HEAD_DIM), lambda i: (0, i, 0)),
            pl.BlockSpec((ROW_TILE, SSM_WIDTH), lambda i: (i, 0)),
            pl.BlockSpec((None, ATTN_WIDTH, d), lambda i: (layer, 0, 0)),
            pl.BlockSpec((None, SSM_WIDTH, d), lambda i: (layer, ATTN_WIDTH // SSM_WIDTH, 0)),
        ],
        out_specs=pl.BlockSpec((ROW_TILE, d), lambda i: (i, 0)),
        out_shape=jax.ShapeDtypeStruct((t, d), F32),
        compiler_params=pltpu.CompilerParams(
            dimension_semantics=("parallel",), vmem_limit_bytes=VMEM_LIMIT),
        name="out_proj",
    )(x2d, attn2d, ssm2d, w_out, w_out)


def _conv_glu_up_kernel(x_ref, halo_ref, nw_ref, wg_ref, wv_ref, cwg_ref, cwv_ref, cbg_ref, cbv_ref,
                        a_ref, h_s, hp_s, ug0_s, uv0_s, ug1_s, uv1_s, *, tiles_per_seq, n_up):
    tm = x_ref.shape[0]
    tf = wg_ref.shape[1]
    grp = ROW_TILE
    span = grp // SUBLANES
    halo = FFN_HALO
    bases = [halo + g * (grp + halo) for g in range(tm // grp)]
    i = pl.program_id(0)
    f = pl.program_id(1)

    u_bufs = ((ug0_s, uv0_s), (ug1_s, uv1_s))

    n_parts = 2 * (tf // MXU_COLS)

    def up_dot_part(slot, part):
        u_s, w_ref = ((u_bufs[slot][0], wg_ref), (u_bufs[slot][1], wv_ref))[part // (tf // MXU_COLS)]
        cols = slice((part % (tf // MXU_COLS)) * MXU_COLS, (part % (tf // MXU_COLS) + 1) * MXU_COLS)
        u_s[:, cols] = jnp.dot(hp_s[...], w_ref[:, cols], preferred_element_type=F32)

    def up_fix(slot):
        for u_s in u_bufs[slot]:
            for g, base in enumerate(bases):
                prev = [base - 1, base - 2] if g == 0 else [bases[g - 1] + grp - 1, bases[g - 1] + grp - 1 - SUBLANES]
                before = [jnp.concatenate([u_s[prev[d - 1]:prev[d - 1] + 1, :],
                                           u_s[base + grp - d * SUBLANES:base + grp - d * SUBLANES + SUBLANES - 1, :]],
                                          axis=0) for d in (1, 2)]
                u_s[base - SUBLANES:base, :] = before[0]
                u_s[base - 2 * SUBLANES:base - SUBLANES, :] = before[1]

    def conv(u_s, cw_ref, cb_ref, r0, cols):
        out = cb_ref[:, cols]
        for j in range(FFN_CONV):
            first = r0 - (FFN_CONV - 1 - j) * SUBLANES
            out = out + cw_ref[j:j + 1, cols] * u_s[first:first + CONV_ROWS, cols]
        return out

    def conv_act(slot, part=0, parts=1):
        ug_s, uv_s = u_bufs[slot]
        for c0 in range(part * (tf // parts), (part + 1) * (tf // parts), LANES):
            cols = slice(c0, c0 + LANES)
            for g, base in enumerate(bases):
                for r0 in range(0, grp, CONV_ROWS):
                    gate = conv(ug_s, cwg_ref, cbg_ref, base + r0, cols)
                    val = conv(uv_s, cwv_ref, cbv_ref, base + r0, cols)
                    rows = slice(g * grp + r0, g * grp + r0 + CONV_ROWS)
                    a_ref[rows, cols] = (gate * jax.nn.sigmoid(gate) * val).astype(BF16)

    @pl.when(f == 0)
    def _():
        _norm_rows(x_ref, nw_ref, h_s, 0, tm)
        keep = jnp.where(i % tiles_per_seq == 0, 0.0, 1.0)
        hp_s[0:halo, :] = (_rms(halo_ref[...], nw_ref[...]) * keep).astype(BF16)
        pos = lax.broadcasted_iota(jnp.int32, (grp, grp), 0)
        row = lax.broadcasted_iota(jnp.int32, (grp, grp), 1)
        perm = jnp.where(row == (pos % SUBLANES) * span + pos // SUBLANES, 1.0, 0.0).astype(BF16)
        for g, base in enumerate(bases):
            if g > 0:
                hp_s[base - halo:base, :] = jnp.zeros((halo, hp_s.shape[1]), BF16)
            for c0 in range(0, h_s.shape[1], tf):
                hp_s[base:base + grp, c0:c0 + tf] = jnp.dot(perm, h_s[g * grp:(g + 1) * grp, c0:c0 + tf],
                                                            preferred_element_type=F32).astype(BF16)
        for part in range(n_parts):
            up_dot_part(0, part)
        up_fix(0)

    for parity in (0, 1):
        @pl.when((f >= 1) & (f < n_up) & (f % 2 == parity))
        def _():
            for part in range(n_parts):
                up_dot_part(parity, part)
                conv_act(1 - parity, part, n_parts)
            up_fix(parity)

    @pl.when(f == n_up)
    def _():
        conv_act((n_up - 1) % 2)


def _conv_glu_down_kernel(a_ref, x_ref, wd_ref, o_ref, op_s):
    tm = x_ref.shape[0]
    span = tm // SUBLANES
    for n0 in range(0, o_ref.shape[1], COL_TILE):
        acc = jnp.dot(a_ref[...], wd_ref[:, n0:n0 + COL_TILE], preferred_element_type=F32)
        for c in range(op_s.shape[0]):
            op_s[c] = acc[:, c * LANES:(c + 1) * LANES]
        for a in range(SUBLANES):
            rows = slice(a * span, (a + 1) * span)
            for c in range(op_s.shape[0]):
                cols = slice(n0 + c * LANES, n0 + (c + 1) * LANES)
                o_ref[rows, cols] = x_ref[rows, cols] + op_s[c, pl.ds(a, span, stride=SUBLANES), :]


def _conv_glu(x2d, nw, w_up, conv_w, conv_b, w_down, layer, seq_len):
    t, d = x2d.shape
    tm, tf = UP_ROW_TILE, COL_TILE
    n_up = D_FF // tf
    halo_blocks = tm // FFN_HALO
    m_rows = (tm // ROW_TILE) * (ROW_TILE + FFN_HALO)
    up = lambda f: jnp.minimum(f, n_up - 1)
    cv = lambda f: jnp.clip(f - 1, 0, n_up - 1)
    act = pl.pallas_call(
        functools.partial(_conv_glu_up_kernel, tiles_per_seq=seq_len // tm, n_up=n_up),
        grid=(t // tm, n_up + 1),
        in_specs=[
            pl.BlockSpec((tm, d), lambda i, f: (i, 0)),
            pl.BlockSpec((FFN_HALO, d), lambda i, f: (jnp.maximum(i * halo_blocks - 1, 0), 0)),
            pl.BlockSpec((1, d), lambda i, f: (0, 0)),
            pl.BlockSpec((None, d, tf), lambda i, f: (layer, 0, up(f))),
            pl.BlockSpec((None, d, tf), lambda i, f: (layer, 0, n_up + up(f))),
            pl.BlockSpec((FFN_CONV, tf), lambda i, f: (0, cv(f))),
            pl.BlockSpec((FFN_CONV, tf), lambda i, f: (0, n_up + cv(f))),
            pl.BlockSpec((1, tf), lambda i, f: (0, cv(f))),
            pl.BlockSpec((1, tf), lambda i, f: (0, n_up + cv(f))),
        ],
        out_specs=pl.BlockSpec((tm, tf), lambda i, f: (i, cv(f))),
        out_shape=jax.ShapeDtypeStruct((t, D_FF), BF16),
        scratch_shapes=[
            pltpu.VMEM((tm, d), BF16),
            pltpu.VMEM((m_rows, d), BF16),
            pltpu.VMEM((m_rows, tf), F32),
            pltpu.VMEM((m_rows, tf), F32),
            pltpu.VMEM((m_rows, tf), F32),
            pltpu.VMEM((m_rows, tf), F32),
        ],
        compiler_params=pltpu.CompilerParams(
            dimension_semantics=("parallel", "arbitrary"), vmem_limit_bytes=VMEM_LIMIT),
        name="conv_glu_up",
    )(x2d, x2d, nw, w_up, w_up, conv_w, conv_w, conv_b, conv_b)
    tm = ROW_TILE
    return pl.pallas_call(
        _conv_glu_down_kernel,
        grid=(t // tm,),
        in_specs=[
            pl.BlockSpec((tm, D_FF), lambda i: (i, 0)),
            pl.BlockSpec((tm, d), lambda i: (i, 0)),
            pl.BlockSpec((None, D_FF, d), lambda i: (layer, 0, 0), pipeline_mode=pl.Buffered(1)),
        ],
        out_specs=pl.BlockSpec((tm, d), lambda i: (i, 0)),
        out_shape=jax.ShapeDtypeStruct((t, d), F32),
        scratch_shapes=[pltpu.VMEM((COL_TILE // LANES, tm, LANES), F32)],
        compiler_params=pltpu.CompilerParams(
            dimension_semantics=("parallel",), vmem_limit_bytes=VMEM_LIMIT),
        name="conv_glu_down",
    )(act, x2d, w_down)


def _pad_lanes(v):
    return jnp.pad(v.reshape(1, -1), ((0, 0), (0, LANES - v.shape[-1])))


def _cast_kernel(w_ref, o_ref):
    o_ref[...] = w_ref[...].astype(BF16)


def _to_bf16(w):
    depth, k, n = w.shape
    return pl.pallas_call(
        _cast_kernel,
        grid=(depth, k // CAST_ROWS),
        in_specs=[pl.BlockSpec((None, CAST_ROWS, n), lambda l, r: (l, r, 0))],
        out_specs=pl.BlockSpec((None, CAST_ROWS, n), lambda l, r: (l, r, 0)),
        out_shape=jax.ShapeDtypeStruct(w.shape, BF16),
        compiler_params=pltpu.CompilerParams(
            dimension_semantics=("parallel", "parallel"), vmem_limit_bytes=VMEM_LIMIT),
        name="cast_bf16",
    )(w)


def _layer(x2d, batch, seq, layer, w_in_b, w_dt_b, w_out_b, w_up_b, w_down_b, norm1_w, q_norm_w, k_norm_w,
           ssm_conv_w, ssm_conv_b, dt_bias, a_log, d_skip, ssm_norm_w, norm2_w, ffn_conv_w, ffn_conv_b):
    qkv, zxbc, k_mean, dt_raw = _in_proj(x2d, norm1_w.reshape(1, -1), w_in_b, layer, w_dt_b,
                                         q_norm_w.reshape(1, -1), k_norm_w.reshape(1, -1))

    attn = _moba(qkv.reshape(3 * ATTN_HEADS, batch, seq, ATTN_HEAD_DIM),
                 k_mean.reshape(batch, seq // MOBA_BLOCK, ATTN_WIDTH))
    ssm = _ssd(zxbc.reshape(batch, seq, -1), dt_raw.reshape(batch, seq, LANES), ssm_conv_w,
               ssm_conv_b.reshape(1, -1), _pad_lanes(dt_bias), _pad_lanes(a_log),
               jnp.repeat(d_skip, SSM_HEAD_DIM).reshape(1, -1), ssm_norm_w.reshape(1, -1))

    x1 = _out_proj(x2d, attn.reshape(ATTN_HEADS, -1, ATTN_HEAD_DIM), ssm.reshape(-1, SSM_WIDTH), w_out_b, layer)
    return _conv_glu(x1, norm2_w.reshape(1, -1), w_up_b, ffn_conv_w, ffn_conv_b.reshape(1, -1), w_down_b,
                     layer, seq)


def kernel(x, norm1_w, w_in, q_norm_w, k_norm_w, ssm_conv_w, ssm_conv_b, dt_bias, a_log, d_skip, ssm_norm_w,
           w_out, norm2_w, w_up, ffn_conv_w, ffn_conv_b, w_down):
    batch, seq, d = x.shape
    x2d = x.reshape(batch * seq, d)
    w_in_b = w_in[:, :, :PROJ_MAIN].astype(BF16)
    w_dt_b = jnp.pad(w_in[:, :, PROJ_MAIN:], ((0, 0), (0, 0), (0, LANES - SSM_HEADS))).astype(BF16)
    w_out_b, w_up_b, w_down_b = _to_bf16(w_out), _to_bf16(w_up), _to_bf16(w_down)
    for i in range(norm1_w.shape[0]):
        x2d = _layer(x2d, batch, seq, i, w_in_b, w_dt_b, w_out_b, w_up_b, w_down_b, norm1_w[i], q_norm_w[i],
                     k_norm_w[i], ssm_conv_w[i], ssm_conv_b[i], dt_bias[i], a_log[i], d_skip[i], ssm_norm_w[i],
                     norm2_w[i], ffn_conv_w[i], ffn_conv_b[i])
    return x2d.reshape(batch, seq, d)
```

```python
import functools

import jax
import jax.numpy as jnp
from jax import lax
from jax.experimental import pallas as pl
from jax.experimental.pallas import tpu as pltpu

F32 = jnp.float32
BF16 = jnp.bfloat16

D_MODEL = 2048
ATTN_WIDTH = 1024
ATTN_HEAD_DIM = 128
ATTN_HEADS = ATTN_WIDTH // ATTN_HEAD_DIM
MOBA_BLOCK = 256
MOBA_TOPK = 3
SSM_WIDTH = 1024
SSM_HEAD_DIM = 64
SSM_HEADS = SSM_WIDTH // SSM_HEAD_DIM
SSM_GROUPS = 2
SSM_HEADS_PER_GROUP = SSM_HEADS // SSM_GROUPS
SSM_GROUP_WIDTH = SSM_WIDTH // SSM_GROUPS
SSM_STATE = 128
SSM_CONV = 4
XBC_WIDTH = SSM_WIDTH + 2 * SSM_GROUPS * SSM_STATE
PROJ_MAIN = 3 * ATTN_WIDTH + SSM_WIDTH + XBC_WIDTH
D_FF = 5632
FFN_CONV = 3
EPS = 1e-6
LOG2_E = 1.4426950408889634

LANES = 128
MXU_COLS = 256
SUBLANES = 8
VMEM_LIMIT = 56 * 1024 * 1024

SSD_CHUNK = 256
ROW_TILE = 512
UP_ROW_TILE = 1024
IN_ROW_TILE = 512
COL_TILE = 512
FFN_HALO = 16
NORM_ROWS = 64
CONV_ROWS = 64
CAST_ROWS = 256

_NT = (((1,), (1,)), ((), ()))


def _rms(x, w):
    return x * lax.rsqrt(jnp.mean(x * x, axis=-1, keepdims=True) + EPS) * w


def _norm_rows(x_ref, nw_ref, h_ref, dst_off, n_rows):
    def body(c, carry):
        r = pl.multiple_of(c * NORM_ROWS, NORM_ROWS)
        h_ref[pl.ds(dst_off + r, NORM_ROWS), :] = _rms(x_ref[pl.ds(r, NORM_ROWS), :], nw_ref[...]).astype(BF16)
        return carry
    lax.fori_loop(0, n_rows // NORM_ROWS, body, 0)


def _in_proj_kernel(x_ref, nw_ref, w_ref, wdt_ref, qw_ref, kw_ref, qkv_ref, rest_ref, km_ref, dt_ref, h_ref):
    _norm_rows(x_ref, nw_ref, h_ref, 0, x_ref.shape[0])
    dt_ref[...] = jnp.dot(h_ref[...], wdt_ref[...], preferred_element_type=F32)
    q_scale = ATTN_HEAD_DIM ** -0.5 * LOG2_E
    for c0 in range(0, PROJ_MAIN, COL_TILE):
        acc = jnp.dot(h_ref[...], w_ref[:, c0:c0 + COL_TILE], preferred_element_type=F32)
        if c0 >= 3 * ATTN_WIDTH:
            rest_ref[:, c0 - 3 * ATTN_WIDTH:c0 - 3 * ATTN_WIDTH + COL_TILE] = acc
        else:
            for h0 in range(0, COL_TILE, ATTN_HEAD_DIM):
                head = acc[:, h0:h0 + ATTN_HEAD_DIM]
                slot = (c0 + h0) // ATTN_HEAD_DIM
                if c0 >= 2 * ATTN_WIDTH:
                    qkv_ref[slot] = head.astype(BF16)
                elif c0 < ATTN_WIDTH:
                    qkv_ref[slot] = (_rms(head, qw_ref[...]) * q_scale).astype(BF16)
                else:
                    kn = _rms(head, kw_ref[...])
                    qkv_ref[slot] = kn.astype(BF16)
                    kcols = slice(c0 + h0 - ATTN_WIDTH, c0 + h0 - ATTN_WIDTH + ATTN_HEAD_DIM)
                    for blk in range(km_ref.shape[0]):
                        rows = slice(blk * MOBA_BLOCK, (blk + 1) * MOBA_BLOCK)
                        km_ref[blk, :, kcols] = jnp.mean(kn[rows], axis=0, keepdims=True)


def _in_proj(x2d, nw, w_all, layer, w_dt, qw, kw):
    t, d = x2d.shape
    tm = IN_ROW_TILE
    rest = PROJ_MAIN - 3 * ATTN_WIDTH
    n_slots = 3 * ATTN_HEADS
    return pl.pallas_call(
        _in_proj_kernel,
        grid=(t // tm,),
        in_specs=[
            pl.BlockSpec((tm, d), lambda i: (i, 0)),
            pl.BlockSpec((1, d), lambda i: (0, 0)),
            pl.BlockSpec((None,) + w_all.shape[1:], lambda i: (layer, 0, 0), pipeline_mode=pl.Buffered(1)),
            pl.BlockSpec((None, d, LANES), lambda i: (layer, 0, 0)),
            pl.BlockSpec((1, ATTN_HEAD_DIM), lambda i: (0, 0)),
            pl.BlockSpec((1, ATTN_HEAD_DIM), lambda i: (0, 0)),
        ],
        out_specs=[
            pl.BlockSpec((n_slots, tm, ATTN_HEAD_DIM), lambda i: (0, i, 0)),
            pl.BlockSpec((tm, rest), lambda i: (i, 0)),
            pl.BlockSpec((tm // MOBA_BLOCK, 1, ATTN_WIDTH), lambda i: (i, 0, 0)),
            pl.BlockSpec((tm, LANES), lambda i: (i, 0)),
        ],
        out_shape=[jax.ShapeDtypeStruct((n_slots, t, ATTN_HEAD_DIM), BF16), jax.ShapeDtypeStruct((t, rest), F32),
                   jax.ShapeDtypeStruct((t // MOBA_BLOCK, 1, ATTN_WIDTH), F32), jax.ShapeDtypeStruct((t, LANES), F32)],
        scratch_shapes=[pltpu.VMEM((tm, d), BF16)],
        compiler_params=pltpu.CompilerParams(
            dimension_semantics=("parallel",), vmem_limit_bytes=VMEM_LIMIT),
        name="in_proj",
    )(x2d, nw, w_all, w_dt, qw, kw)


def _pair_schedule(nb, width):
    remaining = {i: list(range(i)) for i in range(1, nb)}
    qi, kj = [], []
    while any(remaining.values()):
        live = sorted((i for i in remaining if remaining[i]), key=lambda i: -len(remaining[i]))
        if len(live) < width:
            return None
        for i in live[:width]:
            qi.append(i)
            kj.append(remaining[i].pop())
    return qi, kj


def _moba_schedule(nb):
    for width in (4, 2, 1):
        sched = _pair_schedule(nb, width)
        if sched is not None and width <= nb - 1:
            qi, kj = sched
            if (len(qi) // width) % 2:
                qi = qi + list(range(1, width + 1))
                kj = kj + [nb - 1] * width
            return width, (qi, kj)
    raise ValueError(f"no MoBA pair schedule for {nb} blocks")


def _moba_kernel(qi_ref, kj_ref, q_ref, k_ref, v_ref, km_ref, o_ref,
                 vt_s, km3_s, sel_s, m_s, l_s, acc_s,
                 sc_a, sc_b, p_a, p_b, al_a, al_b, *, nb, width, n_groups):
    blk = MOBA_BLOCK
    for j in range(nb):
        vt_s[j] = v_ref[j * blk:(j + 1) * blk, :].astype(F32).T.astype(BF16)

    def block(ref, i):
        return ref[pl.ds(pl.multiple_of(i * blk, blk), blk), :]

    key_i = lax.broadcasted_iota(jnp.int32, (blk, blk), 0)
    qry_i = lax.broadcasted_iota(jnp.int32, (blk, blk), 1)
    bid = lax.broadcasted_iota(jnp.int32, (nb, blk), 0)

    km = km_ref[...]
    km_hi = km.astype(BF16)
    km_r = km - km_hi.astype(F32)
    km_mid = km_r.astype(BF16)
    km3_s[...] = jnp.concatenate([km_hi, km_mid, (km_r - km_mid.astype(F32)).astype(BF16)], axis=0)

    def init_group(t, carry):
        ids = [t * width + u for u in range(width)]
        gate3 = [lax.dot_general(km3_s[...], block(q_ref, i), _NT, preferred_element_type=F32) for i in ids]
        owns = [lax.dot_general(block(k_ref, i), block(q_ref, i), _NT, preferred_element_type=F32) for i in ids]
        for i, g3 in zip(ids, gate3):
            gate = (g3[0:nb] + g3[nb:2 * nb]) + g3[2 * nb:3 * nb]
            past = bid < i
            g = jnp.where(past, gate, -jnp.inf)
            rank = jnp.zeros((nb, blk), jnp.int32)
            for jp in range(nb):
                row = g[jp:jp + 1, :]
                rank = rank + jnp.where(row > g, 1, jnp.where(row == g, jnp.where(bid > jp, 1, 0), 0))
            sel = jnp.where(past, jnp.where(rank < MOBA_TOPK, 1.0, 0.0), 0.0)
            for jp in range(nb):
                sel_s[i * nb + jp] = sel[jp:jp + 1, :]
        for i, s in zip(ids, owns):
            s = jnp.where(key_i <= qry_i, s, -jnp.inf)
            m = jnp.max(s, axis=0, keepdims=True)
            p = jnp.exp2(s - m)
            m_s[i] = m
            l_s[i] = jnp.sum(p, axis=0, keepdims=True)
            acc_s[i] = jnp.dot(vt_s[i], p.astype(BF16), preferred_element_type=F32)
        return carry

    lax.fori_loop(0, nb // width, init_group, 0)

    def group_pairs(g):
        return [(qi_ref[g * width + u], kj_ref[g * width + u]) for u in range(width)]

    def score_group(g, sc):
        for u, (i, j) in enumerate(group_pairs(g)):
            sc[u] = lax.dot_general(block(k_ref, j), block(q_ref, i), _NT, preferred_element_type=F32)

    def softmax_group(g, sc, p_buf, al_buf):
        pairs = group_pairs(g)
        m_old = [m_s[i] for i, _ in pairs]
        upd = []
        for u, ((i, j), m0) in enumerate(zip(pairs, m_old)):
            s = jnp.where(sel_s[i * nb + j] > 0.0, sc[u], -jnp.inf)
            m1 = jnp.maximum(m0, jnp.max(s, axis=0, keepdims=True))
            p = jnp.exp2(s - m1)
            p_buf[u] = p.astype(BF16)
            alpha = jnp.exp2(m0 - m1)
            al_buf[u] = alpha
            upd.append((i, m1, alpha, jnp.sum(p, axis=0, keepdims=True)))
        for i, m1, alpha, psum in upd:
            m_s[i] = m1
            l_s[i] = alpha * l_s[i] + psum

    def pv_group(g, p_buf, al_buf):
        pairs = group_pairs(g)
        pvs = [jnp.dot(vt_s[j], p_buf[u], preferred_element_type=F32) for u, (_, j) in enumerate(pairs)]
        for u, ((i, _), pv) in enumerate(zip(pairs, pvs)):
            acc_s[i] = al_buf[u] * acc_s[i] + pv

    def step(g, sc_cur, sc_next, p_cur, al_cur, p_prev, al_prev):
        pv_group(jnp.maximum(g - 1, 0), p_prev, al_prev)
        score_group(jnp.minimum(g + 1, n_groups - 1), sc_next)
        softmax_group(g, sc_cur, p_cur, al_cur)

    p_b[...] = jnp.zeros(p_b.shape, BF16)
    al_b[...] = jnp.ones(al_b.shape, F32)
    score_group(0, sc_a)

    def two_steps(t, carry):
        step(2 * t, sc_a, sc_b, p_a, al_a, p_b, al_b)
        step(2 * t + 1, sc_b, sc_a, p_b, al_b, p_a, al_a)
        return carry

    lax.fori_loop(0, n_groups // 2, two_steps, 0)
    pv_group(n_groups - 1, p_b, al_b)

    def finish_two(t, carry):
        for i in (2 * t, 2 * t + 1):
            o_ref[pl.ds(pl.multiple_of(i * blk, blk), blk), :] = (acc_s[i] / l_s[i]).T.astype(o_ref.dtype)
        return carry

    lax.fori_loop(0, nb // 2, finish_two, 0)


def _moba(qkv4, km3):
    _, b, s, _ = qkv4.shape
    nb = s // MOBA_BLOCK
    assert nb % 2 == 0
    dh = ATTN_HEAD_DIM
    blk = MOBA_BLOCK
    width, (qi, kj) = _moba_schedule(nb)
    assert nb % width == 0 and (len(qi) // width) % 2 == 0
    smem = pl.BlockSpec(memory_space=pltpu.SMEM)
    return pl.pallas_call(
        functools.partial(_moba_kernel, nb=nb, width=width, n_groups=len(qi) // width),
        grid=(b, ATTN_HEADS),
        in_specs=[
            smem,
            smem,
            pl.BlockSpec((None, None, s, dh), lambda bi, h: (h, bi, 0, 0)),
            pl.BlockSpec((None, None, s, dh), lambda bi, h: (ATTN_HEADS + h, bi, 0, 0)),
            pl.BlockSpec((None, None, s, dh), lambda bi, h: (2 * ATTN_HEADS + h, bi, 0, 0)),
            pl.BlockSpec((None, nb, dh), lambda bi, h: (bi, 0, h)),
        ],
        out_specs=pl.BlockSpec((None, None, s, dh), lambda bi, h: (h, bi, 0, 0)),
        out_shape=jax.ShapeDtypeStruct((ATTN_HEADS, b, s, dh), BF16),
        scratch_shapes=[
            pltpu.VMEM((nb, dh, blk), BF16),
            pltpu.VMEM((3 * nb, dh), BF16),
            pltpu.VMEM((nb * nb, 1, blk), F32),
            pltpu.VMEM((nb, 1, blk), F32),
            pltpu.VMEM((nb, 1, blk), F32),
            pltpu.VMEM((nb, dh, blk), F32),
            pltpu.VMEM((width, blk, blk), F32),
            pltpu.VMEM((width, blk, blk), F32),
            pltpu.VMEM((width, blk, blk), BF16),
            pltpu.VMEM((width, blk, blk), BF16),
            pltpu.VMEM((width, 1, blk), F32),
            pltpu.VMEM((width, 1, blk), F32),
        ],
        compiler_params=pltpu.CompilerParams(
            dimension_semantics=("parallel", "parallel"), vmem_limit_bytes=VMEM_LIMIT),
        name="moba",
    )(jnp.asarray(qi, jnp.int32), jnp.asarray(kj, jnp.int32), qkv4, qkv4, qkv4, km3)


def _pair_cols(arr, i0):
    rows = arr.shape[0]
    lo = lax.broadcasted_iota(jnp.int32, (rows, LANES), 1) < SSM_HEAD_DIM
    a0 = jnp.broadcast_to(arr[:, i0:i0 + 1], (rows, LANES))
    a1 = jnp.broadcast_to(arr[:, i0 + 1:i0 + 2], (rows, LANES))
    return jnp.where(lo, a0, a1)


def _ssd_kernel(xs_ref, bc_ref, z_ref, dt_ref, cw_ref, cb_ref, dtb_ref, alog_ref, dsk_ref, nw_ref, o_ref,
                ext_s, xbc_s, ht_s, y_s, wx_s, dec_s):
    L = xs_ref.shape[0]
    pad = SUBLANES

    @pl.when(pl.program_id(1) == 0)
    def _():
        ext_s[0:pad, :] = jnp.zeros((pad, XBC_WIDTH), F32)
        ht_s[...] = jnp.zeros(ht_s.shape, F32)

    ext_s[pad:pad + L, 0:SSM_WIDTH] = xs_ref[...]
    ext_s[pad:pad + L, SSM_WIDTH:XBC_WIDTH] = bc_ref[...]
    for cblk in range(XBC_WIDTH // LANES):
        cols = slice(cblk * LANES, (cblk + 1) * LANES)
        conv = cb_ref[:, cols] + cw_ref[0:1, cols] * ext_s[pad - 3:pad - 3 + L, cols]
        for j in range(1, SSM_CONV):
            conv = conv + cw_ref[j:j + 1, cols] * ext_s[pad - 3 + j:pad - 3 + j + L, cols]
        xbc_s[:, cols] = conv * jax.nn.sigmoid(conv)
    ext_s[0:pad, :] = ext_s[L:L + pad, :]

    dtv = dt_ref[...] + dtb_ref[...]
    dt = jnp.maximum(dtv, 0.0) + jnp.log1p(jnp.exp(-jnp.abs(dtv)))
    la = dt * (-jnp.exp(alog_ref[...]))
    row_i = lax.broadcasted_iota(jnp.int32, (L, L), 0)
    col_i = lax.broadcasted_iota(jnp.int32, (L, L), 1)
    tril = row_i >= col_i
    acol = jnp.dot(jnp.where(tril, 1.0, 0.0), la, precision=lax.Precision.HIGHEST,
                   preferred_element_type=F32)
    arow = acol.T
    ecol = jnp.exp(acol)
    aend = acol[L - 1:L, :]
    wcol = jnp.exp(aend - acol)
    eend = jnp.exp(aend)
    lo = lax.broadcasted_iota(jnp.int32, (L, LANES), 1) < SSM_HEAD_DIM

    for g in range(SSM_GROUPS):
        b_off = SSM_WIDTH + g * SSM_STATE
        c_off = SSM_WIDTH + SSM_GROUPS * SSM_STATE + g * SSM_STATE
        bg = xbc_s[:, b_off:b_off + SSM_STATE]
        cg = xbc_s[:, c_off:c_off + SSM_STATE].astype(BF16)
        cb = lax.dot_general(cg, bg.astype(BF16), _NT, preferred_element_type=F32)
        cbm = jnp.where(tril, cb, 0.0)
        ht = ht_s[g]
        ch = jnp.dot(cg, ht.astype(BF16), preferred_element_type=F32)
        for pr in range(SSM_HEADS_PER_GROUP // 2):
            i0 = g * SSM_HEADS_PER_GROUP + 2 * pr
            lanes = slice(i0 * SSM_HEAD_DIM, i0 * SSM_HEAD_DIM + LANES)
            gl = slice(pr * LANES, (pr + 1) * LANES)
            xs_p = xbc_s[:, lanes]
            xdt = xs_p * _pair_cols(dt, i0)
            yd = None
            for hh in range(2):
                idx = i0 + hh
                seg = acol[:, idx:idx + 1] - arow[idx:idx + 1, :]
                mm = (cbm * jnp.exp(jnp.minimum(seg, 0.0))).astype(BF16)
                xh = jnp.where(lo, xdt, 0.0) if hh == 0 else jnp.where(lo, 0.0, xdt)
                part = jnp.dot(mm, xh.astype(BF16), preferred_element_type=F32)
                yd = part if yd is None else yd + part
            y_off = ch[:, gl] * _pair_cols(ecol, i0)
            y_s[:, lanes] = yd + y_off + dsk_ref[:, lanes] * xs_p
            wx_s[:, gl] = (xdt * _pair_cols(wcol, i0)).astype(BF16)
            dec_s[:, gl] = _pair_cols(eend, i0)
        ht_s[g] = ht * dec_s[...] + jnp.dot(bg.T.astype(BF16), wx_s[...], preferred_element_type=F32)

    for g in range(SSM_GROUPS):
        cols = slice(g * SSM_GROUP_WIDTH, (g + 1) * SSM_GROUP_WIDTH)
        zz = z_ref[:, cols]
        yg = y_s[:, cols] * (zz * jax.nn.sigmoid(zz))
        o_ref[:, cols] = _rms(yg, nw_ref[:, cols]).astype(o_ref.dtype)


def _ssd(proj3, dt3, conv_w, conv_b, dt_bias, a_log, d_skip, norm_w):
    b, s, _ = proj3.shape
    L = SSD_CHUNK
    full = lambda shape: pl.BlockSpec(shape, lambda bi, c: (0,) * len(shape))
    return pl.pallas_call(
        _ssd_kernel,
        grid=(b, s // L),
        in_specs=[
            pl.BlockSpec((None, L, SSM_WIDTH), lambda bi, c: (bi, c, 1)),
            pl.BlockSpec((None, L, XBC_WIDTH - SSM_WIDTH),
                         lambda bi, c: (bi, c, 2 * SSM_WIDTH // (XBC_WIDTH - SSM_WIDTH))),
            pl.BlockSpec((None, L, SSM_WIDTH), lambda bi, c: (bi, c, 0)),
            pl.BlockSpec((None, L, LANES), lambda bi, c: (bi, c, 0)),
            full((SSM_CONV, XBC_WIDTH)),
            full((1, XBC_WIDTH)),
            full((1, LANES)),
            full((1, LANES)),
            full((1, SSM_WIDTH)),
            full((1, SSM_WIDTH)),
        ],
        out_specs=pl.BlockSpec((None, L, SSM_WIDTH), lambda bi, c: (bi, c, 0)),
        out_shape=jax.ShapeDtypeStruct((b, s, SSM_WIDTH), BF16),
        scratch_shapes=[
            pltpu.VMEM((L + 2 * SUBLANES, XBC_WIDTH), F32),
            pltpu.VMEM((L, XBC_WIDTH), F32),
            pltpu.VMEM((SSM_GROUPS, SSM_STATE, SSM_GROUP_WIDTH), F32),
            pltpu.VMEM((L, SSM_WIDTH), F32),
            pltpu.VMEM((L, SSM_GROUP_WIDTH), BF16),
            pltpu.VMEM((1, SSM_GROUP_WIDTH), F32),
        ],
        compiler_params=pltpu.CompilerParams(
            dimension_semantics=("parallel", "arbitrary"), vmem_limit_bytes=VMEM_LIMIT),
        name="ssd",
    )(proj3, proj3, proj3, dt3, conv_w, conv_b, dt_bias, a_log, d_skip, norm_w)


def _out_proj_kernel(x_ref, a_ref, s_ref, wa_ref, ws_ref, o_ref):
    attn = jnp.concatenate([a_ref[h] for h in range(a_ref.shape[0])], axis=1)
    o_ref[...] = (x_ref[...]
                  + jnp.dot(attn, wa_ref[...], preferred_element_type=F32)
                  + jnp.dot(s_ref[...], ws_ref[...], preferred_element_type=F32))


def _out_proj(x2d, attn2d, ssm2d, w_out, layer):
    t, d = x2d.shape
    return pl.pallas_call(
        _out_proj_kernel,
        grid=(t // ROW_TILE,),
        in_specs=[
            pl.BlockSpec((ROW_TILE, d), lambda i: (i, 0)),
            pl.BlockSpec((ATTN_HEADS, ROW_TILE, ATTN_HEAD_DIM), lambda i: (0, i, 0)),
            pl.BlockSpec((ROW_TILE, SSM_WIDTH), lambda i: (i, 0)),
            pl.BlockSpec((None, ATTN_WIDTH, d), lambda i: (layer, 0, 0)),
            pl.BlockSpec((None, SSM_WIDTH, d), lambda i: (layer, ATTN_WIDTH // SSM_WIDTH, 0)),
        ],
        out_specs=pl.BlockSpec((ROW_TILE, d), lambda i: (i, 0)),
        out_shape=jax.ShapeDtypeStruct((t, d), F32),
        compiler_params=pltpu.CompilerParams(
            dimension_semantics=("parallel",), vmem_limit_bytes=VMEM_LIMIT),
        name="out_proj",
    )(x2d, attn2d, ssm2d, w_out, w_out)


def _conv_glu_up_kernel(x_ref, halo_ref, nw_ref, wg_ref, wv_ref, cwg_ref, cwv_ref, cbg_ref, cbv_ref,
                        a_ref, h_s, hp_s, ug0_s, uv0_s, ug1_s, uv1_s, *, tiles_per_seq, n_up):
    tm = x_ref.shape[0]
    tf = wg_ref.shape[1]
    grp = ROW_TILE
    span = grp // SUBLANES
    halo = FFN_HALO
    bases = [halo + g * (grp + halo) for g in range(tm // grp)]
    i = pl.program_id(0)
    f = pl.program_id(1)

    u_bufs = ((ug0_s, uv0_s), (ug1_s, uv1_s))

    n_parts = 2 * (tf // MXU_COLS)

    def up_dot_part(slot, part):
        u_s, w_ref = ((u_bufs[slot][0], wg_ref), (u_bufs[slot][1], wv_ref))[part // (tf // MXU_COLS)]
        cols = slice((part % (tf // MXU_COLS)) * MXU_COLS, (part % (tf // MXU_COLS) + 1) * MXU_COLS)
        u_s[:, cols] = jnp.dot(hp_s[...], w_ref[:, cols], preferred_element_type=F32)

    def up_fix(slot):
        for u_s in u_bufs[slot]:
            for g, base in enumerate(bases):
                prev = [base - 1, base - 2] if g == 0 else [bases[g - 1] + grp - 1, bases[g - 1] + grp - 1 - SUBLANES]
                before = [jnp.concatenate([u_s[prev[d - 1]:prev[d - 1] + 1, :],
                                           u_s[base + grp - d * SUBLANES:base + grp - d * SUBLANES + SUBLANES - 1, :]],
                                          axis=0) for d in (1, 2)]
                u_s[base - SUBLANES:base, :] = before[0]
                u_s[base - 2 * SUBLANES:base - SUBLANES, :] = before[1]

    def conv(u_s, cw_ref, cb_ref, r0, cols):
        out = cb_ref[:, cols]
        for j in range(FFN_CONV):
            first = r0 - (FFN_CONV - 1 - j) * SUBLANES
            out = out + cw_ref[j:j + 1, cols] * u_s[first:first + CONV_ROWS, cols]
        return out

    def conv_act(slot, part=0, parts=1):
        ug_s, uv_s = u_bufs[slot]
        for c0 in range(part * (tf // parts), (part + 1) * (tf // parts), LANES):
            cols = slice(c0, c0 + LANES)
            for g, base in enumerate(bases):
                for r0 in range(0, grp, CONV_ROWS):
                    gate = conv(ug_s, cwg_ref, cbg_ref, base + r0, cols)
                    val = conv(uv_s, cwv_ref, cbv_ref, base + r0, cols)
                    rows = slice(g * grp + r0, g * grp + r0 + CONV_ROWS)
                    a_ref[rows, cols] = (gate * jax.nn.sigmoid(gate) * val).astype(BF16)

    @pl.when(f == 0)
    def _():
        _norm_rows(x_ref, nw_ref, h_s, 0, tm)
        keep = jnp.where(i % tiles_per_seq == 0, 0.0, 1.0)
        hp_s[0:halo, :] = (_rms(halo_ref[...], nw_ref[...]) * keep).astype(BF16)
        pos = lax.broadcasted_iota(jnp.int32, (grp, grp), 0)
        row = lax.broadcasted_iota(jnp.int32, (grp, grp), 1)
        perm = jnp.where(row == (pos % SUBLANES) * span + pos // SUBLANES, 1.0, 0.0).astype(BF16)
        for g, base in enumerate(bases):
            if g > 0:
                hp_s[base - halo:base, :] = jnp.zeros((halo, hp_s.shape[1]), BF16)
            for c0 in range(0, h_s.shape[1], tf):
                hp_s[base:base + grp, c0:c0 + tf] = jnp.dot(perm, h_s[g * grp:(g + 1) * grp, c0:c0 + tf],
                                                            preferred_element_type=F32).astype(BF16)
        for part in range(n_parts):
            up_dot_part(0, part)
        up_fix(0)

    for parity in (0, 1):
        @pl.when((f >= 1) & (f < n_up) & (f % 2 == parity))
        def _():
            for part in range(n_parts):
                up_dot_part(parity, part)
                conv_act(1 - parity, part, n_parts)
            up_fix(parity)

    @pl.when(f == n_up)
    def _():
        conv_act((n_up - 1) % 2)


def _conv_glu_down_kernel(a_ref, x_ref, wd_ref, o_ref, op_s):
    tm = x_ref.shape[0]
    span = tm // SUBLANES
    for n0 in range(0, o_ref.shape[1], COL_TILE):
        acc = jnp.dot(a_ref[...], wd_ref[:, n0:n0 + COL_TILE], preferred_element_type=F32)
        for c in range(op_s.shape[0]):
            op_s[c] = acc[:, c * LANES:(c + 1) * LANES]
        for a in range(SUBLANES):
            rows = slice(a * span, (a + 1) * span)
            for c in range(op_s.shape[0]):
                cols = slice(n0 + c * LANES, n0 + (c + 1) * LANES)
                o_ref[rows, cols] = x_ref[rows, cols] + op_s[c, pl.ds(a, span, stride=SUBLANES), :]


def _conv_glu(x2d, nw, w_up, conv_w, conv_b, w_down, layer, seq_len):
    t, d = x2d.shape
    tm, tf = UP_ROW_TILE, COL_TILE
    n_up = D_FF // tf
    halo_blocks = tm // FFN_HALO
    m_rows = (tm // ROW_TILE) * (ROW_TILE + FFN_HALO)
    up = lambda f: jnp.minimum(f, n_up - 1)
    cv = lambda f: jnp.clip(f - 1, 0, n_up - 1)
    act = pl.pallas_call(
        functools.partial(_conv_glu_up_kernel, tiles_per_seq=seq_len // tm, n_up=n_up),
        grid=(t // tm, n_up + 1),
        in_specs=[
            pl.BlockSpec((tm, d), lambda i, f: (i, 0)),
            pl.BlockSpec((FFN_HALO, d), lambda i, f: (jnp.maximum(i * halo_blocks - 1, 0), 0)),
            pl.BlockSpec((1, d), lambda i, f: (0, 0)),
            pl.BlockSpec((None, d, tf), lambda i, f: (layer, 0, up(f))),
            pl.BlockSpec((None, d, tf), lambda i, f: (layer, 0, n_up + up(f))),
            pl.BlockSpec((FFN_CONV, tf), lambda i, f: (0, cv(f))),
            pl.BlockSpec((FFN_CONV, tf), lambda i, f: (0, n_up + cv(f))),
            pl.BlockSpec((1, tf), lambda i, f: (0, cv(f))),
            pl.BlockSpec((1, tf), lambda i, f: (0, n_up + cv(f))),
        ],
        out_specs=pl.BlockSpec((tm, tf), lambda i, f: (i, cv(f))),
        out_shape=jax.ShapeDtypeStruct((t, D_FF), BF16),
        scratch_shapes=[
            pltpu.VMEM((tm, d), BF16),
            pltpu.VMEM((m_rows, d), BF16),
            pltpu.VMEM((m_rows, tf), F32),
            pltpu.VMEM((m_rows, tf), F32),
            pltpu.VMEM((m_rows, tf), F32),
            pltpu.VMEM((m_rows, tf), F32),
        ],
        compiler_params=pltpu.CompilerParams(
            dimension_semantics=("parallel", "arbitrary"), vmem_limit_bytes=VMEM_LIMIT),
        name="conv_glu_up",
    )(x2d, x2d, nw, w_up, w_up, conv_w, conv_w, conv_b, conv_b)
    tm = ROW_TILE
    return pl.pallas_call(
        _conv_glu_down_kernel,
        grid=(t // tm,),
        in_specs=[
            pl.BlockSpec((tm, D_FF), lambda i: (i, 0)),
            pl.BlockSpec((tm, d), lambda i: (i, 0)),
            pl.BlockSpec((None, D_FF, d), lambda i: (layer, 0, 0), pipeline_mode=pl.Buffered(1)),
        ],
        out_specs=pl.BlockSpec((tm, d), lambda i: (i, 0)),
        out_shape=jax.ShapeDtypeStruct((t, d), F32),
        scratch_shapes=[pltpu.VMEM((COL_TILE // LANES, tm, LANES), F32)],
        compiler_params=pltpu.CompilerParams(
            dimension_semantics=("parallel",), vmem_limit_bytes=VMEM_LIMIT),
        name="conv_glu_down",
    )(act, x2d, w_down)


def _pad_lanes(v):
    return jnp.pad(v.reshape(1, -1), ((0, 0), (0, LANES - v.shape[-1])))


def _cast_kernel(w_ref, o_ref):
    o_ref[...] = w_ref[...].astype(BF16)


def _to_bf16(w):
    depth, k, n = w.shape
    return pl.pallas_call(
        _cast_kernel,
        grid=(depth, k // CAST_ROWS),
        in_specs=[pl.BlockSpec((None, CAST_ROWS, n), lambda l, r: (l, r, 0))],
        out_specs=pl.BlockSpec((None, CAST_ROWS, n), lambda l, r: (l, r, 0)),
        out_shape=jax.ShapeDtypeStruct(w.shape, BF16),
        compiler_params=pltpu.CompilerParams(
            dimension_semantics=("parallel", "parallel"), vmem_limit_bytes=VMEM_LIMIT),
        name="cast_bf16",
    )(w)


def _layer(x2d, batch, seq, layer, w_in_b, w_dt_b, w_out_b, w_up_b, w_down_b, norm1_w, q_norm_w, k_norm_w,
           ssm_conv_w, ssm_conv_b, dt_bias, a_log, d_skip, ssm_norm_w, norm2_w, ffn_conv_w, ffn_conv_b):
    qkv, zxbc, k_mean, dt_raw = _in_proj(x2d, norm1_w.reshape(1, -1), w_in_b, layer, w_dt_b,
                                         q_norm_w.reshape(1, -1), k_norm_w.reshape(1, -1))

    attn = _moba(qkv.reshape(3 * ATTN_HEADS, batch, seq, ATTN_HEAD_DIM),
                 k_mean.reshape(batch, seq // MOBA_BLOCK, ATTN_WIDTH))
    ssm = _ssd(zxbc.reshape(batch, seq, -1), dt_raw.reshape(batch, seq, LANES), ssm_conv_w,
               ssm_conv_b.reshape(1, -1), _pad_lanes(dt_bias), _pad_lanes(a_log),
               jnp.repeat(d_skip, SSM_HEAD_DIM).reshape(1, -1), ssm_norm_w.reshape(1, -1))

    x1 = _out_proj(x2d, attn.reshape(ATTN_HEADS, -1, ATTN_HEAD_DIM), ssm.reshape(-1, SSM_WIDTH), w_out_b, layer)
    return _conv_glu(x1, norm2_w.reshape(1, -1), w_up_b, ffn_conv_w, ffn_conv_b.reshape(1, -1), w_down_b,
                     layer, seq)


def kernel(x, norm1_w, w_in, q_norm_w, k_norm_w, ssm_conv_w, ssm_conv_b, dt_bias, a_log, d_skip, ssm_norm_w,
           w_out, norm2_w, w_up, ffn_conv_w, ffn_conv_b, w_down):
    batch, seq, d = x.shape
    x2d = x.reshape(batch * seq, d)
    w_in_b = w_in[:, :, :PROJ_MAIN].astype(BF16)
    w_dt_b = jnp.pad(w_in[:, :, PROJ_MAIN:], ((0, 0), (0, 0), (0, LANES - SSM_HEADS))).astype(BF16)
    w_out_b, w_up_b, w_down_b = _to_bf16(w_out), _to_bf16(w_up), _to_bf16(w_down)
    for i in range(norm1_w.shape[0]):
        x2d = _layer(x2d, batch, seq, i, w_in_b, w_dt_b, w_out_b, w_up_b, w_down_b, norm1_w[i], q_norm_w[i],
                     k_norm_w[i], ssm_conv_w[i], ssm_conv_b[i], dt_bias[i], a_log[i], d_skip[i], ssm_norm_w[i],
                     norm2_w[i], ffn_conv_w[i], ffn_conv_b[i])
    return x2d.reshape(batch, seq, d)
```
